```python
import math
import jax
import jax.numpy as jnp
from jax import lax
import numpy as np

D_MODEL = 4096
BATCH = 4
SEQ = 2048
DEPTH = 2
DEC_BATCH = 8
DEC_SEQ = 1
PAST_LEN = 16384
PAGE_SIZE = 128

MIX_WIDTH = D_MODEL
FOX_HEADS = 8
FOX_HEAD_DIM = MIX_WIDTH // (4 * FOX_HEADS)
FOX_WIDTH = FOX_HEADS * FOX_HEAD_DIM
FOX_BLOCK = 128
FOX_GATE_BIAS = 4.0
HG_HEAD_DIM = 128
HG_WIDTH = MIX_WIDTH // 4
HG_HEADS = HG_WIDTH // HG_HEAD_DIM
HG_CHUNK = 64
SSM_WIDTH = MIX_WIDTH - FOX_WIDTH - HG_WIDTH
SSM_HEAD_DIM = 64
SSM_HEADS = SSM_WIDTH // SSM_HEAD_DIM
SSM_GROUPS = 8
SSM_HEADS_PER_GROUP = SSM_HEADS // SSM_GROUPS
SSM_STATE = 128
SSM_CHUNK = 128
CONV_WIDTH = 4
CONV_DIM = SSM_WIDTH + 2 * SSM_GROUPS * SSM_STATE
N_MEM = 256
XA_HEADS = 4
XA_HEAD_DIM = 128
XA_WIDTH = XA_HEADS * XA_HEAD_DIM
D_FF = 4 * D_MODEL
IN_SPLITS = (FOX_WIDTH, FOX_WIDTH, FOX_WIDTH, FOX_HEADS,
             HG_WIDTH, HG_WIDTH, HG_WIDTH, HG_WIDTH,
             SSM_WIDTH, CONV_DIM, SSM_HEADS)
IN_WIDTH = sum(IN_SPLITS)
EPS = 1e-6
DT_MIN = 1e-3
DT_MAX = 1e-1
MASK_VALUE = -1e30

kernel_name = 'hymba_fox_hgrn2_ssd_decoder_step'


def rmsnorm(x, g):
    xf = x.astype(jnp.float32)
    y = xf * lax.rsqrt(jnp.mean(jnp.square(xf), axis=-1, keepdims=True) + EPS)
    return (y * g.astype(jnp.float32)).astype(x.dtype)


def split_in_proj(h):
    idx = np.cumsum(IN_SPLITS)[:-1].tolist()
    return jnp.split(h, idx, axis=-1)


def pad_time(a, n_pad):
    return jnp.pad(a, [(0, 0), (0, n_pad)] + [(0, 0)] * (a.ndim - 2))


def to_chunks(a, c):
    b, t = a.shape[0], a.shape[1]
    return jnp.moveaxis(a.reshape((b, t // c, c) + a.shape[2:]), 1, 0)


def from_chunks(a, t):
    n, b, c = a.shape[0], a.shape[1], a.shape[2]
    return jnp.moveaxis(a, 0, 1).reshape((b, n * c) + a.shape[3:])[:, :t]


def gather_pages(pool, page_table):
    g = pool[page_table]
    return g.reshape((g.shape[0], g.shape[1] * g.shape[2]) + g.shape[3:])


def masked_decay(mask, diff):
    return jnp.where(mask, jnp.exp(jnp.where(mask, diff, 0.0)), 0.0)


def hgrn2_chunked(q, k, v, logf, s0):
    t = q.shape[1]
    c = min(HG_CHUNK, t)
    n_pad = -(-t // c) * c - t
    q, k, v, logf = [to_chunks(pad_time(a, n_pad), c) for a in (q, k, v, logf)]
    mask = jnp.tril(jnp.ones((c, c), dtype=bool))[None, :, :, None, None]

    def step(s, inp):
        qc, kc, vc, lc = inp
        b = jnp.cumsum(lc, axis=1)
        dec = masked_decay(mask, b[:, :, None] - b[:, None, :])
        att = jnp.einsum('bthk,bshk,btshk->bhts', qc, kc, dec)
        o = (jnp.einsum('bhts,bshv->bthv', att, vc)
             + jnp.einsum('bthk,bhkv->bthv', qc * jnp.exp(b), s))
        b_end = b[:, -1]
        s = (s * jnp.exp(b_end)[..., None]
             + jnp.einsum('bshk,bshv->bhkv', kc * jnp.exp(b_end[:, None] - b), vc))
        return s, o

    s_fin, o = lax.scan(step, s0, (q, k, v, logf))
    return from_chunks(o, t), s_fin


def ssd_chunked(x, dt, a_neg, bm, cm, s0):
    t = x.shape[1]
    c = min(SSM_CHUNK, t)
    n_pad = -(-t // c) * c - t
    x, dt, bm, cm = [to_chunks(pad_time(a, n_pad), c) for a in (x, dt, bm, cm)]
    mask = jnp.tril(jnp.ones((c, c), dtype=bool))[None, :, :, None, None]

    def step(s, inp):
        xc, dtc, bc, cc = inp
        a = jnp.cumsum(dtc * a_neg, axis=1)
        seg = masked_decay(mask, a[:, :, None] - a[:, None, :])
        cb = jnp.einsum('btgn,bsgn->btsg', cc, bc)
        y = (jnp.einsum('btsg,btsgh,bsgh,bsghp->btghp', cb, seg, dtc, xc)
             + jnp.einsum('btgn,bghpn,btgh->btghp', cc, s, jnp.exp(a)))
        a_end = a[:, -1]
        s = (s * jnp.exp(a_end)[..., None, None]
             + jnp.einsum('bsgn,bsgh,bsghp->bghpn', bc, dtc * jnp.exp(a_end[:, None] - a), xc))
        return s, y

    s_fin, y = lax.scan(step, s0, (x, dt, bm, cm))
    return from_chunks(y, t), s_fin


def fox_prompt(q, k, v, logf):
    bsz, t = q.shape[0], q.shape[1]
    scale = FOX_HEAD_DIM ** -0.5
    cum = jnp.cumsum(logf, axis=1).transpose(0, 2, 1)
    n_blk = t // FOX_BLOCK
    q_blk = to_chunks(q, FOX_BLOCK)
    c_blk = jnp.moveaxis(cum.reshape(bsz, FOX_HEADS, n_blk, FOX_BLOCK), 2, 0)
    q_pos = jnp.arange(t).reshape(n_blk, FOX_BLOCK)
    k_pos = jnp.arange(t)

    def block(inp):
        qb, cb, pb = inp
        s = (jnp.einsum('bqhd,bkhd->bhqk', qb, k) * scale
             + cb[..., :, None] - cum[:, :, None, :])
        s = jnp.where(pb[:, None] >= k_pos[None, :], s, MASK_VALUE)
        p = jax.nn.softmax(s, axis=-1)
        return jnp.einsum('bhqk,bkhd->bqhd', p, v)

    o = lax.map(block, (q_blk, c_blk, q_pos))
    return from_chunks(o, t)


def fox_sample(q, k_new, v_new, logf_new, k_past, v_past, logf_past):
    f32 = jnp.float32
    scale = FOX_HEAD_DIM ** -0.5
    tn = q.shape[1]
    n_past = k_past.shape[1]
    cn = jnp.cumsum(logf_new, axis=1).transpose(0, 2, 1)
    suffix = lax.cumsum(logf_past.astype(f32), axis=1, reverse=True)
    after = jnp.concatenate([suffix[:, 1:], jnp.zeros_like(suffix[:, :1])], axis=1)
    after = after.transpose(0, 2, 1)
    s_past = (jnp.einsum('bqhd,bkhd->bhqk', q, k_past.astype(f32)) * scale
              + cn[..., :, None] + after[:, :, None, :])
    causal = jnp.tril(jnp.ones((tn, tn), dtype=bool))
    s_new = (jnp.einsum('bqhd,bkhd->bhqk', q, k_new) * scale
             + cn[..., :, None] - cn[..., None, :])
    s_new = jnp.where(causal, s_new, MASK_VALUE)
    p = jax.nn.softmax(jnp.concatenate([s_past, s_new], axis=-1), axis=-1)
    return (jnp.einsum('bhqk,bkhd->bqhd', p[..., :n_past], v_past.astype(f32))
            + jnp.einsum('bhqk,bkhd->bqhd', p[..., n_past:], v_new))


def causal_conv(xbc, conv_state, w, b):
    f32 = jnp.float32
    t = xbc.shape[1]
    xp = jnp.concatenate([conv_state.astype(f32), xbc.astype(f32)], axis=1)
    w = w.astype(f32)
    out = sum(xp[:, j:j + t] * w[j] for j in range(CONV_WIDTH)) + b.astype(f32)
    return jax.nn.silu(out), xp[:, t:]


def token_mixers(h, fox_past, hg_s0, ssm_s0, conv_s0, w_in, fox_gq, fox_gk, fox_bf, hg_lb,
                 hg_gnorm, conv_w, conv_b, dt_bias, a_log, d_skip, ssm_gnorm, w_out):
    f32 = jnp.float32
    bsz, t = h.shape[0], h.shape[1]
    fq, fk, fv, ffg, hq, hf, hi, hgate, z, xbc, dt = split_in_proj(h @ w_in)

    fq = rmsnorm(fq.reshape(bsz, t, FOX_HEADS, FOX_HEAD_DIM), fox_gq).astype(f32)
    fk = rmsnorm(fk.reshape(bsz, t, FOX_HEADS, FOX_HEAD_DIM), fox_gk).astype(f32)
    fv = fv.reshape(bsz, t, FOX_HEADS, FOX_HEAD_DIM).astype(f32)
    flogf = jax.nn.log_sigmoid(ffg.astype(f32) + fox_bf.astype(f32))
    if fox_past is None:
        fo = fox_prompt(fq, fk, fv, flogf)
    else:
        fo = fox_sample(fq, fk, fv, flogf, *fox_past)

    hf = hf.astype(f32).reshape(bsz, t, HG_HEADS, HG_HEAD_DIM)
    lb = hg_lb.astype(f32).reshape(HG_HEADS, HG_HEAD_DIM)
    hlogf = jnp.log(lb + (1.0 - lb) * jax.nn.sigmoid(hf))
    hk = (1.0 - lb) * jax.nn.sigmoid(-hf)
    hq = jax.nn.silu(hq.astype(f32)).reshape(bsz, t, HG_HEADS, HG_HEAD_DIM)
    hv = hi.astype(f32).reshape(bsz, t, HG_HEADS, HG_HEAD_DIM)
    ho, hg_state = hgrn2_chunked(hq, hk, hv, hlogf, hg_s0.astype(f32))
    ho = rmsnorm(ho, hg_gnorm) * jax.nn.silu(hgate.astype(f32).reshape(bsz, t, HG_HEADS, HG_HEAD_DIM))

    xbc_act, conv_state = causal_conv(xbc, conv_s0, conv_w, conv_b)
    sx, sb, sc = jnp.split(xbc_act, [SSM_WIDTH, SSM_WIDTH + SSM_GROUPS * SSM_STATE], axis=-1)
    sx = sx.reshape(bsz, t, SSM_GROUPS, SSM_HEADS_PER_GROUP, SSM_HEAD_DIM)
    sb = sb.reshape(bsz, t, SSM_GROUPS, SSM_STATE)
    sc = sc.reshape(bsz, t, SSM_GROUPS, SSM_STATE)
    sdt = jax.nn.softplus(dt.astype(f32) + dt_bias.astype(f32)).reshape(bsz, t, SSM_GROUPS, SSM_HEADS_PER_GROUP)
    a_neg = -jnp.exp(a_log.astype(f32)).reshape(SSM_GROUPS, SSM_HEADS_PER_GROUP)
    s0 = ssm_s0.astype(f32).reshape(bsz, SSM_GROUPS, SSM_HEADS_PER_GROUP, SSM_HEAD_DIM, SSM_STATE)
    sy, ssm_state = ssd_chunked(sx, sdt, a_neg, sb, sc, s0)
    sy = sy + d_skip.astype(f32).reshape(SSM_GROUPS, SSM_HEADS_PER_GROUP)[..., None] * sx
    sy = sy.reshape(bsz, t, SSM_WIDTH) * jax.nn.silu(z.astype(f32))
    sy = rmsnorm(sy.reshape(bsz, t, SSM_GROUPS, SSM_WIDTH // SSM_GROUPS),
                 ssm_gnorm.reshape(SSM_GROUPS, SSM_WIDTH // SSM_GROUPS)).reshape(bsz, t, SSM_WIDTH)

    mixed = jnp.concatenate([fo.reshape(bsz, t, FOX_WIDTH), ho.reshape(bsz, t, HG_WIDTH), sy], axis=-1)
    out = mixed.astype(h.dtype) @ w_out
    ssm_state = ssm_state.reshape(bsz, SSM_HEADS, SSM_HEAD_DIM, SSM_STATE)
    return out, fk, fv, flogf, hg_state, ssm_state, conv_state


def memory_kv(mem, g_mem, wk, wv, gk):
    m = rmsnorm(mem, g_mem)
    b, n = m.shape[0], m.shape[1]
    mk = rmsnorm((m @ wk).reshape(b, n, XA_HEADS, XA_HEAD_DIM), gk)
    mv = (m @ wv).reshape(b, n, XA_HEADS, XA_HEAD_DIM)
    return mk, mv


def cross_attn(h, mk, mv, wq, gq, wo):
    f32 = jnp.float32
    b, t = h.shape[0], h.shape[1]
    q = rmsnorm((h @ wq).reshape(b, t, XA_HEADS, XA_HEAD_DIM), gq).astype(f32)
    s = jnp.einsum('bqhd,bkhd->bhqk', q, mk.astype(f32)) * XA_HEAD_DIM ** -0.5
    p = jax.nn.softmax(s, axis=-1)
    o = jnp.einsum('bhqk,bkhd->bqhd', p, mv.astype(f32)).reshape(b, t, XA_WIDTH)
    return o.astype(h.dtype) @ wo


def sq_relu_mlp(h, w_up, w_down):
    return jnp.square(jax.nn.relu(h @ w_up)) @ w_down


def setup_inputs(seed: int = 0) -> dict:
    key = jax.random.key(seed)
    ks = iter(jax.random.split(key, 48))
    f32 = jnp.float32

    def nrm(shape, scale=1.0):
        return jax.random.normal(next(ks), shape, f32) * scale

    def gain(shape):
        return 1.0 + nrm(shape, 0.02)

    n_pages = PAST_LEN // PAGE_SIZE
    n_pool = (DEC_BATCH * n_pages * 5) // 4
    perm = jax.random.permutation(next(ks), n_pool)
    page_table = perm[:DEC_BATCH * n_pages].reshape(DEC_BATCH, n_pages).astype(jnp.int32)
    dt0 = jnp.exp(jax.random.uniform(next(ks), (DEPTH, SSM_HEADS), f32, math.log(DT_MIN), math.log(DT_MAX)))
    dt_bias = dt0 + jnp.log(-jnp.expm1(-dt0))
    a_log = jnp.log(jax.random.uniform(next(ks), (DEPTH, SSM_HEADS), f32, 1.0, 16.0))
    return {
        'x_prompt': nrm((BATCH, SEQ, D_MODEL)),
        'x_sample': nrm((DEC_BATCH, DEC_SEQ, D_MODEL)),
        'cache_fox_k': nrm((DEPTH, n_pool, PAGE_SIZE, FOX_HEADS, FOX_HEAD_DIM)),
        'cache_fox_v': nrm((DEPTH, n_pool, PAGE_SIZE, FOX_HEADS, FOX_HEAD_DIM)),
        'cache_fox_logf': jax.nn.log_sigmoid(FOX_GATE_BIAS + nrm((DEPTH, n_pool, PAGE_SIZE, FOX_HEADS))),
        'cache_mem_k': nrm((DEPTH, DEC_BATCH, N_MEM, XA_HEADS, XA_HEAD_DIM)),
        'cache_mem_v': nrm((DEPTH, DEC_BATCH, N_MEM, XA_HEADS, XA_HEAD_DIM)),
        'state_hgrn': nrm((DEPTH, DEC_BATCH, HG_HEADS, HG_HEAD_DIM, HG_HEAD_DIM), 0.5),
        'state_ssm': nrm((DEPTH, DEC_BATCH, SSM_HEADS, SSM_HEAD_DIM, SSM_STATE), 0.1),
        'state_conv': nrm((DEPTH, DEC_BATCH, CONV_WIDTH - 1, CONV_DIM)),
        'page_table': page_table,
        'mem_prompt': nrm((BATCH, N_MEM, D_MODEL)),
        'norm_mix': gain((DEPTH, D_MODEL)),
        'w_in': nrm((DEPTH, D_MODEL, IN_WIDTH), D_MODEL ** -0.5),
        'fox_gq': gain((DEPTH, FOX_HEAD_DIM)),
        'fox_gk': gain((DEPTH, FOX_HEAD_DIM)),
        'fox_bf': FOX_GATE_BIAS + nrm((DEPTH, FOX_HEADS), 0.1),
        'hg_lb_logits': nrm((DEPTH, HG_WIDTH), 0.1),
        'hg_gnorm': gain((DEPTH, HG_HEAD_DIM)),
        'conv_w': nrm((DEPTH, CONV_WIDTH, CONV_DIM), CONV_WIDTH ** -0.5),
        'conv_b': nrm((DEPTH, CONV_DIM), 0.01),
        'dt_bias': dt_bias,
        'a_log': a_log,
        'd_skip': gain((DEPTH, SSM_HEADS)),
        'ssm_gnorm': gain((DEPTH, SSM_WIDTH)),
        'w_out': nrm((DEPTH, MIX_WIDTH, D_MODEL), MIX_WIDTH ** -0.5),
        'norm_xa': gain((DEPTH, D_MODEL)),
        'norm_mem': gain((DEPTH, D_MODEL)),
        'xa_wq': nrm((DEPTH, D_MODEL, XA_WIDTH), D_MODEL ** -0.5),
        'xa_wk': nrm((DEPTH, D_MODEL, XA_WIDTH), D_MODEL ** -0.5),
        'xa_wv': nrm((DEPTH, D_MODEL, XA_WIDTH), D_MODEL ** -0.5),
        'xa_gq': gain((DEPTH, XA_HEAD_DIM)),
        'xa_gk': gain((DEPTH, XA_HEAD_DIM)),
        'xa_wo': nrm((DEPTH, XA_WIDTH, D_MODEL), XA_WIDTH ** -0.5),
        'norm_mlp': gain((DEPTH, D_MODEL)),
        'w_up': nrm((DEPTH, D_MODEL, D_FF), D_MODEL ** -0.5),
        'w_down': nrm((DEPTH, D_FF, D_MODEL), D_FF ** -0.5),
    }


def reference(x_prompt, x_sample, cache_fox_k, cache_fox_v, cache_fox_logf, cache_mem_k, cache_mem_v,
              state_hgrn, state_ssm, state_conv, page_table, mem_prompt, norm_mix, w_in, fox_gq, fox_gk,
              fox_bf, hg_lb_logits, hg_gnorm, conv_w, conv_b, dt_bias, a_log, d_skip, ssm_gnorm, w_out,
              norm_xa, norm_mem, xa_wq, xa_wk, xa_wv, xa_gq, xa_gk, xa_wo, norm_mlp, w_up, w_down):
    f32 = jnp.float32
    lb_probs = jax.nn.softmax(hg_lb_logits.astype(f32), axis=0)
    lower_bounds = jnp.cumsum(lb_probs, axis=0) - lb_probs[0:1]

    xp, xs = x_prompt, x_sample
    bp, bs = x_prompt.shape[0], x_sample.shape[0]
    p_fk, p_fv, p_fl, p_hg, p_ss, p_cv, p_mk, p_mv = [], [], [], [], [], [], [], []
    s_fk, s_fv, s_fl, s_hg, s_ss, s_cv = [], [], [], [], [], []
    for l in range(DEPTH):
        mw = (w_in[l], fox_gq[l], fox_gk[l], fox_bf[l], lower_bounds[l], hg_gnorm[l], conv_w[l], conv_b[l],
              dt_bias[l], a_log[l], d_skip[l], ssm_gnorm[l], w_out[l])

        out, fk, fv, fl, hg, ss, cv = token_mixers(
            rmsnorm(xp, norm_mix[l]), None,
            jnp.zeros((bp, HG_HEADS, HG_HEAD_DIM, HG_HEAD_DIM), f32),
            jnp.zeros((bp, SSM_HEADS, SSM_HEAD_DIM, SSM_STATE), f32),
            jnp.zeros((bp, CONV_WIDTH - 1, CONV_DIM), f32), *mw)
        xp = xp + out
        mk, mv = memory_kv(mem_prompt, norm_mem[l], xa_wk[l], xa_wv[l], xa_gk[l])
        xp = xp + cross_attn(rmsnorm(xp, norm_xa[l]), mk, mv, xa_wq[l], xa_gq[l], xa_wo[l])
        xp = xp + sq_relu_mlp(rmsnorm(xp, norm_mlp[l]), w_up[l], w_down[l])
        p_fk.append(fk); p_fv.append(fv); p_fl.append(fl); p_hg.append(hg); p_ss.append(ss); p_cv.append(cv)
        p_mk.append(mk); p_mv.append(mv)

        fox_past = (gather_pages(cache_fox_k[l], page_table),
                    gather_pages(cache_fox_v[l], page_table),
                    gather_pages(cache_fox_logf[l], page_table))
        out, fk, fv, fl, hg, ss, cv = token_mixers(
            rmsnorm(xs, norm_mix[l]), fox_past, state_hgrn[l], state_ssm[l], state_conv[l], *mw)
        xs = xs + out
        xs = xs + cross_attn(rmsnorm(xs, norm_xa[l]), cache_mem_k[l], cache_mem_v[l], xa_wq[l], xa_gq[l], xa_wo[l])
        xs = xs + sq_relu_mlp(rmsnorm(xs, norm_mlp[l]), w_up[l], w_down[l])
        s_fk.append(fk); s_fv.append(fv); s_fl.append(fl); s_hg.append(hg); s_ss.append(ss); s_cv.append(cv)

    y_prompt, y_sample = xp, xs
    p_fox_k, p_fox_v, p_fox_logf = jnp.stack(p_fk), jnp.stack(p_fv), jnp.stack(p_fl)
    p_hgrn, p_ssm, p_conv = jnp.stack(p_hg), jnp.stack(p_ss), jnp.stack(p_cv)
    p_mem_k, p_mem_v = jnp.stack(p_mk), jnp.stack(p_mv)
    s_fox_k, s_fox_v, s_fox_logf = jnp.stack(s_fk), jnp.stack(s_fv), jnp.stack(s_fl)
    s_hgrn, s_ssm, s_conv = jnp.stack(s_hg), jnp.stack(s_ss), jnp.stack(s_cv)
    return (y_prompt, y_sample, p_fox_k, p_fox_v, p_fox_logf, p_hgrn, p_ssm, p_conv, p_mem_k, p_mem_v,
            s_fox_k, s_fox_v, s_fox_logf, s_hgrn, s_ssm, s_conv)
```

```python
import functools

import jax
import jax.numpy as jnp
from jax import lax
from jax.experimental import pallas as pl
from jax.experimental.pallas import tpu as pltpu

F32 = jnp.float32
BF16 = jnp.bfloat16

EPS = 1e-6
MASK_VALUE = -1e30
HEAD = 128
N_HEADS = 8
FOX_W = N_HEADS * HEAD
HG_W = N_HEADS * HEAD
SSM_W = 2048
SSM_P = 64
SSM_HEADS = SSM_W // SSM_P
SSM_GROUPS = 8
SSM_HPG = SSM_HEADS // SSM_GROUPS
SSM_N = 128
SSM_GW = SSM_HPG * SSM_P
CONV_W = 4
CONV_DIM = SSM_W + 2 * SSM_GROUPS * SSM_N
XA_HEADS = 4
XA_W = XA_HEADS * HEAD
HG_CHUNK = 64
HG_SUB = 16
SSD_CHUNK = 128
PAGE = 128
DT_LANE0 = N_HEADS

COL_XBC = 0
COL_HG = CONV_DIM
COL_Z = COL_HG + 4 * HG_W
COL_FQ = COL_Z + SSM_W
COL_FK = COL_FQ + FOX_W
COL_FV = COL_FK + FOX_W
MAIN_W = COL_FV + FOX_W

V7X_VMEM_LIMIT = 56 * 1024 * 1024


def _cparams(sem, vmem=V7X_VMEM_LIMIT):
    return pltpu.CompilerParams(dimension_semantics=sem, vmem_limit_bytes=vmem)


def _sigmoid(x):
    return 1.0 / (1.0 + jnp.exp(-x))


def _silu(x):
    return x * _sigmoid(x)


def _softplus(x):
    return jnp.maximum(x, 0.0) + jnp.log1p(jnp.exp(-jnp.abs(x)))


def _log_sigmoid(x):
    return -_softplus(-x)


def _dot_nt(a, b):
    return lax.dot_general(a, b, (((1,), (1,)), ((), ())), preferred_element_type=F32)


def _dot_tn(a, b):
    return lax.dot_general(a, b, (((0,), (0,)), ((), ())), preferred_element_type=F32)


def _dot(a, b):
    return jnp.dot(a, b, preferred_element_type=F32)


def _tri_incl(n):
    r = lax.broadcasted_iota(jnp.int32, (n, n), 0)
    c = lax.broadcasted_iota(jnp.int32, (n, n), 1)
    return r >= c


def _cumsum_rows(x):
    n = x.shape[0]
    return jnp.dot(_tri_incl(n).astype(F32), x, preferred_element_type=F32,
                   precision=lax.Precision.HIGHEST)


def _mm_body(*refs, nk, norm, act, has_res):
    it = iter(refs)
    a_ref = next(it)
    g_ref = next(it) if norm else None
    w_ref = next(it)
    r_ref = next(it) if has_res else None
    o_ref = next(it)
    h_ref = next(it) if norm else None
    acc_ref = next(it) if nk > 1 else None
    j = pl.program_id(1)
    k = pl.program_id(2)

    if norm:
        @pl.when(j == 0)
        def _():
            x = a_ref[...].astype(F32)
            ms = jnp.mean(x * x, axis=-1, keepdims=True)
            h_ref[...] = (x * lax.rsqrt(ms + EPS) * g_ref[...]).astype(BF16)
        a = h_ref[...]
    else:
        a = a_ref[...]

    p = _dot(a, w_ref[...])

    def finish(v):
        if act == "relu2":
            v = jnp.square(jnp.maximum(v, 0.0))
        if has_res:
            v = v + r_ref[...]
        o_ref[...] = v.astype(o_ref.dtype)

    if nk == 1:
        finish(p)
    else:
        @pl.when(k == 0)
        def _():
            acc_ref[...] = p

        @pl.when(k > 0)
        def _():
            acc_ref[...] += p

        @pl.when(k == nk - 1)
        def _():
            finish(acc_ref[...])


def _pick_tile(n, candidates):
    for c in candidates:
        if n % c == 0:
            return c
    return n


def _matmul(a, w, *, gain=None, res=None, act=None, out_dtype=F32, tk=None, name="matmul"):
    m, kdim = a.shape
    n = w.shape[1]
    small_m = m <= 64
    tm = m if small_m else _pick_tile(m, (512, 256, 128))
    tn = _pick_tile(n, (2048, 1024, 512, 256, 128)) if small_m else _pick_tile(n, (1024, 512, 256, 128))
    tk = kdim if tk is None else tk
    nk = kdim // tk
    norm = gain is not None
    assert not (norm and nk > 1)
    in_specs = [pl.BlockSpec((tm, tk), lambda i, j, k: (i, k))]
    args = [a]
    if norm:
        in_specs.append(pl.BlockSpec((1, kdim), lambda i, j, k: (0, 0)))
        args.append(gain.reshape(1, kdim).astype(F32))
    in_specs.append(pl.BlockSpec((tk, tn), lambda i, j, k: (k, j)))
    args.append(w)
    if res is not None:
        in_specs.append(pl.BlockSpec((tm, tn), lambda i, j, k: (i, j)))
        args.append(res)
    scratch = []
    if norm:
        scratch.append(pltpu.VMEM((tm, kdim), BF16))
    if nk > 1:
        scratch.append(pltpu.VMEM((tm, tn), F32))
    return pl.pallas_call(
        functools.partial(_mm_body, nk=nk, norm=norm, act=act, has_res=res is not None),
        grid=(m // tm, n // tn, nk),
        in_specs=in_specs,
        out_specs=pl.BlockSpec((tm, tn), lambda i, j, k: (i, j)),
        out_shape=jax.ShapeDtypeStruct((m, n), out_dtype),
        scratch_shapes=scratch,
        compiler_params=_cparams(("parallel", "arbitrary", "arbitrary")),
        name=name,
    )(*args)


def _headnorm_body(x_ref, g_ref, o_ref, *, n_heads):
    g = g_ref[...]
    for h in range(n_heads):
        sl = slice(h * HEAD, (h + 1) * HEAD)
        x = x_ref[:, sl].astype(F32)
        ms = jnp.mean(x * x, axis=-1, keepdims=True)
        o_ref[:, sl] = (x * lax.rsqrt(ms + EPS) * g).astype(o_ref.dtype)


def _headnorm(x, gain, *, col0, width, out_dtype=F32):
    m = x.shape[0]
    tm = m if m <= 64 else _pick_tile(m, (512, 256, 128))
    return pl.pallas_call(
        functools.partial(_headnorm_body, n_heads=width // HEAD),
        grid=(m // tm,),
        in_specs=[pl.BlockSpec((tm, width), lambda i: (i, col0 // width)),
                  pl.BlockSpec((1, HEAD), lambda i: (0, 0))],
        out_specs=pl.BlockSpec((tm, width), lambda i: (i, 0)),
        out_shape=jax.ShapeDtypeStruct((m, width), out_dtype),
        compiler_params=_cparams(("parallel",)),
        name="headnorm",
    )(x, gain.reshape(1, HEAD).astype(F32))


def _fox_gate_body(g_ref, bf_ref, lf_ref, cum_ref, *, t):
    lf = _log_sigmoid(g_ref[0] + bf_ref[...])
    lf_ref[0] = lf
    lane = lax.broadcasted_iota(jnp.int32, lf.shape, 1)
    c = lf
    shift = 1
    while shift < t:
        c = c + jnp.where(lane >= shift, pltpu.roll(c, shift, axis=1), 0.0)
        shift *= 2
    cum_ref[0] = c


def _fox_gate(gates_t, fox_bf):
    b, h, t = gates_t.shape
    spec = pl.BlockSpec((1, h, t), lambda i: (i, 0, 0))
    return pl.pallas_call(
        functools.partial(_fox_gate_body, t=t),
        grid=(b,),
        in_specs=[spec, pl.BlockSpec((h, 1), lambda i: (0, 0))],
        out_specs=[spec, spec],
        out_shape=[jax.ShapeDtypeStruct((b, h, t), F32)] * 2,
        compiler_params=_cparams(("parallel",)),
        name="fox_gate",
    )(gates_t, fox_bf.reshape(h, 1).astype(F32))


def _fox_flash_body(q_ref, k_ref, v_ref, cq_ref, ck_ref, o_ref, m_ref, l_ref, acc_ref, *, tq):
    qi = pl.program_id(1)
    ki = pl.program_id(2)
    scale = HEAD ** -0.5

    @pl.when(ki == 0)
    def _():
        m_ref[...] = jnp.full(m_ref.shape, MASK_VALUE, F32)
        l_ref[...] = jnp.zeros(l_ref.shape, F32)
        acc_ref[...] = jnp.zeros(acc_ref.shape, F32)

    @pl.when(ki <= qi)
    def _():
        row = lax.broadcasted_iota(jnp.int32, (tq, tq), 0)
        col = lax.broadcasted_iota(jnp.int32, (tq, tq), 1)
        keep = jnp.logical_or(ki < qi, row >= col)
        for h in range(N_HEADS):
            sl = slice(h * HEAD, (h + 1) * HEAD)
            s = _dot_nt(q_ref[0, :, sl], k_ref[0, :, sl].astype(BF16)) * scale
            s = s + cq_ref[0, :, h:h + 1] - ck_ref[0, h:h + 1, :]
            s = jnp.where(keep, s, MASK_VALUE)
            m_prev = m_ref[h]
            m_new = jnp.maximum(m_prev, jnp.max(s, axis=-1, keepdims=True))
            alpha = jnp.exp(m_prev - m_new)
            p = jnp.exp(s - m_new)
            l_ref[h] = alpha * l_ref[h] + jnp.sum(p, axis=-1, keepdims=True)
            acc_ref[:, sl] = alpha * acc_ref[:, sl] + _dot(p.astype(BF16), v_ref[0, :, sl].astype(BF16))
            m_ref[h] = m_new

    @pl.when(ki == qi)
    def _():
        for h in range(N_HEADS):
            sl = slice(h * HEAD, (h + 1) * HEAD)
            o_ref[0, :, sl] = (acc_ref[:, sl] / l_ref[h]).astype(o_ref.dtype)


def _fox_flash(qn, kn, proj, cum_col, cum_row, *, tq=512):
    b, t, _ = qn.shape
    nq = t // tq
    return pl.pallas_call(
        functools.partial(_fox_flash_body, tq=tq),
        grid=(b, nq, nq),
        in_specs=[
            pl.BlockSpec((1, tq, FOX_W), lambda i, q, k: (i, q, 0)),
            pl.BlockSpec((1, tq, FOX_W), lambda i, q, k: (i, jnp.minimum(k, q), 0)),
            pl.BlockSpec((1, tq, FOX_W), lambda i, q, k: (i, jnp.minimum(k, q), COL_FV // FOX_W)),
            pl.BlockSpec((1, tq, N_HEADS), lambda i, q, k: (i, q, 0)),
            pl.BlockSpec((1, N_HEADS, tq), lambda i, q, k: (i, 0, jnp.minimum(k, q))),
        ],
        out_specs=pl.BlockSpec((1, tq, FOX_W), lambda i, q, k: (i, q, 0)),
        out_shape=jax.ShapeDtypeStruct((b, t, FOX_W), BF16),
        scratch_shapes=[pltpu.VMEM((N_HEADS, tq, 1), F32), pltpu.VMEM((N_HEADS, tq, 1), F32),
                        pltpu.VMEM((tq, FOX_W), F32)],
        compiler_params=_cparams(("parallel", "parallel", "arbitrary")),
        name="fox_flash",
    )(qn, kn, proj, cum_col, cum_row)


def _fox_decode_body(pt_ref, q_ref, kn_ref, vn_ref, lfn_ref, *rest, pages_per_step, n_steps):
    del pt_ref
    pps = pages_per_step
    k_refs = rest[:pps]
    v_refs = rest[pps:2 * pps]
    lf_refs = rest[2 * pps:3 * pps]
    o_ref, qbd_ref, m_ref, l_ref, acc_ref, carry_ref = rest[3 * pps:]
    s = pl.program_id(1)
    scale = HEAD ** -0.5
    head_of_lane = lax.broadcasted_iota(jnp.int32, (N_HEADS, FOX_W), 1) // HEAD
    own = head_of_lane == lax.broadcasted_iota(jnp.int32, (N_HEADS, FOX_W), 0)

    @pl.when(s == 0)
    def _():
        qbd = jnp.where(own, jnp.broadcast_to(q_ref[0], (N_HEADS, FOX_W)), 0.0)
        qbd_ref[...] = qbd
        m_ref[...] = jnp.sum(qbd * kn_ref[0], axis=-1, keepdims=True) * scale
        l_ref[...] = jnp.ones(l_ref.shape, F32)
        acc_ref[...] = jnp.broadcast_to(vn_ref[0], (N_HEADS, FOX_W))
        carry_ref[...] = jnp.zeros(carry_ref.shape, F32)

    qbd = qbd_ref[...].astype(BF16)
    lane = lax.broadcasted_iota(jnp.int32, (N_HEADS, PAGE), 1)
    for r in range(pps):
        lf = lf_refs[r][0, 0]
        suf = lf
        shift = 1
        while shift < PAGE:
            suf = suf + jnp.where(lane + shift < PAGE, pltpu.roll(suf, PAGE - shift, axis=1), 0.0)
            shift *= 2
        after = suf - lf + carry_ref[...]
        carry_ref[...] = carry_ref[...] + suf[:, 0:1]
        sc = _dot_nt(qbd, k_refs[r][0, 0].astype(BF16)) * scale + lfn_ref[0] + after
        m_prev = m_ref[...]
        m_new = jnp.maximum(m_prev, jnp.max(sc, axis=-1, keepdims=True))
        alpha = jnp.exp(m_prev - m_new)
        p = jnp.exp(sc - m_new)
        l_ref[...] = alpha * l_ref[...] + jnp.sum(p, axis=-1, keepdims=True)
        acc_ref[...] = alpha * acc_ref[...] + _dot(p.astype(BF16), v_refs[r][0, 0].astype(BF16))
        m_ref[...] = m_new

    @pl.when(s == n_steps - 1)
    def _():
        o = jnp.where(own, acc_ref[...] / l_ref[...], 0.0)
        o_ref[0] = jnp.sum(o, axis=0, keepdims=True).astype(o_ref.dtype)


def _fox_decode(layer, qn, kn, v_new, lf_new, cache_k, cache_v, cache_lf_t, page_table, *, pages_per_step=4):
    b = qn.shape[0]
    n_pages = page_table.shape[1]
    pps = pages_per_step
    n_steps = n_pages // pps

    def page_map(r):
        return lambda i, s, pt: (layer, pt[i, n_pages - 1 - (s * pps + r)], 0, 0)

    row = pl.BlockSpec((1, 1, FOX_W), lambda i, s, pt: (i, 0, 0))
    in_specs = [row, row, row, pl.BlockSpec((1, N_HEADS, 1), lambda i, s, pt: (i, 0, 0))]
    in_specs += [pl.BlockSpec((1, 1, PAGE, FOX_W), page_map(r)) for r in range(pps)]
    in_specs += [pl.BlockSpec((1, 1, PAGE, FOX_W), page_map(r)) for r in range(pps)]
    in_specs += [pl.BlockSpec((1, 1, N_HEADS, PAGE), page_map(r)) for r in range(pps)]
    grid_spec = pltpu.PrefetchScalarGridSpec(
        num_scalar_prefetch=1, grid=(b, n_steps), in_specs=in_specs,
        out_specs=pl.BlockSpec((1, 1, FOX_W), lambda i, s, pt: (i, 0, 0)),
        scratch_shapes=[pltpu.VMEM((N_HEADS, FOX_W), F32), pltpu.VMEM((N_HEADS, 1), F32),
                        pltpu.VMEM((N_HEADS, 1), F32), pltpu.VMEM((N_HEADS, FOX_W), F32),
                        pltpu.VMEM((N_HEADS, 1), F32)])
    return pl.pallas_call(
        functools.partial(_fox_decode_body, pages_per_step=pps, n_steps=n_steps),
        grid_spec=grid_spec,
        out_shape=jax.ShapeDtypeStruct((b, 1, FOX_W), BF16),
        compiler_params=_cparams(("parallel", "arbitrary")),
        name="fox_decode",
    )(page_table, qn, kn, v_new, lf_new, *([cache_k] * pps), *([cache_v] * pps), *([cache_lf_t] * pps))


def _hgrn_body(*refs, layer, t_valid, has_s0):
    it = iter(refs)
    x_ref = next(it)
    lbl_ref = next(it)
    gn_ref = next(it)
    s0_ref = next(it) if has_s0 else None
    o_ref = next(it)
    st_ref = next(it)
    q_s, k_s, v_s, b_s, stt_s = (next(it) for _ in range(5))
    c = pl.program_id(1)
    nc = pl.num_programs(1)
    cs = HG_CHUNK

    @pl.when(c == 0)
    def _():
        for h in range(N_HEADS):
            if has_s0:
                stt_s[h] = s0_ref[0, h].astype(F32).T
            else:
                stt_s[h] = jnp.zeros((HEAD, HEAD), F32)

    lg = lbl_ref[...].astype(F32)
    e = jnp.exp(lg - jnp.max(lg, axis=0, keepdims=True))
    pr = e / jnp.sum(e, axis=0, keepdims=True)
    lb = jnp.sum(pr[0:layer + 1], axis=0, keepdims=True) - pr[0:1]

    hf = x_ref[0, :, HG_W:2 * HG_W].astype(F32)
    logf = jnp.log(lb + (1.0 - lb) * _sigmoid(hf))
    kk = (1.0 - lb) * _sigmoid(-hf)
    qq = _silu(x_ref[0, :, 0:HG_W].astype(F32))
    vv = x_ref[0, :, 2 * HG_W:3 * HG_W].astype(F32)
    if t_valid is not None:
        live = (c * cs + lax.broadcasted_iota(jnp.int32, (cs, 1), 0)) < t_valid
        logf = jnp.where(live, logf, 0.0)
        kk = jnp.where(live, kk, 0.0)
        qq = jnp.where(live, qq, 0.0)
        vv = jnp.where(live, vv, 0.0)
    q_s[...] = qq
    k_s[...] = kk
    v_s[...] = vv
    b_s[...] = _cumsum_rows(logf)

    n_sub = cs // HG_SUB
    row = lax.broadcasted_iota(jnp.int32, (HG_SUB, HG_SUB), 0)
    col = lax.broadcasted_iota(jnp.int32, (HG_SUB, HG_SUB), 1)
    g = gn_ref[...]
    for h in range(N_HEADS):
        sl = slice(h * HEAD, (h + 1) * HEAD)
        stt = stt_s[h]
        bh = b_s[:, sl]
        b_end = b_s[cs - 1:cs, sl]
        o_inter = _dot_nt((q_s[:, sl] * jnp.exp(bh)).astype(BF16), stt.astype(BF16))
        outs = []
        for i in range(n_sub):
            r0 = i * HG_SUB
            n_keys = r0 + HG_SUB
            b_i = b_s[r0 - 1:r0, sl] if i > 0 else jnp.zeros((1, HEAD), F32)
            qd = q_s[r0:n_keys, sl] * jnp.exp(b_s[r0:n_keys, sl] - b_i)
            kd = k_s[0:n_keys, sl] * jnp.exp(b_i - b_s[0:n_keys, sl])
            att = _dot_nt(qd.astype(BF16), kd.astype(BF16))
            if i > 0:
                diag = jnp.where(row >= col, att[:, r0:n_keys], 0.0)
                att = jnp.concatenate([att[:, 0:r0], diag], axis=1)
            else:
                att = jnp.where(row >= col, att, 0.0)
            outs.append(_dot(att.astype(BF16), v_s[0:n_keys, sl].astype(BF16)))
        o = jnp.concatenate(outs, axis=0) + o_inter
        kd_end = k_s[:, sl] * jnp.exp(b_end - bh)
        stt_s[h] = stt * jnp.exp(b_end) + _dot_tn(v_s[:, sl].astype(BF16), kd_end.astype(BF16))
        ms = jnp.mean(o * o, axis=-1, keepdims=True)
        gate = _silu(x_ref[0, :, 3 * HG_W + h * HEAD:3 * HG_W + (h + 1) * HEAD].astype(F32))
        o_ref[0, :, sl] = (o * lax.rsqrt(ms + EPS) * g * gate).astype(o_ref.dtype)

    @pl.when(c == nc - 1)
    def _():
        for h in range(N_HEADS):
            st_ref[0, h] = stt_s[h].T


def _hgrn(proj, lb_logits, gnorm, s0, *, layer, t_valid):
    b, t, _ = proj.shape
    cs = HG_CHUNK
    has_s0 = s0 is not None
    depth = lb_logits.shape[0]
    in_specs = [pl.BlockSpec((1, cs, 4 * HG_W), lambda i, c: (i, c, COL_HG // (4 * HG_W))),
                pl.BlockSpec((depth, HG_W), lambda i, c: (0, 0)),
                pl.BlockSpec((1, HEAD), lambda i, c: (0, 0))]
    args = [proj, lb_logits.astype(F32), gnorm.reshape(1, HEAD).astype(F32)]
    if has_s0:
        in_specs.append(pl.BlockSpec((1, N_HEADS, HEAD, HEAD), lambda i, c: (i, 0, 0, 0)))
        args.append(s0)
    return pl.pallas_call(
        functools.partial(_hgrn_body, layer=layer, t_valid=t_valid, has_s0=has_s0),
        grid=(b, t // cs),
        in_specs=in_specs,
        out_specs=[pl.BlockSpec((1, cs, HG_W), lambda i, c: (i, c, 0)),
                   pl.BlockSpec((1, N_HEADS, HEAD, HEAD), lambda i, c: (i, 0, 0, 0))],
        out_shape=[jax.ShapeDtypeStruct((b, t, HG_W), BF16),
                   jax.ShapeDtypeStruct((b, N_HEADS, HEAD, HEAD), F32)],
        scratch_shapes=[pltpu.VMEM((cs, HG_W), F32)] * 4 + [pltpu.VMEM((N_HEADS, HEAD, HEAD), F32)],
        compiler_params=_cparams(("parallel", "arbitrary")),
        name="hgrn2",
    )(*args)


def _lane_blocks(cols, width):
    rows = cols[0].shape[0]
    lane = lax.broadcasted_iota(jnp.int32, (rows, width * len(cols)), 1)
    out = jnp.broadcast_to(cols[-1], (rows, width * len(cols)))
    for j in range(len(cols) - 2, -1, -1):
        out = jnp.where(lane < (j + 1) * width, cols[j], out)
    return out


def _ssd_body(*refs, t_valid, has_s0):
    it = iter(refs)
    xbc_ref = next(it)
    z_ref = next(it)
    dts_ref = next(it)
    cw_ref = next(it)
    cb_ref = next(it)
    dtb_ref = next(it)
    alog_ref = next(it)
    dsk_ref = next(it)
    gn_ref = next(it)
    cs0_ref = next(it) if has_s0 else None
    s0_ref = next(it) if has_s0 else None
    y_ref = next(it)
    st_ref = next(it)
    xp_s = next(it)
    act_s = next(it)
    c = pl.program_id(1)
    cs = SSD_CHUNK
    tail0 = 8 - (CONV_W - 1)

    @pl.when(c == 0)
    def _():
        if has_s0:
            xp_s[tail0:8, :] = cs0_ref[0].astype(F32)
            st_ref[...] = s0_ref[...].astype(F32)
        else:
            xp_s[tail0:8, :] = jnp.zeros((CONV_W - 1, CONV_DIM), F32)
            st_ref[...] = jnp.zeros(st_ref.shape, F32)

    xp_s[8:8 + cs, :] = xbc_ref[0].astype(F32)
    conv = cb_ref[...]
    for j in range(CONV_W):
        conv = conv + xp_s[tail0 + j:tail0 + j + cs, :] * cw_ref[j:j + 1, :]
    act_s[...] = _silu(conv)
    xp_s[tail0:8, :] = xp_s[cs + tail0:cs + 8, :]

    dt = _softplus(dts_ref[0].astype(F32) + dtb_ref[...])
    if t_valid is not None:
        live = (c * cs + lax.broadcasted_iota(jnp.int32, (cs, 1), 0)) < t_valid
        dt = jnp.where(live, dt, 0.0)
    a = _cumsum_rows(dt * (-jnp.exp(alog_ref[...])))
    a_t = a.T
    dt_t = dt.T
    a_end = a[cs - 1:cs, :]
    w_upd = dt * jnp.exp(a_end - a)
    e_a = jnp.exp(a)
    e_end = jnp.exp(a_end)
    causal = _tri_incl(cs)
    lane_g = lax.broadcasted_iota(jnp.int32, (cs, SSM_GW), 1)
    row_g = lax.broadcasted_iota(jnp.int32, (SSM_GW, 1), 0)

    for g in range(SSM_GROUPS):
        xs = slice(g * SSM_GW, (g + 1) * SSM_GW)
        bsl = slice(SSM_W + g * SSM_N, SSM_W + (g + 1) * SSM_N)
        csl = slice(SSM_W + SSM_GROUPS * SSM_N + g * SSM_N, SSM_W + SSM_GROUPS * SSM_N + (g + 1) * SSM_N)
        xg = act_s[:, xs]
        bm = act_s[:, bsl].astype(BF16)
        cm = act_s[:, csl].astype(BF16)
        cbm = _dot_nt(cm, bm)
        s_g = st_ref[0, g]
        lanes = [DT_LANE0 + g * SSM_HPG + j for j in range(SSM_HPG)]
        yg = _dot_nt(cm, s_g.astype(BF16)) * _lane_blocks([e_a[:, ln:ln + 1] for ln in lanes], SSM_P)
        for j, ln in enumerate(lanes):
            diff = a[:, ln:ln + 1] - a_t[ln:ln + 1, :]
            seg = jnp.where(causal, jnp.exp(jnp.where(causal, diff, 0.0)), 0.0)
            mh = cbm * seg * dt_t[ln:ln + 1, :]
            xm = jnp.where(jnp.logical_and(lane_g >= j * SSM_P, lane_g < (j + 1) * SSM_P), xg, 0.0)
            yg = yg + _dot(mh.astype(BF16), xm.astype(BF16))
        xw = xg * _lane_blocks([w_upd[:, ln:ln + 1] for ln in lanes], SSM_P)
        decay = jnp.broadcast_to(e_end[:, lanes[-1]:lanes[-1] + 1], (SSM_GW, 1))
        for j in range(SSM_HPG - 2, -1, -1):
            decay = jnp.where(row_g < (j + 1) * SSM_P, e_end[:, lanes[j]:lanes[j] + 1], decay)
        st_ref[0, g] = s_g * decay + _dot_tn(xw.astype(BF16), bm)

        y = yg + dsk_ref[:, xs] * xg
        y = y * _silu(z_ref[0, :, xs].astype(F32))
        ms = jnp.mean(y * y, axis=-1, keepdims=True)
        y_ref[0, :, xs] = (y * lax.rsqrt(ms + EPS) * gn_ref[:, xs]).astype(y_ref.dtype)


def _ssd(proj, small, conv_w, conv_b, dt_bias, a_log, d_skip, gnorm, conv_s0, s0, *, t_valid):
    b, t, _ = proj.shape
    cs = SSD_CHUNK
    has_s0 = s0 is not None
    pad = jnp.zeros((DT_LANE0,), F32)
    tail = jnp.zeros((128 - DT_LANE0 - SSM_HEADS,), F32)
    on_dt_lanes = lambda v: jnp.concatenate([pad, v.astype(F32), tail]).reshape(1, 128)
    const = lambda shape: pl.BlockSpec(shape, lambda i, c: (0,) * len(shape))
    in_specs = [pl.BlockSpec((1, cs, CONV_DIM), lambda i, c: (i, c, COL_XBC // CONV_DIM)),
                pl.BlockSpec((1, cs, SSM_W), lambda i, c: (i, c, COL_Z // SSM_W)),
                pl.BlockSpec((1, cs, 128), lambda i, c: (i, c, 0)),
                const((CONV_W, CONV_DIM)), const((1, CONV_DIM)), const((1, 128)), const((1, 128)),
                const((1, SSM_W)), const((1, SSM_W))]
    args = [proj, proj, small, conv_w.astype(F32), conv_b.reshape(1, CONV_DIM).astype(F32),
            on_dt_lanes(dt_bias), on_dt_lanes(a_log),
            jnp.repeat(d_skip.astype(F32), SSM_P).reshape(1, SSM_W), gnorm.reshape(1, SSM_W).astype(F32)]
    if has_s0:
        in_specs += [pl.BlockSpec((1, CONV_W - 1, CONV_DIM), lambda i, c: (i, 0, 0)),
                     pl.BlockSpec((1, SSM_GROUPS, SSM_GW, SSM_N), lambda i, c: (i, 0, 0, 0))]
        args += [conv_s0, s0.reshape(b, SSM_GROUPS, SSM_GW, SSM_N)]
    y, st = pl.pallas_call(
        functools.partial(_ssd_body, t_valid=t_valid, has_s0=has_s0),
        grid=(b, t // cs),
        in_specs=in_specs,
        out_specs=[pl.BlockSpec((1, cs, SSM_W), lambda i, c: (i, c, 0)),
                   pl.BlockSpec((1, SSM_GROUPS, SSM_GW, SSM_N), lambda i, c: (i, 0, 0, 0))],
        out_shape=[jax.ShapeDtypeStruct((b, t, SSM_W), BF16),
                   jax.ShapeDtypeStruct((b, SSM_GROUPS, SSM_GW, SSM_N), F32)],
        scratch_shapes=[pltpu.VMEM((cs + 8, CONV_DIM), F32), pltpu.VMEM((cs, CONV_DIM), F32)],
        compiler_params=_cparams(("parallel", "arbitrary")),
        name="ssd",
    )(*args)
    return y, st.reshape(b, SSM_HEADS, SSM_P, SSM_N)


def _xattn_body(q_ref, k_ref, v_ref, gq_ref, o_ref):
    scale = HEAD ** -0.5
    g = gq_ref[...]
    for h in range(XA_HEADS):
        sl = slice(h * HEAD, (h + 1) * HEAD)
        q = q_ref[0, :, sl].astype(F32)
        ms = jnp.mean(q * q, axis=-1, keepdims=True)
        qn = q * lax.rsqrt(ms + EPS) * g
        s = _dot_nt(qn.astype(BF16), k_ref[0, :, sl].astype(BF16)) * scale
        p = jnp.exp(s - jnp.max(s, axis=-1, keepdims=True))
        o = _dot(p.astype(BF16), v_ref[0, :, sl].astype(BF16)) / jnp.sum(p, axis=-1, keepdims=True)
        o_ref[0, :, sl] = o.astype(o_ref.dtype)


def _xattn(q, mk, mv, gq):
    b, t, _ = q.shape
    n_mem = mk.shape[1]
    tq = t if t <= 512 else 512
    return pl.pallas_call(
        _xattn_body,
        grid=(b, t // tq),
        in_specs=[pl.BlockSpec((1, tq, XA_W), lambda i, j: (i, j, 0)),
                  pl.BlockSpec((1, n_mem, XA_W), lambda i, j: (i, 0, 0)),
                  pl.BlockSpec((1, n_mem, XA_W), lambda i, j: (i, 0, 0)),
                  pl.BlockSpec((1, HEAD), lambda i, j: (0, 0))],
        out_specs=pl.BlockSpec((1, tq, XA_W), lambda i, j: (i, j, 0)),
        out_shape=jax.ShapeDtypeStruct((b, t, XA_W), BF16),
        compiler_params=_cparams(("parallel", "parallel")),
        name="xattn",
    )(q, mk, mv, gq.reshape(1, HEAD).astype(F32))


def _split_w_in(w):
    o = 0
    parts = {}
    for name, width in (("fq", FOX_W), ("fk", FOX_W), ("fv", FOX_W), ("fg", N_HEADS), ("hg", 4 * HG_W),
                        ("z", SSM_W), ("xbc", CONV_DIM), ("dt", SSM_HEADS)):
        parts[name] = w[:, o:o + width]
        o += width
    main = jnp.concatenate([parts["xbc"], parts["hg"], parts["z"], parts["fq"], parts["fk"], parts["fv"]], axis=1)
    small = jnp.concatenate([parts["fg"], parts["dt"],
                             jnp.zeros((w.shape[0], 128 - N_HEADS - SSM_HEADS), w.dtype)], axis=1)
    return main.astype(BF16), small.astype(BF16)


def _mixers(x, lw, l, *, fox_past, hg_s0, ssm_s0, conv_s0, page_table):
    b, t, d = x.shape
    x2 = x.reshape(b * t, d)
    proj2 = _matmul(x2, lw["w_main"], gain=lw["norm_mix"], name="in_proj")
    small2 = _matmul(x2, lw["w_small"], gain=lw["norm_mix"], name="in_proj_small")
    proj = proj2.reshape(b, t, MAIN_W)
    small = small2.reshape(b, t, 128)

    qn = _headnorm(proj2, lw["fox_gq"], col0=COL_FQ, width=FOX_W, out_dtype=BF16 if fox_past is None else F32)
    kn = _headnorm(proj2, lw["fox_gk"], col0=COL_FK, width=FOX_W)
    fv = proj[:, :, COL_FV:COL_FV + FOX_W]
    gates_t = jnp.swapaxes(small[:, :, 0:N_HEADS], 1, 2)
    logf_t, cum_t = _fox_gate(gates_t, lw["fox_bf"])
    logf = jnp.swapaxes(logf_t, 1, 2)

    if fox_past is None:
        fo = _fox_flash(qn.reshape(b, t, FOX_W), kn.reshape(b, t, FOX_W), proj,
                        jnp.swapaxes(cum_t, 1, 2), cum_t)
        hg_in, ssd_in, small_in, t_valid = proj, proj, small, None
        t_hg = t_ssd = t
    else:
        cache_k, cache_v, cache_lf_t = fox_past
        fo = _fox_decode(l, qn.reshape(b, t, FOX_W), kn.reshape(b, t, FOX_W), fv, logf_t,
                         cache_k, cache_v, cache_lf_t, page_table)
        t_hg, t_ssd, t_valid = HG_CHUNK, SSD_CHUNK, t
        hg_in = jnp.pad(proj, ((0, 0), (0, t_hg - t), (0, 0)))
        ssd_in = jnp.pad(proj, ((0, 0), (0, t_ssd - t), (0, 0)))
        small_in = jnp.pad(small, ((0, 0), (0, t_ssd - t), (0, 0)))

    ho, hg_state = _hgrn(hg_in, lw["hg_lb_logits"], lw["hg_gnorm"], hg_s0, layer=l, t_valid=t_valid)
    sy, ssm_state = _ssd(ssd_in, small_in, lw["conv_w"], lw["conv_b"], lw["dt_bias"], lw["a_log"],
                         lw["d_skip"], lw["ssm_gnorm"], conv_s0, ssm_s0, t_valid=t_valid)
    mixed = jnp.concatenate([fo, ho[:, :t], sy[:, :t]], axis=-1).reshape(b * t, d)
    x_new = _matmul(mixed, lw["w_out"], res=x2, name="out_proj").reshape(b, t, d)

    xbc = proj[:, :, COL_XBC:COL_XBC + CONV_DIM]
    prev = jnp.zeros((b, CONV_W - 1, CONV_DIM), F32) if conv_s0 is None else conv_s0.astype(F32)
    conv_state = jnp.concatenate([prev, xbc], axis=1)[:, t:]
    fk = kn.reshape(b, t, N_HEADS, HEAD)
    return x_new, fk, fv.reshape(b, t, N_HEADS, HEAD), logf, hg_state, ssm_state, conv_state


def _cross_and_mlp(x, lw, mk, mv):
    b, t, d = x.shape
    x2 = x.reshape(b * t, d)
    q = _matmul(x2, lw["xa_wq"], gain=lw["norm_xa"], name="xa_q")
    o = _xattn(q.reshape(b, t, XA_W), mk, mv, lw["xa_gq"])
    x2 = _matmul(o.reshape(b * t, XA_W), lw["xa_wo"], res=x2, name="xa_out")
    u = _matmul(x2, lw["w_up"], gain=lw["norm_mlp"], act="relu2", out_dtype=BF16, name="mlp_up")
    x2 = _matmul(u, lw["w_down"], res=x2, tk=2048, name="mlp_down")
    return x2.reshape(b, t, d)


def kernel(x_prompt, x_sample, cache_fox_k, cache_fox_v, cache_fox_logf, cache_mem_k, cache_mem_v, state_hgrn, state_ssm, state_conv, page_table, mem_prompt, norm_mix, w_in, fox_gq, fox_gk, fox_bf, hg_lb_logits, hg_gnorm, conv_w, conv_b, dt_bias, a_log, d_skip, ssm_gnorm, w_out, norm_xa, norm_mem, xa_wq, xa_wk, xa_wv, xa_gq, xa_gk, xa_wo, norm_mlp, w_up, w_down):
    depth = w_in.shape[0]
    bp = x_prompt.shape[0]
    n_mem = mem_prompt.shape[1]
    pool = cache_fox_k.shape[1]
    cache_k = cache_fox_k.reshape(depth, pool, PAGE, FOX_W)
    cache_v = cache_fox_v.reshape(depth, pool, PAGE, FOX_W)
    cache_lf_t = jnp.swapaxes(cache_fox_logf, 2, 3)
    mem2 = mem_prompt.reshape(bp * n_mem, -1)

    xp, xs = x_prompt, x_sample
    outs = {k: [] for k in ("p_fk", "p_fv", "p_fl", "p_hg", "p_ss", "p_cv", "p_mk", "p_mv",
                            "s_fk", "s_fv", "s_fl", "s_hg", "s_ss", "s_cv")}
    for l in range(depth):
        w_main, w_small = _split_w_in(w_in[l])
        lw = dict(w_main=w_main, w_small=w_small, norm_mix=norm_mix[l], fox_gq=fox_gq[l], fox_gk=fox_gk[l],
                  fox_bf=fox_bf[l], hg_lb_logits=hg_lb_logits, hg_gnorm=hg_gnorm[l], conv_w=conv_w[l],
                  conv_b=conv_b[l], dt_bias=dt_bias[l], a_log=a_log[l], d_skip=d_skip[l],
                  ssm_gnorm=ssm_gnorm[l], w_out=w_out[l].astype(BF16), norm_xa=norm_xa[l],
                  xa_wq=xa_wq[l].astype(BF16), xa_gq=xa_gq[l], xa_wo=xa_wo[l].astype(BF16),
                  norm_mlp=norm_mlp[l], w_up=w_up[l].astype(BF16), w_down=w_down[l].astype(BF16))

        xp, fk, fv, fl, hg, ss, cv = _mixers(xp, lw, l, fox_past=None, hg_s0=None, ssm_s0=None,
                                             conv_s0=None, page_table=None)
        w_kv = jnp.concatenate([xa_wk[l], xa_wv[l]], axis=1).astype(BF16)
        kv = _matmul(mem2, w_kv, gain=norm_mem[l], name="mem_kv")
        mk = _headnorm(kv, xa_gk[l], col0=0, width=XA_W).reshape(bp, n_mem, XA_W)
        mv = kv[:, XA_W:].reshape(bp, n_mem, XA_W)
        xp = _cross_and_mlp(xp, lw, mk, mv)
        for key, val in zip(("p_fk", "p_fv", "p_fl", "p_hg", "p_ss", "p_cv"), (fk, fv, fl, hg, ss, cv)):
            outs[key].append(val)
        outs["p_mk"].append(mk.reshape(bp, n_mem, XA_HEADS, HEAD))
        outs["p_mv"].append(mv.reshape(bp, n_mem, XA_HEADS, HEAD))

        xs, fk, fv, fl, hg, ss, cv = _mixers(xs, lw, l, fox_past=(cache_k, cache_v, cache_lf_t),
                                             hg_s0=state_hgrn[l], ssm_s0=state_ssm[l],
                                             conv_s0=state_conv[l], page_table=page_table)
        bs = xs.shape[0]
        xs = _cross_and_mlp(xs, lw, cache_mem_k[l].reshape(bs, n_mem, XA_W),
                            cache_mem_v[l].reshape(bs, n_mem, XA_W))
        for key, val in zip(("s_fk", "s_fv", "s_fl", "s_hg", "s_ss", "s_cv"), (fk, fv, fl, hg, ss, cv)):
            outs[key].append(val)

    st = {k: jnp.stack(v) for k, v in outs.items()}
    return (xp, xs, st["p_fk"], st["p_fv"], st["p_fl"], st["p_hg"], st["p_ss"], st["p_cv"], st["p_mk"],
            st["p_mv"], st["s_fk"], st["s_fv"], st["s_fl"], st["s_hg"], st["s_ss"], st["s_cv"])
```

```python
import functools

import jax
import jax.numpy as jnp
from jax import lax
from jax.experimental import pallas as pl
from jax.experimental.pallas import tpu as pltpu

F32 = jnp.float32
BF16 = jnp.bfloat16

EPS = 1e-6
MASK_VALUE = -1e30
HEAD = 128
N_HEADS = 8
FOX_W = N_HEADS * HEAD
HG_W = N_HEADS * HEAD
SSM_W = 2048
SSM_P = 64
SSM_HEADS = SSM_W // SSM_P
SSM_GROUPS = 8
SSM_HPG = SSM_HEADS // SSM_GROUPS
SSM_N = 128
SSM_GW = SSM_HPG * SSM_P
CONV_W = 4
CONV_DIM = SSM_W + 2 * SSM_GROUPS * SSM_N
XA_HEADS = 4
XA_W = XA_HEADS * HEAD
HG_CHUNK = 64
HG_SUB = 16
SSD_CHUNK = 128
PAGE = 128
DT_LANE0 = N_HEADS

COL_XBC = 0
COL_HG = CONV_DIM
COL_Z = COL_HG + 4 * HG_W
COL_FQ = COL_Z + SSM_W
COL_FK = COL_FQ + FOX_W
COL_FV = COL_FK + FOX_W
MAIN_W = COL_FV + FOX_W

V7X_VMEM_LIMIT = 56 * 1024 * 1024


def _cparams(sem, vmem=V7X_VMEM_LIMIT):
    return pltpu.CompilerParams(dimension_semantics=sem, vmem_limit_bytes=vmem)


def _sigmoid(x):
    return 1.0 / (1.0 + jnp.exp(-x))


def _silu(x):
    return x * _sigmoid(x)


def _softplus(x):
    return jnp.maximum(x, 0.0) + jnp.log1p(jnp.exp(-jnp.abs(x)))


def _log_sigmoid(x):
    return -_softplus(-x)


def _dot_nt(a, b):
    return lax.dot_general(a, b, (((1,), (1,)), ((), ())), preferred_element_type=F32)


def _dot_tn(a, b):
    return lax.dot_general(a, b, (((0,), (0,)), ((), ())), preferred_element_type=F32)


def _dot(a, b):
    return jnp.dot(a, b, preferred_element_type=F32)


def _tri_incl(n):
    r = lax.broadcasted_iota(jnp.int32, (n, n), 0)
    c = lax.broadcasted_iota(jnp.int32, (n, n), 1)
    return r >= c


def _cumsum_rows(x):
    n = x.shape[0]
    tri = _tri_incl(n).astype(BF16)
    hi = x.astype(BF16)
    r1 = x - hi.astype(F32)
    mid = r1.astype(BF16)
    lo = (r1 - mid.astype(F32)).astype(BF16)
    return _dot(tri, hi) + _dot(tri, mid) + _dot(tri, lo)


def _mm_body(*refs, nk, norm, act, has_res):
    it = iter(refs)
    a_ref = next(it)
    g_ref = next(it) if norm else None
    w_ref = next(it)
    r_ref = next(it) if has_res else None
    o_ref = next(it)
    h_ref = next(it) if norm else None
    acc_ref = next(it) if nk > 1 else None
    j = pl.program_id(1)
    k = pl.program_id(2)

    if norm:
        @pl.when(j == 0)
        def _():
            x = a_ref[...].astype(F32)
            ms = jnp.mean(x * x, axis=-1, keepdims=True)
            h_ref[...] = (x * lax.rsqrt(ms + EPS) * g_ref[...]).astype(BF16)
        a = h_ref[...]
    else:
        a = a_ref[...]

    p = _dot(a, w_ref[...])

    def finish(v):
        if act == "relu2":
            v = jnp.square(jnp.maximum(v, 0.0))
        if has_res:
            v = v + r_ref[...]
        o_ref[...] = v.astype(o_ref.dtype)

    if nk == 1:
        finish(p)
    else:
        @pl.when(k == 0)
        def _():
            acc_ref[...] = p

        @pl.when(k > 0)
        def _():
            acc_ref[...] += p

        @pl.when(k == nk - 1)
        def _():
            finish(acc_ref[...])


def _pick_tile(n, candidates):
    for c in candidates:
        if n % c == 0:
            return c
    return n


def _matmul(a, w, *, gain=None, res=None, act=None, out_dtype=F32, tm_max=512, tk=None, name="matmul"):
    m, kdim = a.shape
    n = w.shape[1]
    small_m = m <= 64
    tm = m if small_m else _pick_tile(m, tuple(c for c in (1024, 512, 256, 128) if c <= tm_max))
    tn = _pick_tile(n, (2048, 1024, 512, 256, 128)) if small_m else _pick_tile(n, (1024, 512, 256, 128))
    tk = kdim if tk is None else tk
    nk = kdim // tk
    norm = gain is not None
    assert not (norm and nk > 1)
    in_specs = [pl.BlockSpec((tm, tk), lambda i, j, k: (i, k))]
    args = [a]
    if norm:
        in_specs.append(pl.BlockSpec((1, kdim), lambda i, j, k: (0, 0)))
        args.append(gain.reshape(1, kdim).astype(F32))
    in_specs.append(pl.BlockSpec((tk, tn), lambda i, j, k: (k, j)))
    args.append(w)
    if res is not None:
        in_specs.append(pl.BlockSpec((tm, tn), lambda i, j, k: (i, j)))
        args.append(res)
    scratch = []
    if norm:
        scratch.append(pltpu.VMEM((tm, kdim), BF16))
    if nk > 1:
        scratch.append(pltpu.VMEM((tm, tn), F32))
    return pl.pallas_call(
        functools.partial(_mm_body, nk=nk, norm=norm, act=act, has_res=res is not None),
        grid=(m // tm, n // tn, nk),
        in_specs=in_specs,
        out_specs=pl.BlockSpec((tm, tn), lambda i, j, k: (i, j)),
        out_shape=jax.ShapeDtypeStruct((m, n), out_dtype),
        scratch_shapes=scratch,
        compiler_params=_cparams(("parallel", "arbitrary", "arbitrary")),
        name=name,
    )(*args)


def _outproj_body(fo_ref, ho_ref, sy_ref, w0_ref, w1_ref, w2_ref, r_ref, o_ref):
    p = _dot(fo_ref[...], w0_ref[...]) + _dot(ho_ref[...], w1_ref[...]) + _dot(sy_ref[...], w2_ref[...])
    o_ref[...] = p + r_ref[...]


def _out_proj(fo, ho, sy, w, res):
    m = fo.shape[0]
    n = w.shape[1]
    tm = m if m <= 64 else _pick_tile(m, (1024, 512, 256, 128))
    tn = _pick_tile(n, (1024, 512, 256, 128))
    assert fo.shape[1] == FOX_W and ho.shape[1] == HG_W and sy.shape[1] == SSM_W and FOX_W == HG_W
    return pl.pallas_call(
        _outproj_body,
        grid=(m // tm, n // tn),
        in_specs=[pl.BlockSpec((tm, FOX_W), lambda i, j: (i, 0)),
                  pl.BlockSpec((tm, HG_W), lambda i, j: (i, 0)),
                  pl.BlockSpec((tm, SSM_W), lambda i, j: (i, 0)),
                  pl.BlockSpec((FOX_W, tn), lambda i, j: (0, j)),
                  pl.BlockSpec((HG_W, tn), lambda i, j: (1, j)),
                  pl.BlockSpec((SSM_W, tn), lambda i, j: ((FOX_W + HG_W) // SSM_W, j)),
                  pl.BlockSpec((tm, tn), lambda i, j: (i, j))],
        out_specs=pl.BlockSpec((tm, tn), lambda i, j: (i, j)),
        out_shape=jax.ShapeDtypeStruct((m, n), F32),
        compiler_params=_cparams(("parallel", "arbitrary")),
        name="out_proj",
    )(fo, ho, sy, w, w, w, res)


def _headnorm_body(x_ref, g_ref, o_ref, *, n_heads):
    g = g_ref[...]
    for h in range(n_heads):
        sl = slice(h * HEAD, (h + 1) * HEAD)
        x = x_ref[:, sl].astype(F32)
        ms = jnp.mean(x * x, axis=-1, keepdims=True)
        o_ref[:, sl] = (x * lax.rsqrt(ms + EPS) * g).astype(o_ref.dtype)


def _headnorm(x, gain, *, col0, width, out_dtype=F32):
    m = x.shape[0]
    tm = m if m <= 64 else _pick_tile(m, (512, 256, 128))
    return pl.pallas_call(
        functools.partial(_headnorm_body, n_heads=width // HEAD),
        grid=(m // tm,),
        in_specs=[pl.BlockSpec((tm, width), lambda i: (i, col0 // width)),
                  pl.BlockSpec((1, HEAD), lambda i: (0, 0))],
        out_specs=pl.BlockSpec((tm, width), lambda i: (i, 0)),
        out_shape=jax.ShapeDtypeStruct((m, width), out_dtype),
        compiler_params=_cparams(("parallel",)),
        name="headnorm",
    )(x, gain.reshape(1, HEAD).astype(F32))


def _fox_gate_body(g_ref, bf_ref, lf_ref, cum_ref, *, t):
    lf = _log_sigmoid(g_ref[0] + bf_ref[...])
    lf_ref[0] = lf
    lane = lax.broadcasted_iota(jnp.int32, lf.shape, 1)
    c = lf
    shift = 1
    while shift < t:
        c = c + jnp.where(lane >= shift, pltpu.roll(c, shift, axis=1), 0.0)
        shift *= 2
    cum_ref[0] = c


def _fox_gate(gates_t, fox_bf):
    b, h, t = gates_t.shape
    spec = pl.BlockSpec((1, h, t), lambda i: (i, 0, 0))
    return pl.pallas_call(
        functools.partial(_fox_gate_body, t=t),
        grid=(b,),
        in_specs=[spec, pl.BlockSpec((h, 1), lambda i: (0, 0))],
        out_specs=[spec, spec],
        out_shape=[jax.ShapeDtypeStruct((b, h, t), F32)] * 2,
        compiler_params=_cparams(("parallel",)),
        name="fox_gate",
    )(gates_t, fox_bf.reshape(h, 1).astype(F32))


def _fox_flash_body(q_ref, k_ref, v_ref, cq_ref, ck_ref, o_ref, m_ref, l_ref, acc_ref, *, tq):
    qi = pl.program_id(1)
    ki = pl.program_id(2)
    scale = HEAD ** -0.5

    @pl.when(ki == 0)
    def _():
        m_ref[...] = jnp.full(m_ref.shape, MASK_VALUE, F32)
        l_ref[...] = jnp.zeros(l_ref.shape, F32)
        acc_ref[...] = jnp.zeros(acc_ref.shape, F32)

    def update(diagonal):
        for h in range(N_HEADS):
            sl = slice(h * HEAD, (h + 1) * HEAD)
            s = _dot_nt(q_ref[0, :, sl], k_ref[0, :, sl].astype(BF16)) * scale
            s = s + cq_ref[0, :, h:h + 1] - ck_ref[0, h:h + 1, :]
            if diagonal:
                s = jnp.where(_tri_incl(tq), s, MASK_VALUE)
            m_prev = m_ref[h]
            m_new = jnp.maximum(m_prev, jnp.max(s, axis=-1, keepdims=True))
            alpha = jnp.exp(m_prev - m_new)
            p = jnp.exp(s - m_new)
            l_ref[h] = alpha * l_ref[h] + jnp.sum(p, axis=-1, keepdims=True)
            acc_ref[:, sl] = alpha * acc_ref[:, sl] + _dot(p.astype(BF16), v_ref[0, :, sl].astype(BF16))
            m_ref[h] = m_new

    @pl.when(ki < qi)
    def _():
        update(diagonal=False)

    @pl.when(ki == qi)
    def _():
        update(diagonal=True)
        for h in range(N_HEADS):
            sl = slice(h * HEAD, (h + 1) * HEAD)
            o_ref[0, :, sl] = (acc_ref[:, sl] / l_ref[h]).astype(o_ref.dtype)


def _fox_flash(qn, kn, proj, cum_col, cum_row, *, tq=512):
    b, t, _ = qn.shape
    nq = t // tq
    return pl.pallas_call(
        functools.partial(_fox_flash_body, tq=tq),
        grid=(b, nq, nq),
        in_specs=[
            pl.BlockSpec((1, tq, FOX_W), lambda i, q, k: (i, q, 0)),
            pl.BlockSpec((1, tq, FOX_W), lambda i, q, k: (i, jnp.minimum(k, q), 0)),
            pl.BlockSpec((1, tq, FOX_W), lambda i, q, k: (i, jnp.minimum(k, q), COL_FV // FOX_W)),
            pl.BlockSpec((1, tq, N_HEADS), lambda i, q, k: (i, q, 0)),
            pl.BlockSpec((1, N_HEADS, tq), lambda i, q, k: (i, 0, jnp.minimum(k, q))),
        ],
        out_specs=pl.BlockSpec((1, tq, FOX_W), lambda i, q, k: (i, q, 0)),
        out_shape=jax.ShapeDtypeStruct((b, t, FOX_W), BF16),
        scratch_shapes=[pltpu.VMEM((N_HEADS, tq, 1), F32), pltpu.VMEM((N_HEADS, tq, 1), F32),
                        pltpu.VMEM((tq, FOX_W), F32)],
        compiler_params=_cparams(("parallel", "parallel", "arbitrary")),
        name="fox_flash",
    )(qn, kn, proj, cum_col, cum_row)


def _fox_decode_body(pt_ref, q_ref, kn_ref, vn_ref, lfn_ref, *rest, pages_per_step, n_steps):
    del pt_ref
    pps = pages_per_step
    k_refs = rest[:pps]
    v_refs = rest[pps:2 * pps]
    lf_refs = rest[2 * pps:3 * pps]
    o_ref, m_ref, l_ref, acc_ref, carry_ref = rest[3 * pps:]
    s = pl.program_id(1)
    scale = HEAD ** -0.5
    rows = PAGE * N_HEADS
    lane = lax.broadcasted_iota(jnp.int32, (1, rows), 1)
    own = (lax.broadcasted_iota(jnp.int32, (N_HEADS, rows), 1) % N_HEADS
           == lax.broadcasted_iota(jnp.int32, (N_HEADS, rows), 0))

    @pl.when(s == 0)
    def _():
        m_ref[...] = jnp.sum(q_ref[0] * kn_ref[0], axis=-1, keepdims=True) * scale
        l_ref[...] = jnp.ones(l_ref.shape, F32)
        acc_ref[...] = vn_ref[0]
        carry_ref[...] = jnp.zeros(carry_ref.shape, F32)

    lf = jnp.concatenate([lf_refs[r][0, 0] for r in range(pps)], axis=0)
    tot = lf
    suf = lf
    shift = N_HEADS
    while shift < rows:
        tot = tot + pltpu.roll(tot, shift, axis=1)
        suf = suf + jnp.where(lane + shift < rows, pltpu.roll(suf, rows - shift, axis=1), 0.0)
        shift *= 2
    page = lax.broadcasted_iota(jnp.int32, (pps, 1), 0)
    newer = tot
    shift = 1
    while shift < pps:
        newer = newer + jnp.where(page >= shift, pltpu.roll(newer, shift, axis=0), 0.0)
        shift *= 2
    after = suf - lf + (newer - tot) + carry_ref[...]
    carry_ref[...] = carry_ref[...] + newer[pps - 1:pps, :]

    q = q_ref[0].astype(BF16)
    scs = []
    for r in range(pps):
        sc = _dot_nt(q, k_refs[r][0, 0].astype(BF16)) * scale + lfn_ref[0] + after[r:r + 1, :]
        scs.append(jnp.where(own, sc, MASK_VALUE))
    m_prev = m_ref[...]
    m_new = m_prev
    for sc in scs:
        m_new = jnp.maximum(m_new, jnp.max(sc, axis=-1, keepdims=True))
    alpha = jnp.exp(m_prev - m_new)
    l_new = alpha * l_ref[...]
    acc = alpha * acc_ref[...]
    for r, sc in enumerate(scs):
        p = jnp.exp(sc - m_new)
        l_new = l_new + jnp.sum(p, axis=-1, keepdims=True)
        acc = acc + _dot(p.astype(BF16), v_refs[r][0, 0].astype(BF16))
    l_ref[...] = l_new
    acc_ref[...] = acc
    m_ref[...] = m_new

    @pl.when(s == n_steps - 1)
    def _():
        o_ref[0] = (acc_ref[...] / l_ref[...]).astype(o_ref.dtype)


def _fox_decode(layer, qn, kn, v_new, lf_new, cache_k, cache_v, cache_lf, page_table, *, pages_per_step=8):
    b = qn.shape[0]
    n_pages = page_table.shape[1]
    pps = pages_per_step
    n_steps = n_pages // pps
    rows = PAGE * N_HEADS

    def page_map(r):
        return lambda i, s, pt: (layer, pt[i, n_pages - 1 - (s * pps + r)], 0, 0)

    head = pl.BlockSpec((1, N_HEADS, HEAD), lambda i, s, pt: (i, 0, 0))
    in_specs = [head, head, head, pl.BlockSpec((1, N_HEADS, 1), lambda i, s, pt: (i, 0, 0))]
    in_specs += [pl.BlockSpec((1, 1, rows, HEAD), page_map(r)) for r in range(pps)]
    in_specs += [pl.BlockSpec((1, 1, rows, HEAD), page_map(r)) for r in range(pps)]
    in_specs += [pl.BlockSpec((1, 1, 1, rows), page_map(r)) for r in range(pps)]
    grid_spec = pltpu.PrefetchScalarGridSpec(
        num_scalar_prefetch=1, grid=(b, n_steps), in_specs=in_specs,
        out_specs=pl.BlockSpec((1, N_HEADS, HEAD), lambda i, s, pt: (i, 0, 0)),
        scratch_shapes=[pltpu.VMEM((N_HEADS, 1), F32), pltpu.VMEM((N_HEADS, 1), F32),
                        pltpu.VMEM((N_HEADS, HEAD), F32), pltpu.VMEM((1, rows), F32)])
    return pl.pallas_call(
        functools.partial(_fox_decode_body, pages_per_step=pps, n_steps=n_steps),
        grid_spec=grid_spec,
        out_shape=jax.ShapeDtypeStruct((b, N_HEADS, HEAD), BF16),
        compiler_params=_cparams(("parallel", "arbitrary")),
        name="fox_decode",
    )(page_table, qn, kn, v_new, lf_new, *([cache_k] * pps), *([cache_v] * pps), *([cache_lf] * pps))


def _rows_of_chunk(ref, cols, chunk_rows):
    x = ref[0, :, cols].astype(F32)
    if x.shape[0] == chunk_rows:
        return x
    assert x.shape[0] == 1
    return jnp.broadcast_to(x, (chunk_rows, x.shape[1]))


def _hgrn_body(*refs, layer, t_valid, has_s0):
    it = iter(refs)
    x_ref = next(it)
    lbl_ref = next(it)
    gn_ref = next(it)
    s0_ref = next(it) if has_s0 else None
    o_ref = next(it)
    st_ref = next(it)
    q_s, k_s, v_s, b_s, stt_s = (next(it) for _ in range(5))
    c = pl.program_id(1)
    nc = pl.num_programs(1)
    cs = HG_CHUNK
    t_blk = o_ref.shape[1]

    @pl.when(c == 0)
    def _():
        for h in range(N_HEADS):
            if has_s0:
                stt_s[h] = s0_ref[0, h].astype(F32).T
            else:
                stt_s[h] = jnp.zeros((HEAD, HEAD), F32)

    lg = lbl_ref[...].astype(F32)
    e = jnp.exp(lg - jnp.max(lg, axis=0, keepdims=True))
    pr = e / jnp.sum(e, axis=0, keepdims=True)
    lb = jnp.sum(pr[0:layer + 1], axis=0, keepdims=True) - pr[0:1]

    hf = _rows_of_chunk(x_ref, slice(HG_W, 2 * HG_W), cs)
    logf = jnp.log(lb + (1.0 - lb) * _sigmoid(hf))
    kk = (1.0 - lb) * _sigmoid(-hf)
    qq = _silu(_rows_of_chunk(x_ref, slice(0, HG_W), cs))
    vv = _rows_of_chunk(x_ref, slice(2 * HG_W, 3 * HG_W), cs)
    if t_valid is not None:
        live = (c * cs + lax.broadcasted_iota(jnp.int32, (cs, 1), 0)) < t_valid
        logf = jnp.where(live, logf, 0.0)
        kk = jnp.where(live, kk, 0.0)
        qq = jnp.where(live, qq, 0.0)
        vv = jnp.where(live, vv, 0.0)
    q_s[...] = qq
    k_s[...] = kk
    v_s[...] = vv
    b_s[...] = _cumsum_rows(logf)

    n_sub = cs // HG_SUB
    causal = _tri_incl(cs)
    zeros_sub = jnp.zeros((HG_SUB, HEAD), F32)
    g = gn_ref[...]

    atts = []
    for h in range(N_HEADS):
        sl = slice(h * HEAD, (h + 1) * HEAD)
        q_parts, k_parts = [], []
        for i in range(n_sub):
            r0 = i * HG_SUB
            n_keys = r0 + HG_SUB
            b_i = b_s[r0 - 1:r0, sl] if i > 0 else jnp.zeros((1, HEAD), F32)
            qd = q_s[r0:n_keys, sl] * jnp.exp(b_s[r0:n_keys, sl] - b_i)
            kd = k_s[0:n_keys, sl] * jnp.exp(b_i - b_s[0:n_keys, sl])
            q_parts.append(jnp.concatenate([zeros_sub] * i + [qd] + [zeros_sub] * (n_sub - 1 - i), axis=0))
            k_parts.append(jnp.concatenate([kd] + [zeros_sub] * (n_sub - 1 - i), axis=0))
        q_cat = jnp.concatenate(q_parts, axis=1).astype(BF16)
        k_cat = jnp.concatenate(k_parts, axis=1).astype(BF16)
        atts.append(jnp.where(causal, _dot_nt(q_cat, k_cat), 0.0).astype(BF16))

    for h in range(N_HEADS):
        sl = slice(h * HEAD, (h + 1) * HEAD)
        stt = stt_s[h]
        bh = b_s[:, sl]
        b_end = b_s[cs - 1:cs, sl]
        o = _dot(atts[h], v_s[:, sl].astype(BF16))
        o = o + _dot_nt((q_s[:, sl] * jnp.exp(bh)).astype(BF16), stt.astype(BF16))
        kd_end = k_s[:, sl] * jnp.exp(b_end - bh)
        stt_s[h] = stt * jnp.exp(b_end) + _dot_tn(v_s[:, sl].astype(BF16), kd_end.astype(BF16))
        ms = jnp.mean(o * o, axis=-1, keepdims=True)
        gate = _silu(_rows_of_chunk(x_ref, slice(3 * HG_W + h * HEAD, 3 * HG_W + (h + 1) * HEAD), cs))
        res = o * lax.rsqrt(ms + EPS) * g * gate
        o_ref[0, :, sl] = res[0:t_blk].astype(o_ref.dtype)

    @pl.when(c == nc - 1)
    def _():
        for h in range(N_HEADS):
            st_ref[0, h] = stt_s[h].T


def _hgrn(proj, lb_logits, gnorm, s0, *, layer):
    b, t, _ = proj.shape
    cs = HG_CHUNK
    assert t == 1 or t % cs == 0
    t_blk, n_chunks, t_valid = (1, 1, 1) if t == 1 else (cs, t // cs, None)
    has_s0 = s0 is not None
    depth = lb_logits.shape[0]
    in_specs = [pl.BlockSpec((1, t_blk, 4 * HG_W), lambda i, c: (i, c, COL_HG // (4 * HG_W))),
                pl.BlockSpec((depth, HG_W), lambda i, c: (0, 0)),
                pl.BlockSpec((1, HEAD), lambda i, c: (0, 0))]
    args = [proj, lb_logits.astype(F32), gnorm.reshape(1, HEAD).astype(F32)]
    if has_s0:
        in_specs.append(pl.BlockSpec((1, N_HEADS, HEAD, HEAD), lambda i, c: (i, 0, 0, 0)))
        args.append(s0)
    return pl.pallas_call(
        functools.partial(_hgrn_body, layer=layer, t_valid=t_valid, has_s0=has_s0),
        grid=(b, n_chunks),
        in_specs=in_specs,
        out_specs=[pl.BlockSpec((1, t_blk, HG_W), lambda i, c: (i, c, 0)),
                   pl.BlockSpec((1, N_HEADS, HEAD, HEAD), lambda i, c: (i, 0, 0, 0))],
        out_shape=[jax.ShapeDtypeStruct((b, t, HG_W), BF16),
                   jax.ShapeDtypeStruct((b, N_HEADS, HEAD, HEAD), F32)],
        scratch_shapes=[pltpu.VMEM((cs, HG_W), F32)] * 4 + [pltpu.VMEM((N_HEADS, HEAD, HEAD), F32)],
        compiler_params=_cparams(("parallel", "arbitrary")),
        name="hgrn2",
    )(*args)


def _lane_blocks(cols, width):
    rows = cols[0].shape[0]
    lane = lax.broadcasted_iota(jnp.int32, (rows, width * len(cols)), 1)
    out = jnp.broadcast_to(cols[-1], (rows, width * len(cols)))
    for j in range(len(cols) - 2, -1, -1):
        out = jnp.where(lane < (j + 1) * width, cols[j], out)
    return out


def _ssd_body(*refs, t_valid, has_s0):
    it = iter(refs)
    xbc_ref = next(it)
    z_ref = next(it)
    dts_ref = next(it)
    cw_ref = next(it)
    cb_ref = next(it)
    dtb_ref = next(it)
    alog_ref = next(it)
    dsk_ref = next(it)
    gn_ref = next(it)
    cs0_ref = next(it) if has_s0 else None
    s0_ref = next(it) if has_s0 else None
    y_ref = next(it)
    st_ref = next(it)
    xp_s = next(it)
    act_s = next(it)
    c = pl.program_id(1)
    cs = SSD_CHUNK
    t_blk = y_ref.shape[1]
    tail0 = 8 - (CONV_W - 1)

    @pl.when(c == 0)
    def _():
        if has_s0:
            xp_s[tail0:8, :] = cs0_ref[0].astype(F32)
            st_ref[...] = s0_ref[...].astype(F32)
        else:
            xp_s[tail0:8, :] = jnp.zeros((CONV_W - 1, CONV_DIM), F32)
            st_ref[...] = jnp.zeros(st_ref.shape, F32)

    xp_s[8:8 + cs, :] = _rows_of_chunk(xbc_ref, slice(0, CONV_DIM), cs)
    conv = cb_ref[...]
    for j in range(CONV_W):
        conv = conv + xp_s[tail0 + j:tail0 + j + cs, :] * cw_ref[j:j + 1, :]
    act_s[...] = _silu(conv)
    xp_s[tail0:8, :] = xp_s[cs + tail0:cs + 8, :]

    dt = _softplus(_rows_of_chunk(dts_ref, slice(0, 128), cs) + dtb_ref[...])
    if t_valid is not None:
        live = (c * cs + lax.broadcasted_iota(jnp.int32, (cs, 1), 0)) < t_valid
        dt = jnp.where(live, dt, 0.0)
    a = _cumsum_rows(dt * (-jnp.exp(alog_ref[...])))
    a_t = a.T
    dt_t = dt.T
    a_end = a[cs - 1:cs, :]
    w_upd = dt * jnp.exp(a_end - a)
    e_a = jnp.exp(a)
    e_end = jnp.exp(a_end)
    causal = _tri_incl(cs)
    lane_g = lax.broadcasted_iota(jnp.int32, (cs, SSM_GW), 1)
    row_g = lax.broadcasted_iota(jnp.int32, (SSM_GW, 1), 0)

    for g in range(SSM_GROUPS):
        xs = slice(g * SSM_GW, (g + 1) * SSM_GW)
        bsl = slice(SSM_W + g * SSM_N, SSM_W + (g + 1) * SSM_N)
        csl = slice(SSM_W + SSM_GROUPS * SSM_N + g * SSM_N, SSM_W + SSM_GROUPS * SSM_N + (g + 1) * SSM_N)
        xg = act_s[:, xs]
        bm = act_s[:, bsl].astype(BF16)
        cm = act_s[:, csl].astype(BF16)
        cbm = _dot_nt(cm, bm)
        s_g = st_ref[0, g]
        lanes = [DT_LANE0 + g * SSM_HPG + j for j in range(SSM_HPG)]
        yg = _dot_nt(cm, s_g.astype(BF16)) * _lane_blocks([e_a[:, ln:ln + 1] for ln in lanes], SSM_P)
        for j, ln in enumerate(lanes):
            diff = a[:, ln:ln + 1] - a_t[ln:ln + 1, :]
            seg = jnp.where(causal, jnp.exp(jnp.where(causal, diff, 0.0)), 0.0)
            mh = cbm * seg * dt_t[ln:ln + 1, :]
            xm = jnp.where(jnp.logical_and(lane_g >= j * SSM_P, lane_g < (j + 1) * SSM_P), xg, 0.0)
            yg = yg + _dot(mh.astype(BF16), xm.astype(BF16))
        xw = xg * _lane_blocks([w_upd[:, ln:ln + 1] for ln in lanes], SSM_P)
        decay = jnp.broadcast_to(e_end[:, lanes[-1]:lanes[-1] + 1], (SSM_GW, 1))
        for j in range(SSM_HPG - 2, -1, -1):
            decay = jnp.where(row_g < (j + 1) * SSM_P, e_end[:, lanes[j]:lanes[j] + 1], decay)
        st_ref[0, g] = s_g * decay + _dot_tn(xw.astype(BF16), bm)

        y = yg + dsk_ref[:, xs] * xg
        y = y * _silu(_rows_of_chunk(z_ref, xs, cs))
        ms = jnp.mean(y * y, axis=-1, keepdims=True)
        res = y * lax.rsqrt(ms + EPS) * gn_ref[:, xs]
        y_ref[0, :, xs] = res[0:t_blk].astype(y_ref.dtype)


def _ssd(proj, small, conv_w, conv_b, dt_bias, a_log, d_skip, gnorm, conv_s0, s0):
    b, t, _ = proj.shape
    cs = SSD_CHUNK
    assert t == 1 or t % cs == 0
    t_blk, n_chunks, t_valid = (1, 1, 1) if t == 1 else (cs, t // cs, None)
    has_s0 = s0 is not None
    pad = jnp.zeros((DT_LANE0,), F32)
    tail = jnp.zeros((128 - DT_LANE0 - SSM_HEADS,), F32)
    on_dt_lanes = lambda v: jnp.concatenate([pad, v.astype(F32), tail]).reshape(1, 128)
    const = lambda shape: pl.BlockSpec(shape, lambda i, c: (0,) * len(shape))
    in_specs = [pl.BlockSpec((1, t_blk, CONV_DIM), lambda i, c: (i, c, COL_XBC // CONV_DIM)),
                pl.BlockSpec((1, t_blk, SSM_W), lambda i, c: (i, c, COL_Z // SSM_W)),
                pl.BlockSpec((1, t_blk, 128), lambda i, c: (i, c, 0)),
                const((CONV_W, CONV_DIM)), const((1, CONV_DIM)), const((1, 128)), const((1, 128)),
                const((1, SSM_W)), const((1, SSM_W))]
    args = [proj, proj, small, conv_w.astype(F32), conv_b.reshape(1, CONV_DIM).astype(F32),
            on_dt_lanes(dt_bias), on_dt_lanes(a_log),
            jnp.repeat(d_skip.astype(F32), SSM_P).reshape(1, SSM_W), gnorm.reshape(1, SSM_W).astype(F32)]
    if has_s0:
        in_specs += [pl.BlockSpec((1, CONV_W - 1, CONV_DIM), lambda i, c: (i, 0, 0)),
                     pl.BlockSpec((1, SSM_GROUPS, SSM_GW, SSM_N), lambda i, c: (i, 0, 0, 0))]
        args += [conv_s0, s0.reshape(b, SSM_GROUPS, SSM_GW, SSM_N)]
    y, st = pl.pallas_call(
        functools.partial(_ssd_body, t_valid=t_valid, has_s0=has_s0),
        grid=(b, n_chunks),
        in_specs=in_specs,
        out_specs=[pl.BlockSpec((1, t_blk, SSM_W), lambda i, c: (i, c, 0)),
                   pl.BlockSpec((1, SSM_GROUPS, SSM_GW, SSM_N), lambda i, c: (i, 0, 0, 0))],
        out_shape=[jax.ShapeDtypeStruct((b, t, SSM_W), BF16),
                   jax.ShapeDtypeStruct((b, SSM_GROUPS, SSM_GW, SSM_N), F32)],
        scratch_shapes=[pltpu.VMEM((cs + 8, CONV_DIM), F32), pltpu.VMEM((cs, CONV_DIM), F32)],
        compiler_params=_cparams(("parallel", "arbitrary")),
        name="ssd",
    )(*args)
    return y, st.reshape(b, SSM_HEADS, SSM_P, SSM_N)


def _xattn_body(q_ref, k_ref, v_ref, gq_ref, o_ref):
    scale = HEAD ** -0.5
    g = gq_ref[...]
    for h in range(XA_HEADS):
        sl = slice(h * HEAD, (h + 1) * HEAD)
        q = q_ref[0, :, sl].astype(F32)
        ms = jnp.mean(q * q, axis=-1, keepdims=True)
        qn = q * lax.rsqrt(ms + EPS) * g
        s = _dot_nt(qn.astype(BF16), k_ref[0, :, sl].astype(BF16)) * scale
        p = jnp.exp(s - jnp.max(s, axis=-1, keepdims=True))
        o = _dot(p.astype(BF16), v_ref[0, :, sl].astype(BF16)) / jnp.sum(p, axis=-1, keepdims=True)
        o_ref[0, :, sl] = o.astype(o_ref.dtype)


def _xattn(q, mk, mv, gq):
    b, t, _ = q.shape
    n_mem = mk.shape[1]
    tq = t if t <= 512 else 512
    return pl.pallas_call(
        _xattn_body,
        grid=(b, t // tq),
        in_specs=[pl.BlockSpec((1, tq, XA_W), lambda i, j: (i, j, 0)),
                  pl.BlockSpec((1, n_mem, XA_W), lambda i, j: (i, 0, 0)),
                  pl.BlockSpec((1, n_mem, XA_W), lambda i, j: (i, 0, 0)),
                  pl.BlockSpec((1, HEAD), lambda i, j: (0, 0))],
        out_specs=pl.BlockSpec((1, tq, XA_W), lambda i, j: (i, j, 0)),
        out_shape=jax.ShapeDtypeStruct((b, t, XA_W), BF16),
        compiler_params=_cparams(("parallel", "parallel")),
        name="xattn",
    )(q, mk, mv, gq.reshape(1, HEAD).astype(F32))


def _split_w_in(w):
    o = 0
    parts = {}
    for name, width in (("fq", FOX_W), ("fk", FOX_W), ("fv", FOX_W), ("fg", N_HEADS), ("hg", 4 * HG_W),
                        ("z", SSM_W), ("xbc", CONV_DIM), ("dt", SSM_HEADS)):
        parts[name] = w[:, o:o + width]
        o += width
    main = jnp.concatenate([parts["xbc"], parts["hg"], parts["z"], parts["fq"], parts["fk"], parts["fv"]], axis=1)
    small = jnp.concatenate([parts["fg"], parts["dt"],
                             jnp.zeros((w.shape[0], 128 - N_HEADS - SSM_HEADS), w.dtype)], axis=1)
    return main.astype(BF16), small.astype(BF16)


def _mixers(x, lw, l, *, fox_past, hg_s0, ssm_s0, conv_s0, page_table):
    b, t, d = x.shape
    x2 = x.reshape(b * t, d)
    proj2 = _matmul(x2, lw["w_main"], gain=lw["norm_mix"], name="in_proj")
    small2 = _matmul(x2, lw["w_small"], gain=lw["norm_mix"], name="in_proj_small")
    proj = proj2.reshape(b, t, MAIN_W)
    small = small2.reshape(b, t, 128)

    qn = _headnorm(proj2, lw["fox_gq"], col0=COL_FQ, width=FOX_W, out_dtype=BF16 if fox_past is None else F32)
    kn = _headnorm(proj2, lw["fox_gk"], col0=COL_FK, width=FOX_W)
    fv = proj[:, :, COL_FV:COL_FV + FOX_W]
    gates_t = jnp.swapaxes(small[:, :, 0:N_HEADS], 1, 2)
    logf_t, cum_t = _fox_gate(gates_t, lw["fox_bf"])
    logf = jnp.swapaxes(logf_t, 1, 2)

    if fox_past is None:
        fo = _fox_flash(qn.reshape(b, t, FOX_W), kn.reshape(b, t, FOX_W), proj,
                        jnp.swapaxes(cum_t, 1, 2), cum_t)
    else:
        cache_k, cache_v, cache_lf = fox_past
        fo = _fox_decode(l, qn.reshape(b, N_HEADS, HEAD), kn.reshape(b, N_HEADS, HEAD),
                         fv.reshape(b, N_HEADS, HEAD), logf_t, cache_k, cache_v, cache_lf, page_table)
        fo = fo.reshape(b, t, FOX_W)

    ho, hg_state = _hgrn(proj, lw["hg_lb_logits"], lw["hg_gnorm"], hg_s0, layer=l)
    sy, ssm_state = _ssd(proj, small, lw["conv_w"], lw["conv_b"], lw["dt_bias"], lw["a_log"],
                         lw["d_skip"], lw["ssm_gnorm"], conv_s0, ssm_s0)
    x_new = _out_proj(fo.reshape(b * t, FOX_W), ho.reshape(b * t, HG_W), sy.reshape(b * t, SSM_W),
                      lw["w_out"], x2).reshape(b, t, d)

    xbc = proj[:, :, COL_XBC:COL_XBC + CONV_DIM]
    prev = jnp.zeros((b, CONV_W - 1, CONV_DIM), F32) if conv_s0 is None else conv_s0.astype(F32)
    conv_state = jnp.concatenate([prev, xbc], axis=1)[:, t:]
    fk = kn.reshape(b, t, N_HEADS, HEAD)
    return x_new, fk, fv.reshape(b, t, N_HEADS, HEAD), logf, hg_state, ssm_state, conv_state


def _cross_and_mlp(x, lw, mk, mv):
    b, t, d = x.shape
    x2 = x.reshape(b * t, d)
    q = _matmul(x2, lw["xa_wq"], gain=lw["norm_xa"], name="xa_q")
    o = _xattn(q.reshape(b, t, XA_W), mk, mv, lw["xa_gq"])
    x2 = _matmul(o.reshape(b * t, XA_W), lw["xa_wo"], res=x2, tm_max=1024, name="xa_out")
    u = _matmul(x2, lw["w_up"], gain=lw["norm_mlp"], act="relu2", out_dtype=BF16, name="mlp_up")
    x2 = _matmul(u, lw["w_down"], res=x2, tm_max=1024, tk=2048, name="mlp_down")
    return x2.reshape(b, t, d)


def kernel(x_prompt, x_sample, cache_fox_k, cache_fox_v, cache_fox_logf, cache_mem_k, cache_mem_v, state_hgrn, state_ssm, state_conv, page_table, mem_prompt, norm_mix, w_in, fox_gq, fox_gk, fox_bf, hg_lb_logits, hg_gnorm, conv_w, conv_b, dt_bias, a_log, d_skip, ssm_gnorm, w_out, norm_xa, norm_mem, xa_wq, xa_wk, xa_wv, xa_gq, xa_gk, xa_wo, norm_mlp, w_up, w_down):
    depth = w_in.shape[0]
    bp = x_prompt.shape[0]
    n_mem = mem_prompt.shape[1]
    pool = cache_fox_k.shape[1]
    cache_k = cache_fox_k.reshape(depth, pool, PAGE * N_HEADS, HEAD)
    cache_v = cache_fox_v.reshape(depth, pool, PAGE * N_HEADS, HEAD)
    cache_lf = cache_fox_logf.reshape(depth, pool, 1, PAGE * N_HEADS)
    mem2 = mem_prompt.reshape(bp * n_mem, -1)

    xp, xs = x_prompt, x_sample
    outs = {k: [] for k in ("p_fk", "p_fv", "p_fl", "p_hg", "p_ss", "p_cv", "p_mk", "p_mv",
                            "s_fk", "s_fv", "s_fl", "s_hg", "s_ss", "s_cv")}
    for l in range(depth):
        w_main, w_small = _split_w_in(w_in[l])
        lw = dict(w_main=w_main, w_small=w_small, norm_mix=norm_mix[l], fox_gq=fox_gq[l], fox_gk=fox_gk[l],
                  fox_bf=fox_bf[l], hg_lb_logits=hg_lb_logits, hg_gnorm=hg_gnorm[l], conv_w=conv_w[l],
                  conv_b=conv_b[l], dt_bias=dt_bias[l], a_log=a_log[l], d_skip=d_skip[l],
                  ssm_gnorm=ssm_gnorm[l], w_out=w_out[l].astype(BF16), norm_xa=norm_xa[l],
                  xa_wq=xa_wq[l].astype(BF16), xa_gq=xa_gq[l], xa_wo=xa_wo[l].astype(BF16),
                  norm_mlp=norm_mlp[l], w_up=w_up[l].astype(BF16), w_down=w_down[l].astype(BF16))

        xp, fk, fv, fl, hg, ss, cv = _mixers(xp, lw, l, fox_past=None, hg_s0=None, ssm_s0=None,
                                             conv_s0=None, page_table=None)
        w_kv = jnp.concatenate([xa_wk[l], xa_wv[l]], axis=1).astype(BF16)
        kv = _matmul(mem2, w_kv, gain=norm_mem[l], name="mem_kv")
        mk = _headnorm(kv, xa_gk[l], col0=0, width=XA_W).reshape(bp, n_mem, XA_W)
        mv = kv[:, XA_W:].reshape(bp, n_mem, XA_W)
        xp = _cross_and_mlp(xp, lw, mk, mv)
        for key, val in zip(("p_fk", "p_fv", "p_fl", "p_hg", "p_ss", "p_cv"), (fk, fv, fl, hg, ss, cv)):
            outs[key].append(val)
        outs["p_mk"].append(mk.reshape(bp, n_mem, XA_HEADS, HEAD))
        outs["p_mv"].append(mv.reshape(bp, n_mem, XA_HEADS, HEAD))

        xs, fk, fv, fl, hg, ss, cv = _mixers(xs, lw, l, fox_past=(cache_k, cache_v, cache_lf),
                                             hg_s0=state_hgrn[l], ssm_s0=state_ssm[l],
                                             conv_s0=state_conv[l], page_table=page_table)
        bs = xs.shape[0]
        xs = _cross_and_mlp(xs, lw, cache_mem_k[l].reshape(bs, n_mem, XA_W),
                            cache_mem_v[l].reshape(bs, n_mem, XA_W))
        for key, val in zip(("s_fk", "s_fv", "s_fl", "s_hg", "s_ss", "s_cv"), (fk, fv, fl, hg, ss, cv)):
            outs[key].append(val)

    st = {k: jnp.stack(v) for k, v in outs.items()}
    return (xp, xs, st["p_fk"], st["p_fv"], st["p_fl"], st["p_hg"], st["p_ss"], st["p_cv"], st["p_mk"],
            st["p_mv"], st["s_fk"], st["s_fv"], st["s_fl"], st["s_hg"], st["s_ss"], st["s_cv"])
```

```python
import functools

import jax
import jax.numpy as jnp
from jax import lax
from jax.experimental import pallas as pl
from jax.experimental.pallas import tpu as pltpu

F32 = jnp.float32
BF16 = jnp.bfloat16

EPS = 1e-6
MASK_VALUE = -1e30
HEAD = 128
N_HEADS = 8
FOX_W = N_HEADS * HEAD
HG_W = N_HEADS * HEAD
SSM_W = 2048
SSM_P = 64
SSM_HEADS = SSM_W // SSM_P
SSM_GROUPS = 8
SSM_HPG = SSM_HEADS // SSM_GROUPS
SSM_N = 128
SSM_GW = SSM_HPG * SSM_P
CONV_W = 4
CONV_DIM = SSM_W + 2 * SSM_GROUPS * SSM_N
XA_HEADS = 4
XA_W = XA_HEADS * HEAD
HG_CHUNK = 64
HG_SUB = 16
SSD_CHUNK = 128
PAGE = 128
DT_LANE0 = N_HEADS

COL_XBC = 0
COL_HG = CONV_DIM
COL_Z = COL_HG + 4 * HG_W
COL_FQ = COL_Z + SSM_W
COL_FK = COL_FQ + FOX_W
COL_FV = COL_FK + FOX_W
MAIN_W = COL_FV + FOX_W

V7X_VMEM_LIMIT = 56 * 1024 * 1024


def _cparams(sem, vmem=V7X_VMEM_LIMIT):
    return pltpu.CompilerParams(dimension_semantics=sem, vmem_limit_bytes=vmem)


def _sigmoid(x):
    return 1.0 / (1.0 + jnp.exp(-x))


def _silu(x):
    return x * _sigmoid(x)


def _softplus(x):
    return jnp.maximum(x, 0.0) + jnp.log1p(jnp.exp(-jnp.abs(x)))


def _log_sigmoid(x):
    return -_softplus(-x)


def _dot_nt(a, b):
    return lax.dot_general(a, b, (((1,), (1,)), ((), ())), preferred_element_type=F32)


def _dot_tn(a, b):
    return lax.dot_general(a, b, (((0,), (0,)), ((), ())), preferred_element_type=F32)


def _dot(a, b):
    return jnp.dot(a, b, preferred_element_type=F32)


def _tri_incl(n):
    r = lax.broadcasted_iota(jnp.int32, (n, n), 0)
    c = lax.broadcasted_iota(jnp.int32, (n, n), 1)
    return r >= c


def _cumsum_rows(x):
    n = x.shape[0]
    tri = _tri_incl(n).astype(BF16)
    hi = x.astype(BF16)
    r1 = x - hi.astype(F32)
    mid = r1.astype(BF16)
    lo = (r1 - mid.astype(F32)).astype(BF16)
    return _dot(tri, hi) + _dot(tri, mid) + _dot(tri, lo)


def _mm_body(*refs, nk, norm, act, has_res):
    it = iter(refs)
    a_ref = next(it)
    g_ref = next(it) if norm else None
    w_ref = next(it)
    r_ref = next(it) if has_res else None
    o_ref = next(it)
    h_ref = next(it) if norm else None
    acc_ref = next(it) if nk > 1 else None
    j = pl.program_id(1)
    k = pl.program_id(2)

    if norm:
        @pl.when(j == 0)
        def _():
            x = a_ref[...].astype(F32)
            ms = jnp.mean(x * x, axis=-1, keepdims=True)
            h_ref[...] = (x * lax.rsqrt(ms + EPS) * g_ref[...]).astype(BF16)
        a = h_ref[...]
    else:
        a = a_ref[...]

    p = _dot(a, w_ref[...])

    def finish(v):
        if act == "relu2":
            v = jnp.square(jnp.maximum(v, 0.0))
        if has_res:
            v = v + r_ref[...]
        o_ref[...] = v.astype(o_ref.dtype)

    if nk == 1:
        finish(p)
    else:
        @pl.when(k == 0)
        def _():
            acc_ref[...] = p

        @pl.when(k > 0)
        def _():
            acc_ref[...] += p

        @pl.when(k == nk - 1)
        def _():
            finish(acc_ref[...])


def _pick_tile(n, candidates):
    for c in candidates:
        if n % c == 0:
            return c
    return n


def _matmul(a, w, *, gain=None, res=None, act=None, out_dtype=F32, tm_max=512, tk=None, name="matmul"):
    m, kdim = a.shape
    n = w.shape[1]
    small_m = m <= 64
    tm = m if small_m else _pick_tile(m, tuple(c for c in (1024, 512, 256, 128) if c <= tm_max))
    tn = _pick_tile(n, (2048, 1024, 512, 256, 128)) if small_m else _pick_tile(n, (1024, 512, 256, 128))
    tk = kdim if tk is None else tk
    nk = kdim // tk
    norm = gain is not None
    assert not (norm and nk > 1)
    in_specs = [pl.BlockSpec((tm, tk), lambda i, j, k: (i, k))]
    args = [a]
    if norm:
        in_specs.append(pl.BlockSpec((1, kdim), lambda i, j, k: (0, 0)))
        args.append(gain.reshape(1, kdim).astype(F32))
    in_specs.append(pl.BlockSpec((tk, tn), lambda i, j, k: (k, j)))
    args.append(w)
    if res is not None:
        in_specs.append(pl.BlockSpec((tm, tn), lambda i, j, k: (i, j)))
        args.append(res)
    scratch = []
    if norm:
        scratch.append(pltpu.VMEM((tm, kdim), BF16))
    if nk > 1:
        scratch.append(pltpu.VMEM((tm, tn), F32))
    return pl.pallas_call(
        functools.partial(_mm_body, nk=nk, norm=norm, act=act, has_res=res is not None),
        grid=(m // tm, n // tn, nk),
        in_specs=in_specs,
        out_specs=pl.BlockSpec((tm, tn), lambda i, j, k: (i, j)),
        out_shape=jax.ShapeDtypeStruct((m, n), out_dtype),
        scratch_shapes=scratch,
        compiler_params=_cparams(("parallel", "arbitrary", "arbitrary")),
        name=name,
    )(*args)


def _outproj_body(fo_ref, ho_ref, sy_ref, w0_ref, w1_ref, w2_ref, r_ref, o_ref):
    p = _dot(fo_ref[...], w0_ref[...]) + _dot(ho_ref[...], w1_ref[...]) + _dot(sy_ref[...], w2_ref[...])
    o_ref[...] = p + r_ref[...]


def _out_proj(fo, ho, sy, w, res):
    m = fo.shape[0]
    n = w.shape[1]
    tm = m if m <= 64 else _pick_tile(m, (1024, 512, 256, 128))
    tn = _pick_tile(n, (1024, 512, 256, 128))
    assert fo.shape[1] == FOX_W and ho.shape[1] == HG_W and sy.shape[1] == SSM_W and FOX_W == HG_W
    return pl.pallas_call(
        _outproj_body,
        grid=(m // tm, n // tn),
        in_specs=[pl.BlockSpec((tm, FOX_W), lambda i, j: (i, 0)),
                  pl.BlockSpec((tm, HG_W), lambda i, j: (i, 0)),
                  pl.BlockSpec((tm, SSM_W), lambda i, j: (i, 0)),
                  pl.BlockSpec((FOX_W, tn), lambda i, j: (0, j)),
                  pl.BlockSpec((HG_W, tn), lambda i, j: (1, j)),
                  pl.BlockSpec((SSM_W, tn), lambda i, j: ((FOX_W + HG_W) // SSM_W, j)),
                  pl.BlockSpec((tm, tn), lambda i, j: (i, j))],
        out_specs=pl.BlockSpec((tm, tn), lambda i, j: (i, j)),
        out_shape=jax.ShapeDtypeStruct((m, n), F32),
        compiler_params=_cparams(("parallel", "arbitrary")),
        name="out_proj",
    )(fo, ho, sy, w, w, w, res)


def _headnorm_body(x_ref, g_ref, o_ref, *, n_heads):
    g = g_ref[...]
    for h in range(n_heads):
        sl = slice(h * HEAD, (h + 1) * HEAD)
        x = x_ref[:, sl].astype(F32)
        ms = jnp.mean(x * x, axis=-1, keepdims=True)
        o_ref[:, sl] = (x * lax.rsqrt(ms + EPS) * g).astype(o_ref.dtype)


def _headnorm(x, gain, *, col0, width, out_dtype=F32):
    m = x.shape[0]
    tm = m if m <= 64 else _pick_tile(m, (512, 256, 128))
    return pl.pallas_call(
        functools.partial(_headnorm_body, n_heads=width // HEAD),
        grid=(m // tm,),
        in_specs=[pl.BlockSpec((tm, width), lambda i: (i, col0 // width)),
                  pl.BlockSpec((1, HEAD), lambda i: (0, 0))],
        out_specs=pl.BlockSpec((tm, width), lambda i: (i, 0)),
        out_shape=jax.ShapeDtypeStruct((m, width), out_dtype),
        compiler_params=_cparams(("parallel",)),
        name="headnorm",
    )(x, gain.reshape(1, HEAD).astype(F32))


def _fox_gate_body(g_ref, bf_ref, lf_ref, cum_ref, *, t):
    lf = _log_sigmoid(g_ref[0] + bf_ref[...])
    lf_ref[0] = lf
    lane = lax.broadcasted_iota(jnp.int32, lf.shape, 1)
    c = lf
    shift = 1
    while shift < t:
        c = c + jnp.where(lane >= shift, pltpu.roll(c, shift, axis=1), 0.0)
        shift *= 2
    cum_ref[0] = c


def _fox_gate(gates_t, fox_bf):
    b, h, t = gates_t.shape
    spec = pl.BlockSpec((1, h, t), lambda i: (i, 0, 0))
    return pl.pallas_call(
        functools.partial(_fox_gate_body, t=t),
        grid=(b,),
        in_specs=[spec, pl.BlockSpec((h, 1), lambda i: (0, 0))],
        out_specs=[spec, spec],
        out_shape=[jax.ShapeDtypeStruct((b, h, t), F32)] * 2,
        compiler_params=_cparams(("parallel",)),
        name="fox_gate",
    )(gates_t, fox_bf.reshape(h, 1).astype(F32))


FLASH_STRIP = 256


def _fox_flash_body(q_ref, k_ref, v_ref, cq_ref, ck_ref, o_ref, m_ref, l_ref, acc_ref, *, tq):
    qi = pl.program_id(1)
    ki = pl.program_id(2)
    scale = HEAD ** -0.5

    @pl.when(ki == 0)
    def _():
        m_ref[...] = jnp.full(m_ref.shape, MASK_VALUE, F32)
        l_ref[...] = jnp.zeros(l_ref.shape, F32)
        acc_ref[...] = jnp.zeros(acc_ref.shape, F32)

    def update(diagonal):
        for r0 in range(0, tq, FLASH_STRIP):
            rows = slice(r0, r0 + FLASH_STRIP)
            if diagonal:
                keep = (r0 + lax.broadcasted_iota(jnp.int32, (FLASH_STRIP, tq), 0)
                        >= lax.broadcasted_iota(jnp.int32, (FLASH_STRIP, tq), 1))
            for h in range(N_HEADS):
                sl = slice(h * HEAD, (h + 1) * HEAD)
                s = _dot_nt(q_ref[0, rows, sl], k_ref[0, :, sl]) * scale
                s = s + cq_ref[0, rows, h:h + 1] - ck_ref[0, h:h + 1, :]
                if diagonal:
                    s = jnp.where(keep, s, MASK_VALUE)
                m_prev = m_ref[h, rows]
                m_new = jnp.maximum(m_prev, jnp.max(s, axis=-1, keepdims=True))
                alpha = jnp.exp(m_prev - m_new)
                p = jnp.exp(s - m_new)
                l_ref[h, rows] = alpha * l_ref[h, rows] + jnp.sum(p, axis=-1, keepdims=True)
                acc_ref[rows, sl] = alpha * acc_ref[rows, sl] + _dot(p.astype(BF16), v_ref[0, :, sl])
                m_ref[h, rows] = m_new

    @pl.when(ki < qi)
    def _():
        update(diagonal=False)

    @pl.when(ki == qi)
    def _():
        update(diagonal=True)
        for h in range(N_HEADS):
            sl = slice(h * HEAD, (h + 1) * HEAD)
            o_ref[0, :, sl] = (acc_ref[:, sl] / l_ref[h]).astype(o_ref.dtype)


def _fox_prep_body(q_ref, k_ref, v_ref, gq_ref, gk_ref, qn_ref, kn_ref, knb_ref, vf_ref, vb_ref):
    for h in range(N_HEADS):
        sl = slice(h * HEAD, (h + 1) * HEAD)
        q = q_ref[:, sl]
        qn_ref[:, sl] = (q * lax.rsqrt(jnp.mean(q * q, axis=-1, keepdims=True) + EPS) * gq_ref[...]).astype(BF16)
        k = k_ref[:, sl]
        kn = k * lax.rsqrt(jnp.mean(k * k, axis=-1, keepdims=True) + EPS) * gk_ref[...]
        kn_ref[:, sl] = kn
        knb_ref[:, sl] = kn.astype(BF16)
    v = v_ref[...]
    vf_ref[...] = v
    vb_ref[...] = v.astype(BF16)


def _fox_prep(proj2, gq, gk):
    m = proj2.shape[0]
    tm = _pick_tile(m, (512, 256, 128))
    col = lambda c: pl.BlockSpec((tm, FOX_W), lambda i: (i, c // FOX_W))
    out = pl.BlockSpec((tm, FOX_W), lambda i: (i, 0))
    gain = pl.BlockSpec((1, HEAD), lambda i: (0, 0))
    return pl.pallas_call(
        _fox_prep_body,
        grid=(m // tm,),
        in_specs=[col(COL_FQ), col(COL_FK), col(COL_FV), gain, gain],
        out_specs=[out] * 5,
        out_shape=[jax.ShapeDtypeStruct((m, FOX_W), dt) for dt in (BF16, F32, BF16, F32, BF16)],
        compiler_params=_cparams(("parallel",)),
        name="fox_prep",
    )(proj2, proj2, proj2, gq.reshape(1, HEAD).astype(F32), gk.reshape(1, HEAD).astype(F32))


def _fox_flash(qn, kn, vb, cum_col, cum_row, *, tq=512):
    b, t, _ = qn.shape
    nq = t // tq
    return pl.pallas_call(
        functools.partial(_fox_flash_body, tq=tq),
        grid=(b, nq, nq),
        in_specs=[
            pl.BlockSpec((1, tq, FOX_W), lambda i, q, k: (i, q, 0)),
            pl.BlockSpec((1, tq, FOX_W), lambda i, q, k: (i, jnp.minimum(k, q), 0)),
            pl.BlockSpec((1, tq, FOX_W), lambda i, q, k: (i, jnp.minimum(k, q), 0)),
            pl.BlockSpec((1, tq, N_HEADS), lambda i, q, k: (i, q, 0)),
            pl.BlockSpec((1, N_HEADS, tq), lambda i, q, k: (i, 0, jnp.minimum(k, q))),
        ],
        out_specs=pl.BlockSpec((1, tq, FOX_W), lambda i, q, k: (i, q, 0)),
        out_shape=jax.ShapeDtypeStruct((b, t, FOX_W), BF16),
        scratch_shapes=[pltpu.VMEM((N_HEADS, tq, 1), F32), pltpu.VMEM((N_HEADS, tq, 1), F32),
                        pltpu.VMEM((tq, FOX_W), F32)],
        compiler_params=_cparams(("parallel", "parallel", "arbitrary")),
        name="fox_flash",
    )(qn, kn, vb, cum_col, cum_row)


def _fox_decode_body(pt_ref, q_ref, kn_ref, vn_ref, lfn_ref, *rest, pages_per_step, n_steps):
    del pt_ref
    pps = pages_per_step
    k_refs = rest[:pps]
    v_refs = rest[pps:2 * pps]
    lf_refs = rest[2 * pps:3 * pps]
    o_ref, m_ref, l_ref, acc_ref, carry_ref = rest[3 * pps:]
    s = pl.program_id(1)
    scale = HEAD ** -0.5
    rows = PAGE * N_HEADS
    lane = lax.broadcasted_iota(jnp.int32, (1, rows), 1)
    own = (lax.broadcasted_iota(jnp.int32, (N_HEADS, rows), 1) % N_HEADS
           == lax.broadcasted_iota(jnp.int32, (N_HEADS, rows), 0))

    @pl.when(s == 0)
    def _():
        m_ref[...] = jnp.sum(q_ref[0] * kn_ref[0], axis=-1, keepdims=True) * scale
        l_ref[...] = jnp.ones(l_ref.shape, F32)
        acc_ref[...] = vn_ref[0]
        carry_ref[...] = jnp.zeros(carry_ref.shape, F32)

    lf = jnp.concatenate([lf_refs[r][0, 0] for r in range(pps)], axis=0)
    tot = lf
    suf = lf
    shift = N_HEADS
    while shift < rows:
        tot = tot + pltpu.roll(tot, shift, axis=1)
        suf = suf + jnp.where(lane + shift < rows, pltpu.roll(suf, rows - shift, axis=1), 0.0)
        shift *= 2
    page = lax.broadcasted_iota(jnp.int32, (pps, 1), 0)
    newer = tot
    shift = 1
    while shift < pps:
        newer = newer + jnp.where(page >= shift, pltpu.roll(newer, shift, axis=0), 0.0)
        shift *= 2
    after = suf - lf + (newer - tot) + carry_ref[...]
    carry_ref[...] = carry_ref[...] + newer[pps - 1:pps, :]

    q = q_ref[0].astype(BF16)
    scs = []
    for r in range(pps):
        sc = _dot_nt(q, k_refs[r][0, 0].astype(BF16)) * scale + lfn_ref[0] + after[r:r + 1, :]
        scs.append(jnp.where(own, sc, MASK_VALUE))
    m_prev = m_ref[...]
    m_new = m_prev
    for sc in scs:
        m_new = jnp.maximum(m_new, jnp.max(sc, axis=-1, keepdims=True))
    alpha = jnp.exp(m_prev - m_new)
    l_new = alpha * l_ref[...]
    acc = alpha * acc_ref[...]
    for r, sc in enumerate(scs):
        p = jnp.exp(sc - m_new)
        l_new = l_new + jnp.sum(p, axis=-1, keepdims=True)
        acc = acc + _dot(p.astype(BF16), v_refs[r][0, 0].astype(BF16))
    l_ref[...] = l_new
    acc_ref[...] = acc
    m_ref[...] = m_new

    @pl.when(s == n_steps - 1)
    def _():
        o_ref[0] = (acc_ref[...] / l_ref[...]).astype(o_ref.dtype)


def _fox_decode(layer, qn, kn, v_new, lf_new, cache_k, cache_v, cache_lf, page_table, *, pages_per_step=8):
    b = qn.shape[0]
    n_pages = page_table.shape[1]
    pps = pages_per_step
    n_steps = n_pages // pps
    rows = PAGE * N_HEADS

    def page_map(r):
        return lambda i, s, pt: (layer, pt[i, n_pages - 1 - (s * pps + r)], 0, 0)

    head = pl.BlockSpec((1, N_HEADS, HEAD), lambda i, s, pt: (i, 0, 0))
    in_specs = [head, head, head, pl.BlockSpec((1, N_HEADS, 1), lambda i, s, pt: (i, 0, 0))]
    in_specs += [pl.BlockSpec((1, 1, rows, HEAD), page_map(r)) for r in range(pps)]
    in_specs += [pl.BlockSpec((1, 1, rows, HEAD), page_map(r)) for r in range(pps)]
    in_specs += [pl.BlockSpec((1, 1, 1, rows), page_map(r)) for r in range(pps)]
    grid_spec = pltpu.PrefetchScalarGridSpec(
        num_scalar_prefetch=1, grid=(b, n_steps), in_specs=in_specs,
        out_specs=pl.BlockSpec((1, N_HEADS, HEAD), lambda i, s, pt: (i, 0, 0)),
        scratch_shapes=[pltpu.VMEM((N_HEADS, 1), F32), pltpu.VMEM((N_HEADS, 1), F32),
                        pltpu.VMEM((N_HEADS, HEAD), F32), pltpu.VMEM((1, rows), F32)])
    return pl.pallas_call(
        functools.partial(_fox_decode_body, pages_per_step=pps, n_steps=n_steps),
        grid_spec=grid_spec,
        out_shape=jax.ShapeDtypeStruct((b, N_HEADS, HEAD), BF16),
        compiler_params=_cparams(("parallel", "arbitrary")),
        name="fox_decode",
    )(page_table, qn, kn, v_new, lf_new, *([cache_k] * pps), *([cache_v] * pps), *([cache_lf] * pps))


def _rows_of_chunk(ref, cols, chunk_rows):
    x = ref[0, :, cols].astype(F32)
    if x.shape[0] == chunk_rows:
        return x
    assert x.shape[0] == 1
    return jnp.broadcast_to(x, (chunk_rows, x.shape[1]))


def _hgrn_body(*refs, layer, t_valid, has_s0):
    it = iter(refs)
    x_ref = next(it)
    lbl_ref = next(it)
    gn_ref = next(it)
    s0_ref = next(it) if has_s0 else None
    o_ref = next(it)
    st_ref = next(it)
    q_s, k_s, v_s, b_s, stt_s = (next(it) for _ in range(5))
    c = pl.program_id(1)
    nc = pl.num_programs(1)
    cs = HG_CHUNK
    t_blk = o_ref.shape[1]

    @pl.when(c == 0)
    def _():
        for h in range(N_HEADS):
            if has_s0:
                stt_s[h] = s0_ref[0, h].astype(F32).T
            else:
                stt_s[h] = jnp.zeros((HEAD, HEAD), F32)

    lg = lbl_ref[...].astype(F32)
    e = jnp.exp(lg - jnp.max(lg, axis=0, keepdims=True))
    pr = e / jnp.sum(e, axis=0, keepdims=True)
    lb = jnp.sum(pr[0:layer + 1], axis=0, keepdims=True) - pr[0:1]

    hf = _rows_of_chunk(x_ref, slice(HG_W, 2 * HG_W), cs)
    logf = jnp.log(lb + (1.0 - lb) * _sigmoid(hf))
    kk = (1.0 - lb) * _sigmoid(-hf)
    qq = _silu(_rows_of_chunk(x_ref, slice(0, HG_W), cs))
    vv = _rows_of_chunk(x_ref, slice(2 * HG_W, 3 * HG_W), cs)
    if t_valid is not None:
        live = (c * cs + lax.broadcasted_iota(jnp.int32, (cs, 1), 0)) < t_valid
        logf = jnp.where(live, logf, 0.0)
        kk = jnp.where(live, kk, 0.0)
        qq = jnp.where(live, qq, 0.0)
        vv = jnp.where(live, vv, 0.0)
    q_s[...] = qq
    k_s[...] = kk
    v_s[...] = vv
    b_s[...] = _cumsum_rows(logf)

    n_sub = cs // HG_SUB
    causal = _tri_incl(cs)
    zeros_sub = jnp.zeros((HG_SUB, HEAD), F32)
    g = gn_ref[...]

    atts = []
    for h in range(N_HEADS):
        sl = slice(h * HEAD, (h + 1) * HEAD)
        q_parts, k_parts = [], []
        for i in range(n_sub):
            r0 = i * HG_SUB
            n_keys = r0 + HG_SUB
            b_i = b_s[r0 - 1:r0, sl] if i > 0 else jnp.zeros((1, HEAD), F32)
            qd = q_s[r0:n_keys, sl] * jnp.exp(b_s[r0:n_keys, sl] - b_i)
            kd = k_s[0:n_keys, sl] * jnp.exp(b_i - b_s[0:n_keys, sl])
            q_parts.append(jnp.concatenate([zeros_sub] * i + [qd] + [zeros_sub] * (n_sub - 1 - i), axis=0))
            k_parts.append(jnp.concatenate([kd] + [zeros_sub] * (n_sub - 1 - i), axis=0))
        q_cat = jnp.concatenate(q_parts, axis=1).astype(BF16)
        k_cat = jnp.concatenate(k_parts, axis=1).astype(BF16)
        atts.append(jnp.where(causal, _dot_nt(q_cat, k_cat), 0.0).astype(BF16))

    for h in range(N_HEADS):
        sl = slice(h * HEAD, (h + 1) * HEAD)
        stt = stt_s[h]
        bh = b_s[:, sl]
        b_end = b_s[cs - 1:cs, sl]
        o = _dot(atts[h], v_s[:, sl].astype(BF16))
        o = o + _dot_nt((q_s[:, sl] * jnp.exp(bh)).astype(BF16), stt.astype(BF16))
        kd_end = k_s[:, sl] * jnp.exp(b_end - bh)
        stt_s[h] = stt * jnp.exp(b_end) + _dot_tn(v_s[:, sl].astype(BF16), kd_end.astype(BF16))
        ms = jnp.mean(o * o, axis=-1, keepdims=True)
        gate = _silu(_rows_of_chunk(x_ref, slice(3 * HG_W + h * HEAD, 3 * HG_W + (h + 1) * HEAD), cs))
        res = o * lax.rsqrt(ms + EPS) * g * gate
        o_ref[0, :, sl] = res[0:t_blk].astype(o_ref.dtype)

    @pl.when(c == nc - 1)
    def _():
        for h in range(N_HEADS):
            st_ref[0, h] = stt_s[h].T


def _hgrn(proj, lb_logits, gnorm, s0, *, layer):
    b, t, _ = proj.shape
    cs = HG_CHUNK
    assert t == 1 or t % cs == 0
    t_blk, n_chunks, t_valid = (1, 1, 1) if t == 1 else (cs, t // cs, None)
    has_s0 = s0 is not None
    depth = lb_logits.shape[0]
    in_specs = [pl.BlockSpec((1, t_blk, 4 * HG_W), lambda i, c: (i, c, COL_HG // (4 * HG_W))),
                pl.BlockSpec((depth, HG_W), lambda i, c: (0, 0)),
                pl.BlockSpec((1, HEAD), lambda i, c: (0, 0))]
    args = [proj, lb_logits.astype(F32), gnorm.reshape(1, HEAD).astype(F32)]
    if has_s0:
        in_specs.append(pl.BlockSpec((1, N_HEADS, HEAD, HEAD), lambda i, c: (i, 0, 0, 0)))
        args.append(s0)
    return pl.pallas_call(
        functools.partial(_hgrn_body, layer=layer, t_valid=t_valid, has_s0=has_s0),
        grid=(b, n_chunks),
        in_specs=in_specs,
        out_specs=[pl.BlockSpec((1, t_blk, HG_W), lambda i, c: (i, c, 0)),
                   pl.BlockSpec((1, N_HEADS, HEAD, HEAD), lambda i, c: (i, 0, 0, 0))],
        out_shape=[jax.ShapeDtypeStruct((b, t, HG_W), BF16),
                   jax.ShapeDtypeStruct((b, N_HEADS, HEAD, HEAD), F32)],
        scratch_shapes=[pltpu.VMEM((cs, HG_W), F32)] * 4 + [pltpu.VMEM((N_HEADS, HEAD, HEAD), F32)],
        compiler_params=_cparams(("parallel", "arbitrary")),
        name="hgrn2",
    )(*args)


def _lane_blocks(cols, width):
    rows = cols[0].shape[0]
    lane = lax.broadcasted_iota(jnp.int32, (rows, width * len(cols)), 1)
    out = jnp.broadcast_to(cols[-1], (rows, width * len(cols)))
    for j in range(len(cols) - 2, -1, -1):
        out = jnp.where(lane < (j + 1) * width, cols[j], out)
    return out


def _ssd_body(*refs, t_valid, has_s0):
    it = iter(refs)
    xbc_ref = next(it)
    z_ref = next(it)
    dts_ref = next(it)
    cw_ref = next(it)
    cb_ref = next(it)
    dtb_ref = next(it)
    alog_ref = next(it)
    dsk_ref = next(it)
    gn_ref = next(it)
    cs0_ref = next(it) if has_s0 else None
    s0_ref = next(it) if has_s0 else None
    y_ref = next(it)
    st_ref = next(it)
    prev_s = next(it)
    act_s = next(it)
    c = pl.program_id(1)
    cs = SSD_CHUNK
    t_blk = y_ref.shape[1]
    tail0 = 8 - (CONV_W - 1)

    @pl.when(c == 0)
    def _():
        prev_s[...] = jnp.zeros(prev_s.shape, F32)
        if has_s0:
            prev_s[tail0:8, :] = cs0_ref[0].astype(F32)
            st_ref[...] = s0_ref[...].astype(F32)
        else:
            st_ref[...] = jnp.zeros(st_ref.shape, F32)

    cur = _rows_of_chunk(xbc_ref, slice(0, CONV_DIM), cs)
    win = jnp.concatenate([prev_s[...], cur[0:8]], axis=0)
    conv = cb_ref[...] + cur * cw_ref[CONV_W - 1:CONV_W, :]
    head = cb_ref[...] + win[8:16] * cw_ref[CONV_W - 1:CONV_W, :]
    for d in range(1, CONV_W):
        w_d = cw_ref[CONV_W - 1 - d:CONV_W - d, :]
        conv = conv + pltpu.roll(cur, d, axis=0) * w_d
        head = head + pltpu.roll(win, d, axis=0)[8:16] * w_d
    act_s[...] = _silu(conv)
    act_s[0:8, :] = _silu(head)
    prev_s[...] = cur[cs - 8:cs]

    dt = _softplus(_rows_of_chunk(dts_ref, slice(0, 128), cs) + dtb_ref[...])
    if t_valid is not None:
        live = (c * cs + lax.broadcasted_iota(jnp.int32, (cs, 1), 0)) < t_valid
        dt = jnp.where(live, dt, 0.0)
    a = _cumsum_rows(dt * (-jnp.exp(alog_ref[...])))
    a_t = a.T
    dt_t = dt.T
    a_end = a[cs - 1:cs, :]
    w_upd = dt * jnp.exp(a_end - a)
    e_a = jnp.exp(a)
    e_end = jnp.exp(a_end)
    causal = _tri_incl(cs)
    lane_g = lax.broadcasted_iota(jnp.int32, (cs, SSM_GW), 1)
    row_g = lax.broadcasted_iota(jnp.int32, (SSM_GW, 1), 0)

    for g in range(SSM_GROUPS):
        xs = slice(g * SSM_GW, (g + 1) * SSM_GW)
        bsl = slice(SSM_W + g * SSM_N, SSM_W + (g + 1) * SSM_N)
        csl = slice(SSM_W + SSM_GROUPS * SSM_N + g * SSM_N, SSM_W + SSM_GROUPS * SSM_N + (g + 1) * SSM_N)
        xg = act_s[:, xs]
        bm = act_s[:, bsl].astype(BF16)
        cm = act_s[:, csl].astype(BF16)
        cbm = _dot_nt(cm, bm)
        s_g = st_ref[0, g]
        lanes = [DT_LANE0 + g * SSM_HPG + j for j in range(SSM_HPG)]
        yg = _dot_nt(cm, s_g.astype(BF16)) * _lane_blocks([e_a[:, ln:ln + 1] for ln in lanes], SSM_P)
        for j, ln in enumerate(lanes):
            diff = a[:, ln:ln + 1] - a_t[ln:ln + 1, :]
            seg = jnp.where(causal, jnp.exp(jnp.where(causal, diff, 0.0)), 0.0)
            mh = cbm * seg * dt_t[ln:ln + 1, :]
            xm = jnp.where(jnp.logical_and(lane_g >= j * SSM_P, lane_g < (j + 1) * SSM_P), xg, 0.0)
            yg = yg + _dot(mh.astype(BF16), xm.astype(BF16))
        xw = xg * _lane_blocks([w_upd[:, ln:ln + 1] for ln in lanes], SSM_P)
        decay = jnp.broadcast_to(e_end[:, lanes[-1]:lanes[-1] + 1], (SSM_GW, 1))
        for j in range(SSM_HPG - 2, -1, -1):
            decay = jnp.where(row_g < (j + 1) * SSM_P, e_end[:, lanes[j]:lanes[j] + 1], decay)
        st_ref[0, g] = s_g * decay + _dot_tn(xw.astype(BF16), bm)

        y = yg + dsk_ref[:, xs] * xg
        y = y * _silu(_rows_of_chunk(z_ref, xs, cs))
        ms = jnp.mean(y * y, axis=-1, keepdims=True)
        res = y * lax.rsqrt(ms + EPS) * gn_ref[:, xs]
        y_ref[0, :, xs] = res[0:t_blk].astype(y_ref.dtype)


def _ssd(proj, small, conv_w, conv_b, dt_bias, a_log, d_skip, gnorm, conv_s0, s0):
    b, t, _ = proj.shape
    cs = SSD_CHUNK
    assert t == 1 or t % cs == 0
    t_blk, n_chunks, t_valid = (1, 1, 1) if t == 1 else (cs, t // cs, None)
    has_s0 = s0 is not None
    pad = jnp.zeros((DT_LANE0,), F32)
    tail = jnp.zeros((128 - DT_LANE0 - SSM_HEADS,), F32)
    on_dt_lanes = lambda v: jnp.concatenate([pad, v.astype(F32), tail]).reshape(1, 128)
    const = lambda shape: pl.BlockSpec(shape, lambda i, c: (0,) * len(shape))
    in_specs = [pl.BlockSpec((1, t_blk, CONV_DIM), lambda i, c: (i, c, COL_XBC // CONV_DIM)),
                pl.BlockSpec((1, t_blk, SSM_W), lambda i, c: (i, c, COL_Z // SSM_W)),
                pl.BlockSpec((1, t_blk, 128), lambda i, c: (i, c, 0)),
                const((CONV_W, CONV_DIM)), const((1, CONV_DIM)), const((1, 128)), const((1, 128)),
                const((1, SSM_W)), const((1, SSM_W))]
    args = [proj, proj, small, conv_w.astype(F32), conv_b.reshape(1, CONV_DIM).astype(F32),
            on_dt_lanes(dt_bias), on_dt_lanes(a_log),
            jnp.repeat(d_skip.astype(F32), SSM_P).reshape(1, SSM_W), gnorm.reshape(1, SSM_W).astype(F32)]
    if has_s0:
        in_specs += [pl.BlockSpec((1, CONV_W - 1, CONV_DIM), lambda i, c: (i, 0, 0)),
                     pl.BlockSpec((1, SSM_GROUPS, SSM_GW, SSM_N), lambda i, c: (i, 0, 0, 0))]
        args += [conv_s0, s0.reshape(b, SSM_GROUPS, SSM_GW, SSM_N)]
    y, st = pl.pallas_call(
        functools.partial(_ssd_body, t_valid=t_valid, has_s0=has_s0),
        grid=(b, n_chunks),
        in_specs=in_specs,
        out_specs=[pl.BlockSpec((1, t_blk, SSM_W), lambda i, c: (i, c, 0)),
                   pl.BlockSpec((1, SSM_GROUPS, SSM_GW, SSM_N), lambda i, c: (i, 0, 0, 0))],
        out_shape=[jax.ShapeDtypeStruct((b, t, SSM_W), BF16),
                   jax.ShapeDtypeStruct((b, SSM_GROUPS, SSM_GW, SSM_N), F32)],
        scratch_shapes=[pltpu.VMEM((8, CONV_DIM), F32), pltpu.VMEM((cs, CONV_DIM), F32)],
        compiler_params=_cparams(("parallel", "arbitrary")),
        name="ssd",
    )(*args)
    return y, st.reshape(b, SSM_HEADS, SSM_P, SSM_N)


def _xattn_body(q_ref, k_ref, v_ref, gq_ref, o_ref):
    scale = HEAD ** -0.5
    g = gq_ref[...]
    for h in range(XA_HEADS):
        sl = slice(h * HEAD, (h + 1) * HEAD)
        q = q_ref[0, :, sl].astype(F32)
        ms = jnp.mean(q * q, axis=-1, keepdims=True)
        qn = q * lax.rsqrt(ms + EPS) * g
        s = _dot_nt(qn.astype(BF16), k_ref[0, :, sl].astype(BF16)) * scale
        p = jnp.exp(s - jnp.max(s, axis=-1, keepdims=True))
        o = _dot(p.astype(BF16), v_ref[0, :, sl].astype(BF16)) / jnp.sum(p, axis=-1, keepdims=True)
        o_ref[0, :, sl] = o.astype(o_ref.dtype)


def _xattn(q, mk, mv, gq):
    b, t, _ = q.shape
    n_mem = mk.shape[1]
    tq = t if t <= 512 else 512
    return pl.pallas_call(
        _xattn_body,
        grid=(b, t // tq),
        in_specs=[pl.BlockSpec((1, tq, XA_W), lambda i, j: (i, j, 0)),
                  pl.BlockSpec((1, n_mem, XA_W), lambda i, j: (i, 0, 0)),
                  pl.BlockSpec((1, n_mem, XA_W), lambda i, j: (i, 0, 0)),
                  pl.BlockSpec((1, HEAD), lambda i, j: (0, 0))],
        out_specs=pl.BlockSpec((1, tq, XA_W), lambda i, j: (i, j, 0)),
        out_shape=jax.ShapeDtypeStruct((b, t, XA_W), BF16),
        compiler_params=_cparams(("parallel", "parallel")),
        name="xattn",
    )(q, mk, mv, gq.reshape(1, HEAD).astype(F32))


def _split_w_in(w):
    o = 0
    parts = {}
    for name, width in (("fq", FOX_W), ("fk", FOX_W), ("fv", FOX_W), ("fg", N_HEADS), ("hg", 4 * HG_W),
                        ("z", SSM_W), ("xbc", CONV_DIM), ("dt", SSM_HEADS)):
        parts[name] = w[:, o:o + width]
        o += width
    main = jnp.concatenate([parts["xbc"], parts["hg"], parts["z"], parts["fq"], parts["fk"], parts["fv"]], axis=1)
    small = jnp.concatenate([parts["fg"], parts["dt"],
                             jnp.zeros((w.shape[0], 128 - N_HEADS - SSM_HEADS), w.dtype)], axis=1)
    return main.astype(BF16), small.astype(BF16)


def _mixers(x, lw, l, *, fox_past, hg_s0, ssm_s0, conv_s0, page_table):
    b, t, d = x.shape
    x2 = x.reshape(b * t, d)
    proj2 = _matmul(x2, lw["w_main"], gain=lw["norm_mix"], name="in_proj")
    small2 = _matmul(x2, lw["w_small"], gain=lw["norm_mix"], name="in_proj_small")
    proj = proj2.reshape(b, t, MAIN_W)
    small = small2.reshape(b, t, 128)

    gates_t = jnp.swapaxes(small[:, :, 0:N_HEADS], 1, 2)
    logf_t, cum_t = _fox_gate(gates_t, lw["fox_bf"])
    logf = jnp.swapaxes(logf_t, 1, 2)

    if fox_past is None:
        qn, kn, knb, fv, vb = _fox_prep(proj2, lw["fox_gq"], lw["fox_gk"])
        fo = _fox_flash(qn.reshape(b, t, FOX_W), knb.reshape(b, t, FOX_W), vb.reshape(b, t, FOX_W),
                        jnp.swapaxes(cum_t, 1, 2), cum_t)
    else:
        qn = _headnorm(proj2, lw["fox_gq"], col0=COL_FQ, width=FOX_W)
        kn = _headnorm(proj2, lw["fox_gk"], col0=COL_FK, width=FOX_W)
        fv = proj2[:, COL_FV:COL_FV + FOX_W]
        cache_k, cache_v, cache_lf = fox_past
        fo = _fox_decode(l, qn.reshape(b, N_HEADS, HEAD), kn.reshape(b, N_HEADS, HEAD),
                         fv.reshape(b, N_HEADS, HEAD), logf_t, cache_k, cache_v, cache_lf, page_table)
        fo = fo.reshape(b, t, FOX_W)

    ho, hg_state = _hgrn(proj, lw["hg_lb_logits"], lw["hg_gnorm"], hg_s0, layer=l)
    sy, ssm_state = _ssd(proj, small, lw["conv_w"], lw["conv_b"], lw["dt_bias"], lw["a_log"],
                         lw["d_skip"], lw["ssm_gnorm"], conv_s0, ssm_s0)
    x_new = _out_proj(fo.reshape(b * t, FOX_W), ho.reshape(b * t, HG_W), sy.reshape(b * t, SSM_W),
                      lw["w_out"], x2).reshape(b, t, d)

    keep = CONV_W - 1
    if t >= keep:
        conv_state = proj[:, t - keep:, COL_XBC:COL_XBC + CONV_DIM]
    else:
        prev = jnp.zeros((b, keep, CONV_DIM), F32) if conv_s0 is None else conv_s0.astype(F32)
        conv_state = jnp.concatenate([prev[:, t:], proj[:, :, COL_XBC:COL_XBC + CONV_DIM]], axis=1)
    fk = kn.reshape(b, t, N_HEADS, HEAD)
    return x_new, fk, fv.reshape(b, t, N_HEADS, HEAD), logf, hg_state, ssm_state, conv_state


def _cross_and_mlp(x, lw, mk, mv):
    b, t, d = x.shape
    x2 = x.reshape(b * t, d)
    q = _matmul(x2, lw["xa_wq"], gain=lw["norm_xa"], name="xa_q")
    o = _xattn(q.reshape(b, t, XA_W), mk, mv, lw["xa_gq"])
    x2 = _matmul(o.reshape(b * t, XA_W), lw["xa_wo"], res=x2, tm_max=1024, name="xa_out")
    u = _matmul(x2, lw["w_up"], gain=lw["norm_mlp"], act="relu2", out_dtype=BF16, name="mlp_up")
    x2 = _matmul(u, lw["w_down"], res=x2, tm_max=1024, tk=2048, name="mlp_down")
    return x2.reshape(b, t, d)


def kernel(x_prompt, x_sample, cache_fox_k, cache_fox_v, cache_fox_logf, cache_mem_k, cache_mem_v, state_hgrn, state_ssm, state_conv, page_table, mem_prompt, norm_mix, w_in, fox_gq, fox_gk, fox_bf, hg_lb_logits, hg_gnorm, conv_w, conv_b, dt_bias, a_log, d_skip, ssm_gnorm, w_out, norm_xa, norm_mem, xa_wq, xa_wk, xa_wv, xa_gq, xa_gk, xa_wo, norm_mlp, w_up, w_down):
    depth = w_in.shape[0]
    bp = x_prompt.shape[0]
    n_mem = mem_prompt.shape[1]
    pool = cache_fox_k.shape[1]
    cache_k = cache_fox_k.reshape(depth, pool, PAGE * N_HEADS, HEAD)
    cache_v = cache_fox_v.reshape(depth, pool, PAGE * N_HEADS, HEAD)
    cache_lf = cache_fox_logf.reshape(depth, pool, 1, PAGE * N_HEADS)
    mem2 = mem_prompt.reshape(bp * n_mem, -1)

    xp, xs = x_prompt, x_sample
    outs = {k: [] for k in ("p_fk", "p_fv", "p_fl", "p_hg", "p_ss", "p_cv", "p_mk", "p_mv",
                            "s_fk", "s_fv", "s_fl", "s_hg", "s_ss", "s_cv")}
    for l in range(depth):
        w_main, w_small = _split_w_in(w_in[l])
        lw = dict(w_main=w_main, w_small=w_small, norm_mix=norm_mix[l], fox_gq=fox_gq[l], fox_gk=fox_gk[l],
                  fox_bf=fox_bf[l], hg_lb_logits=hg_lb_logits, hg_gnorm=hg_gnorm[l], conv_w=conv_w[l],
                  conv_b=conv_b[l], dt_bias=dt_bias[l], a_log=a_log[l], d_skip=d_skip[l],
                  ssm_gnorm=ssm_gnorm[l], w_out=w_out[l].astype(BF16), norm_xa=norm_xa[l],
                  xa_wq=xa_wq[l].astype(BF16), xa_gq=xa_gq[l], xa_wo=xa_wo[l].astype(BF16),
                  norm_mlp=norm_mlp[l], w_up=w_up[l].astype(BF16), w_down=w_down[l].astype(BF16))

        xp, fk, fv, fl, hg, ss, cv = _mixers(xp, lw, l, fox_past=None, hg_s0=None, ssm_s0=None,
                                             conv_s0=None, page_table=None)
        w_kv = jnp.concatenate([xa_wk[l], xa_wv[l]], axis=1).astype(BF16)
        kv = _matmul(mem2, w_kv, gain=norm_mem[l], name="mem_kv")
        mk = _headnorm(kv, xa_gk[l], col0=0, width=XA_W).reshape(bp, n_mem, XA_W)
        mv = kv[:, XA_W:].reshape(bp, n_mem, XA_W)
        xp = _cross_and_mlp(xp, lw, mk, mv)
        for key, val in zip(("p_fk", "p_fv", "p_fl", "p_hg", "p_ss", "p_cv"), (fk, fv, fl, hg, ss, cv)):
            outs[key].append(val)
        outs["p_mk"].append(mk.reshape(bp, n_mem, XA_HEADS, HEAD))
        outs["p_mv"].append(mv.reshape(bp, n_mem, XA_HEADS, HEAD))

        xs, fk, fv, fl, hg, ss, cv = _mixers(xs, lw, l, fox_past=(cache_k, cache_v, cache_lf),
                                             hg_s0=state_hgrn[l], ssm_s0=state_ssm[l],
                                             conv_s0=state_conv[l], page_table=page_table)
        bs = xs.shape[0]
        xs = _cross_and_mlp(xs, lw, cache_mem_k[l].reshape(bs, n_mem, XA_W),
                            cache_mem_v[l].reshape(bs, n_mem, XA_W))
        for key, val in zip(("s_fk", "s_fv", "s_fl", "s_hg", "s_ss", "s_cv"), (fk, fv, fl, hg, ss, cv)):
            outs[key].append(val)

    st = {k: jnp.stack(v) for k, v in outs.items()}
    return (xp, xs, st["p_fk"], st["p_fv"], st["p_fl"], st["p_hg"], st["p_ss"], st["p_cv"], st["p_mk"],
            st["p_mv"], st["s_fk"], st["s_fv"], st["s_fl"], st["s_hg"], st["s_ss"], st["s_cv"])
```

```python
import functools

import jax
import jax.numpy as jnp
from jax import lax
from jax.experimental import pallas as pl
from jax.experimental.pallas import tpu as pltpu

F32 = jnp.float32
BF16 = jnp.bfloat16

EPS = 1e-6
MASK_VALUE = -1e30
HEAD = 128
N_HEADS = 8
FOX_W = N_HEADS * HEAD
HG_W = N_HEADS * HEAD
SSM_W = 2048
SSM_P = 64
SSM_HEADS = SSM_W // SSM_P
SSM_GROUPS = 8
SSM_HPG = SSM_HEADS // SSM_GROUPS
SSM_N = 128
SSM_GW = SSM_HPG * SSM_P
CONV_W = 4
CONV_DIM = SSM_W + 2 * SSM_GROUPS * SSM_N
XA_HEADS = 4
XA_W = XA_HEADS * HEAD
HG_CHUNK = 64
HG_SUB = 16
SSD_CHUNK = 128
PAGE = 128
DT_LANE0 = N_HEADS

COL_XBC = 0
COL_HG = CONV_DIM
COL_Z = COL_HG + 4 * HG_W
COL_FQ = COL_Z + SSM_W
COL_FK = COL_FQ + FOX_W
COL_FV = COL_FK + FOX_W
MAIN_W = COL_FV + FOX_W

V7X_VMEM_LIMIT = 56 * 1024 * 1024


def _cparams(sem, vmem=V7X_VMEM_LIMIT):
    return pltpu.CompilerParams(dimension_semantics=sem, vmem_limit_bytes=vmem)


def _sigmoid(x):
    return 1.0 / (1.0 + jnp.exp(-x))


def _silu(x):
    h = 0.5 * x
    return h + h * jnp.tanh(h)


def _softplus(x):
    return jnp.maximum(x, 0.0) + jnp.log1p(jnp.exp(-jnp.abs(x)))


def _log_sigmoid(x):
    return -_softplus(-x)


def _dot_nt(a, b):
    return lax.dot_general(a, b, (((1,), (1,)), ((), ())), preferred_element_type=F32)


def _dot_tn(a, b):
    return lax.dot_general(a, b, (((0,), (0,)), ((), ())), preferred_element_type=F32)


def _dot(a, b):
    return jnp.dot(a, b, preferred_element_type=F32)


def _tri_incl(n):
    r = lax.broadcasted_iota(jnp.int32, (n, n), 0)
    c = lax.broadcasted_iota(jnp.int32, (n, n), 1)
    return r >= c


def _cumsum_rows(x):
    n = x.shape[0]
    tri = _tri_incl(n).astype(BF16)
    hi = x.astype(BF16)
    r1 = x - hi.astype(F32)
    mid = r1.astype(BF16)
    lo = (r1 - mid.astype(F32)).astype(BF16)
    return _dot(tri, hi) + _dot(tri, mid) + _dot(tri, lo)


def _mm_body(*refs, nk, norm, act, has_res, cast_w, precise):
    it = iter(refs)
    a_ref = next(it)
    g_ref = next(it) if norm else None
    w_ref = next(it)
    r_ref = next(it) if has_res else None
    o_ref = next(it)
    wb_ref = next(it) if cast_w else None
    h_ref = next(it) if norm else None
    acc_ref = next(it) if nk > 1 else None
    j = pl.program_id(1)
    k = pl.program_id(2)

    if norm:
        @pl.when(j == 0)
        def _():
            x = a_ref[...].astype(F32)
            ms = jnp.mean(x * x, axis=-1, keepdims=True)
            h_ref[...] = (x * lax.rsqrt(ms + EPS) * g_ref[...]).astype(h_ref.dtype)
        a = h_ref[...]
    else:
        a = a_ref[...]

    if precise:
        p = jnp.dot(a, w_ref[...], preferred_element_type=F32, precision=lax.Precision.HIGHEST)
    elif cast_w:
        wb = w_ref[...].astype(BF16)
        wb_ref[...] = wb
        p = _dot(a, wb)
    else:
        p = _dot(a, w_ref[...])

    def finish(v):
        if act == "relu2":
            v = jnp.square(jnp.maximum(v, 0.0))
        if has_res:
            v = v + r_ref[...]
        o_ref[...] = v.astype(o_ref.dtype)

    if nk == 1:
        finish(p)
    else:
        @pl.when(k == 0)
        def _():
            acc_ref[...] = p

        @pl.when(k > 0)
        def _():
            acc_ref[...] += p

        @pl.when(k == nk - 1)
        def _():
            finish(acc_ref[...])


def _pick_tile(n, candidates):
    for c in candidates:
        if n % c == 0:
            return c
    return n


def _matmul(a, w, *, gain=None, res=None, act=None, out_dtype=F32, tm_max=512, tn_max=None, tk=None,
            cast_w=False, precise=False, name="matmul"):
    m, kdim = a.shape
    n = w.shape[1]
    small_m = m <= 64
    tm = m if small_m else _pick_tile(m, tuple(c for c in (1024, 512, 256, 128) if c <= tm_max))
    tn_max = tn_max or (2048 if small_m else 1024)
    tn = _pick_tile(n, tuple(c for c in (2048, 1024, 512, 256, 128) if c <= tn_max))
    tk = kdim if tk is None else tk
    nk = kdim // tk
    norm = gain is not None
    assert not (norm and nk > 1)
    assert not cast_w or (m == tm and not precise)
    in_specs = [pl.BlockSpec((tm, tk), lambda i, j, k: (i, k))]
    args = [a]
    if norm:
        in_specs.append(pl.BlockSpec((1, kdim), lambda i, j, k: (0, 0)))
        args.append(gain.reshape(1, kdim).astype(F32))
    in_specs.append(pl.BlockSpec((tk, tn), lambda i, j, k: (k, j)))
    args.append(w)
    if res is not None:
        in_specs.append(pl.BlockSpec((tm, tn), lambda i, j, k: (i, j)))
        args.append(res)
    scratch = []
    if norm:
        scratch.append(pltpu.VMEM((tm, kdim), F32 if precise else BF16))
    if nk > 1:
        scratch.append(pltpu.VMEM((tm, tn), F32))
    out_specs = pl.BlockSpec((tm, tn), lambda i, j, k: (i, j))
    out_shape = jax.ShapeDtypeStruct((m, n), out_dtype)
    if cast_w:
        out_specs = [out_specs, pl.BlockSpec((tk, tn), lambda i, j, k: (k, j))]
        out_shape = [out_shape, jax.ShapeDtypeStruct((kdim, n), BF16)]
    return pl.pallas_call(
        functools.partial(_mm_body, nk=nk, norm=norm, act=act, has_res=res is not None, cast_w=cast_w,
                          precise=precise),
        grid=(m // tm, n // tn, nk),
        in_specs=in_specs,
        out_specs=out_specs,
        out_shape=out_shape,
        scratch_shapes=scratch,
        compiler_params=_cparams(("parallel", "arbitrary", "arbitrary")),
        name=name,
    )(*args)


def _outproj_body(fo_ref, ho_ref, sy_ref, w_ref, r_ref, o_ref, *maybe_wb_ref):
    if maybe_wb_ref:
        maybe_wb_ref[0][...] = w_ref[...].astype(BF16)
        w_ref = maybe_wb_ref[0]
    k1 = FOX_W + HG_W
    p = (_dot(fo_ref[...], w_ref[0:FOX_W, :]) + _dot(ho_ref[...], w_ref[FOX_W:k1, :])
         + _dot(sy_ref[...], w_ref[k1:k1 + SSM_W, :]))
    o_ref[...] = p + r_ref[...]


def _out_proj(fo, ho, sy, w, res, *, cast_w=False):
    m = fo.shape[0]
    kdim, n = w.shape
    tm = m if m <= 64 else _pick_tile(m, (1024, 512, 256, 128))
    tn = _pick_tile(n, (1024, 512, 256, 128))
    assert fo.shape[1] == FOX_W and ho.shape[1] == HG_W and sy.shape[1] == SSM_W
    assert not cast_w or m == tm
    out_specs = pl.BlockSpec((tm, tn), lambda i, j: (i, j))
    out_shape = jax.ShapeDtypeStruct((m, n), F32)
    if cast_w:
        out_specs = [out_specs, pl.BlockSpec((kdim, tn), lambda i, j: (0, j))]
        out_shape = [out_shape, jax.ShapeDtypeStruct((kdim, n), BF16)]
    return pl.pallas_call(
        _outproj_body,
        grid=(m // tm, n // tn),
        in_specs=[pl.BlockSpec((tm, FOX_W), lambda i, j: (i, 0)),
                  pl.BlockSpec((tm, HG_W), lambda i, j: (i, 0)),
                  pl.BlockSpec((tm, SSM_W), lambda i, j: (i, 0)),
                  pl.BlockSpec((kdim, tn), lambda i, j: (0, j)),
                  pl.BlockSpec((tm, tn), lambda i, j: (i, j))],
        out_specs=out_specs,
        out_shape=out_shape,
        compiler_params=_cparams(("parallel", "arbitrary")),
        name="out_proj",
    )(fo, ho, sy, w, res)


def _headnorm_body(x_ref, g_ref, o_ref, *, n_heads):
    g = g_ref[...]
    for h in range(n_heads):
        sl = slice(h * HEAD, (h + 1) * HEAD)
        x = x_ref[:, sl].astype(F32)
        ms = jnp.mean(x * x, axis=-1, keepdims=True)
        o_ref[:, sl] = (x * lax.rsqrt(ms + EPS) * g).astype(o_ref.dtype)


def _headnorm(x, gain, *, col0, width, out_dtype=F32):
    m = x.shape[0]
    tm = m if m <= 64 else _pick_tile(m, (512, 256, 128))
    return pl.pallas_call(
        functools.partial(_headnorm_body, n_heads=width // HEAD),
        grid=(m // tm,),
        in_specs=[pl.BlockSpec((tm, width), lambda i: (i, col0 // width)),
                  pl.BlockSpec((1, HEAD), lambda i: (0, 0))],
        out_specs=pl.BlockSpec((tm, width), lambda i: (i, 0)),
        out_shape=jax.ShapeDtypeStruct((m, width), out_dtype),
        compiler_params=_cparams(("parallel",)),
        name="headnorm",
    )(x, gain.reshape(1, HEAD).astype(F32))


def _fox_gate_body(g_ref, bf_ref, lf_ref, cum_ref, *, t):
    lf = _log_sigmoid(g_ref[0] + bf_ref[...])
    lf_ref[0] = lf
    lane = lax.broadcasted_iota(jnp.int32, lf.shape, 1)
    c = lf
    shift = 1
    while shift < t:
        c = c + jnp.where(lane >= shift, pltpu.roll(c, shift, axis=1), 0.0)
        shift *= 2
    cum_ref[0] = c


def _fox_gate(gates_t, fox_bf):
    b, h, t = gates_t.shape
    spec = pl.BlockSpec((1, h, t), lambda i: (i, 0, 0))
    return pl.pallas_call(
        functools.partial(_fox_gate_body, t=t),
        grid=(b,),
        in_specs=[spec, pl.BlockSpec((h, 1), lambda i: (0, 0))],
        out_specs=[spec, spec],
        out_shape=[jax.ShapeDtypeStruct((b, h, t), F32)] * 2,
        compiler_params=_cparams(("parallel",)),
        name="fox_gate",
    )(gates_t, fox_bf.reshape(h, 1).astype(F32))


FLASH_STRIP = 256


def _fox_flash_body(q_ref, k_ref, v_ref, cq_ref, ck_ref, o_ref, m_ref, l_ref, acc_ref, *, tq):
    qi = pl.program_id(1)
    ki = pl.program_id(2)
    scale = HEAD ** -0.5

    @pl.when(ki == 0)
    def _():
        m_ref[...] = jnp.full(m_ref.shape, MASK_VALUE, F32)
        l_ref[...] = jnp.zeros(l_ref.shape, F32)
        acc_ref[...] = jnp.zeros(acc_ref.shape, F32)

    def update(diagonal):
        for r0 in range(0, tq, FLASH_STRIP):
            rows = slice(r0, r0 + FLASH_STRIP)
            if diagonal:
                keep = (r0 + lax.broadcasted_iota(jnp.int32, (FLASH_STRIP, tq), 0)
                        >= lax.broadcasted_iota(jnp.int32, (FLASH_STRIP, tq), 1))
            for h in range(N_HEADS):
                sl = slice(h * HEAD, (h + 1) * HEAD)
                s = _dot_nt(q_ref[0, rows, sl], k_ref[0, :, sl]) * scale
                s = s + cq_ref[0, rows, h:h + 1] - ck_ref[0, h:h + 1, :]
                if diagonal:
                    s = jnp.where(keep, s, MASK_VALUE)
                m_prev = m_ref[h, rows]
                m_new = jnp.maximum(m_prev, jnp.max(s, axis=-1, keepdims=True))
                alpha = jnp.exp(m_prev - m_new)
                p = jnp.exp(s - m_new)
                l_ref[h, rows] = alpha * l_ref[h, rows] + jnp.sum(p, axis=-1, keepdims=True)
                acc_ref[rows, sl] = alpha * acc_ref[rows, sl] + _dot(p.astype(BF16), v_ref[0, :, sl])
                m_ref[h, rows] = m_new

    @pl.when(ki < qi)
    def _():
        update(diagonal=False)

    @pl.when(ki == qi)
    def _():
        update(diagonal=True)
        for h in range(N_HEADS):
            sl = slice(h * HEAD, (h + 1) * HEAD)
            o_ref[0, :, sl] = (acc_ref[:, sl] / l_ref[h]).astype(o_ref.dtype)


def _fox_prep_body(q_ref, k_ref, v_ref, gq_ref, gk_ref, qn_ref, kn_ref, knb_ref, vf_ref, vb_ref):
    for h in range(N_HEADS):
        sl = slice(h * HEAD, (h + 1) * HEAD)
        q = q_ref[:, sl]
        qn_ref[:, sl] = (q * lax.rsqrt(jnp.mean(q * q, axis=-1, keepdims=True) + EPS) * gq_ref[...]).astype(BF16)
        k = k_ref[:, sl]
        kn = k * lax.rsqrt(jnp.mean(k * k, axis=-1, keepdims=True) + EPS) * gk_ref[...]
        kn_ref[:, sl] = kn
        knb_ref[:, sl] = kn.astype(BF16)
    v = v_ref[...]
    vf_ref[...] = v
    vb_ref[...] = v.astype(BF16)


def _fox_prep(proj2, gq, gk):
    m = proj2.shape[0]
    tm = _pick_tile(m, (512, 256, 128))
    col = lambda c: pl.BlockSpec((tm, FOX_W), lambda i: (i, c // FOX_W))
    out = pl.BlockSpec((tm, FOX_W), lambda i: (i, 0))
    gain = pl.BlockSpec((1, HEAD), lambda i: (0, 0))
    return pl.pallas_call(
        _fox_prep_body,
        grid=(m // tm,),
        in_specs=[col(COL_FQ), col(COL_FK), col(COL_FV), gain, gain],
        out_specs=[out] * 5,
        out_shape=[jax.ShapeDtypeStruct((m, FOX_W), dt) for dt in (BF16, F32, BF16, F32, BF16)],
        compiler_params=_cparams(("parallel",)),
        name="fox_prep",
    )(proj2, proj2, proj2, gq.reshape(1, HEAD).astype(F32), gk.reshape(1, HEAD).astype(F32))


def _fox_flash(qn, kn, vb, cum_col, cum_row, *, tq=512):
    b, t, _ = qn.shape
    nq = t // tq
    return pl.pallas_call(
        functools.partial(_fox_flash_body, tq=tq),
        grid=(b, nq, nq),
        in_specs=[
            pl.BlockSpec((1, tq, FOX_W), lambda i, q, k: (i, q, 0)),
            pl.BlockSpec((1, tq, FOX_W), lambda i, q, k: (i, jnp.minimum(k, q), 0)),
            pl.BlockSpec((1, tq, FOX_W), lambda i, q, k: (i, jnp.minimum(k, q), 0)),
            pl.BlockSpec((1, tq, N_HEADS), lambda i, q, k: (i, q, 0)),
            pl.BlockSpec((1, N_HEADS, tq), lambda i, q, k: (i, 0, jnp.minimum(k, q))),
        ],
        out_specs=pl.BlockSpec((1, tq, FOX_W), lambda i, q, k: (i, q, 0)),
        out_shape=jax.ShapeDtypeStruct((b, t, FOX_W), BF16),
        scratch_shapes=[pltpu.VMEM((N_HEADS, tq, 1), F32), pltpu.VMEM((N_HEADS, tq, 1), F32),
                        pltpu.VMEM((tq, FOX_W), F32)],
        compiler_params=_cparams(("parallel", "parallel", "arbitrary")),
        name="fox_flash",
    )(qn, kn, vb, cum_col, cum_row)


def _fox_decode_body(pt_ref, q_ref, kn_ref, vn_ref, lfn_ref, *rest, pages_per_step, n_steps):
    del pt_ref
    pps = pages_per_step
    k_refs = rest[:pps]
    v_refs = rest[pps:2 * pps]
    lf_refs = rest[2 * pps:3 * pps]
    o_ref, m_ref, l_ref, acc_ref, carry_ref = rest[3 * pps:]
    s = pl.program_id(1)
    scale = HEAD ** -0.5
    rows = PAGE * N_HEADS
    lane = lax.broadcasted_iota(jnp.int32, (1, rows), 1)
    own = (lax.broadcasted_iota(jnp.int32, (N_HEADS, rows), 1) % N_HEADS
           == lax.broadcasted_iota(jnp.int32, (N_HEADS, rows), 0))

    @pl.when(s == 0)
    def _():
        m_ref[...] = jnp.sum(q_ref[0] * kn_ref[0], axis=-1, keepdims=True) * scale
        l_ref[...] = jnp.ones(l_ref.shape, F32)
        acc_ref[...] = vn_ref[0]
        carry_ref[...] = jnp.zeros(carry_ref.shape, F32)

    lf = jnp.concatenate([lf_refs[r][0, 0] for r in range(pps)], axis=0)
    tot = lf
    suf = lf
    shift = N_HEADS
    while shift < rows:
        tot = tot + pltpu.roll(tot, shift, axis=1)
        suf = suf + jnp.where(lane + shift < rows, pltpu.roll(suf, rows - shift, axis=1), 0.0)
        shift *= 2
    page = lax.broadcasted_iota(jnp.int32, (pps, 1), 0)
    newer = tot
    shift = 1
    while shift < pps:
        newer = newer + jnp.where(page >= shift, pltpu.roll(newer, shift, axis=0), 0.0)
        shift *= 2
    after = suf - lf + (newer - tot) + carry_ref[...]
    carry_ref[...] = carry_ref[...] + newer[pps - 1:pps, :]

    q = q_ref[0].astype(BF16)
    scs = []
    for r in range(pps):
        sc = _dot_nt(q, k_refs[r][0, 0].astype(BF16)) * scale + lfn_ref[0] + after[r:r + 1, :]
        scs.append(jnp.where(own, sc, MASK_VALUE))
    m_prev = m_ref[...]
    m_new = m_prev
    for sc in scs:
        m_new = jnp.maximum(m_new, jnp.max(sc, axis=-1, keepdims=True))
    alpha = jnp.exp(m_prev - m_new)
    l_new = alpha * l_ref[...]
    acc = alpha * acc_ref[...]
    for r, sc in enumerate(scs):
        p = jnp.exp(sc - m_new)
        l_new = l_new + jnp.sum(p, axis=-1, keepdims=True)
        acc = acc + _dot(p.astype(BF16), v_refs[r][0, 0].astype(BF16))
    l_ref[...] = l_new
    acc_ref[...] = acc
    m_ref[...] = m_new

    @pl.when(s == n_steps - 1)
    def _():
        o_ref[0] = (acc_ref[...] / l_ref[...]).astype(o_ref.dtype)


def _fox_decode(layer, qn, kn, v_new, lf_new, cache_k, cache_v, cache_lf, page_table, *, pages_per_step=8):
    b = qn.shape[0]
    n_pages = page_table.shape[1]
    pps = pages_per_step
    n_steps = n_pages // pps
    rows = PAGE * N_HEADS

    def page_map(r):
        return lambda i, s, pt: (layer, pt[i, n_pages - 1 - (s * pps + r)], 0, 0)

    head = pl.BlockSpec((1, N_HEADS, HEAD), lambda i, s, pt: (i, 0, 0))
    in_specs = [head, head, head, pl.BlockSpec((1, N_HEADS, 1), lambda i, s, pt: (i, 0, 0))]
    in_specs += [pl.BlockSpec((1, 1, rows, HEAD), page_map(r)) for r in range(pps)]
    in_specs += [pl.BlockSpec((1, 1, rows, HEAD), page_map(r)) for r in range(pps)]
    in_specs += [pl.BlockSpec((1, 1, 1, rows), page_map(r)) for r in range(pps)]
    grid_spec = pltpu.PrefetchScalarGridSpec(
        num_scalar_prefetch=1, grid=(b, n_steps), in_specs=in_specs,
        out_specs=pl.BlockSpec((1, N_HEADS, HEAD), lambda i, s, pt: (i, 0, 0)),
        scratch_shapes=[pltpu.VMEM((N_HEADS, 1), F32), pltpu.VMEM((N_HEADS, 1), F32),
                        pltpu.VMEM((N_HEADS, HEAD), F32), pltpu.VMEM((1, rows), F32)])
    return pl.pallas_call(
        functools.partial(_fox_decode_body, pages_per_step=pps, n_steps=n_steps),
        grid_spec=grid_spec,
        out_shape=jax.ShapeDtypeStruct((b, N_HEADS, HEAD), BF16),
        compiler_params=_cparams(("parallel", "arbitrary")),
        name="fox_decode",
    )(page_table, qn, kn, v_new, lf_new, *([cache_k] * pps), *([cache_v] * pps), *([cache_lf] * pps))


def _rows_of_chunk(ref, cols, chunk_rows):
    x = ref[0, :, cols].astype(F32)
    if x.shape[0] == chunk_rows:
        return x
    assert x.shape[0] == 1
    return jnp.broadcast_to(x, (chunk_rows, x.shape[1]))


def _hgrn_body(*refs, layer, t_valid, has_s0):
    it = iter(refs)
    x_ref = next(it)
    lbl_ref = next(it)
    gn_ref = next(it)
    s0_ref = next(it) if has_s0 else None
    o_ref = next(it)
    st_ref = next(it)
    q_s, k_s, v_s, b_s, stt_s = (next(it) for _ in range(5))
    c = pl.program_id(1)
    nc = pl.num_programs(1)
    cs = HG_CHUNK
    t_blk = o_ref.shape[1]

    @pl.when(c == 0)
    def _():
        for h in range(N_HEADS):
            if has_s0:
                stt_s[h] = s0_ref[0, h].astype(F32).T
            else:
                stt_s[h] = jnp.zeros((HEAD, HEAD), F32)

    lg = lbl_ref[...].astype(F32)
    e = jnp.exp(lg - jnp.max(lg, axis=0, keepdims=True))
    pr = e / jnp.sum(e, axis=0, keepdims=True)
    lb = jnp.sum(pr[0:layer + 1], axis=0, keepdims=True) - pr[0:1]

    hf = _rows_of_chunk(x_ref, slice(HG_W, 2 * HG_W), cs)
    logf = jnp.log(lb + (1.0 - lb) * _sigmoid(hf))
    kk = (1.0 - lb) * _sigmoid(-hf)
    qq = _silu(_rows_of_chunk(x_ref, slice(0, HG_W), cs))
    vv = _rows_of_chunk(x_ref, slice(2 * HG_W, 3 * HG_W), cs)
    if t_valid is not None:
        live = (c * cs + lax.broadcasted_iota(jnp.int32, (cs, 1), 0)) < t_valid
        logf = jnp.where(live, logf, 0.0)
        kk = jnp.where(live, kk, 0.0)
        qq = jnp.where(live, qq, 0.0)
        vv = jnp.where(live, vv, 0.0)
    q_s[...] = qq
    k_s[...] = kk
    v_s[...] = vv
    b_s[...] = _cumsum_rows(logf)

    n_sub = cs // HG_SUB
    causal = _tri_incl(cs)
    zeros_sub = jnp.zeros((HG_SUB, HEAD), F32)
    g = gn_ref[...]

    atts = []
    for h in range(N_HEADS):
        sl = slice(h * HEAD, (h + 1) * HEAD)
        q_parts, k_parts = [], []
        for i in range(n_sub):
            r0 = i * HG_SUB
            n_keys = r0 + HG_SUB
            b_i = b_s[r0 - 1:r0, sl] if i > 0 else jnp.zeros((1, HEAD), F32)
            qd = q_s[r0:n_keys, sl] * jnp.exp(b_s[r0:n_keys, sl] - b_i)
            kd = k_s[0:n_keys, sl] * jnp.exp(b_i - b_s[0:n_keys, sl])
            q_parts.append(jnp.concatenate([zeros_sub] * i + [qd] + [zeros_sub] * (n_sub - 1 - i), axis=0))
            k_parts.append(jnp.concatenate([kd] + [zeros_sub] * (n_sub - 1 - i), axis=0))
        q_cat = jnp.concatenate(q_parts, axis=1).astype(BF16)
        k_cat = jnp.concatenate(k_parts, axis=1).astype(BF16)
        atts.append(jnp.where(causal, _dot_nt(q_cat, k_cat), 0.0).astype(BF16))

    for h in range(N_HEADS):
        sl = slice(h * HEAD, (h + 1) * HEAD)
        stt = stt_s[h]
        bh = b_s[:, sl]
        b_end = b_s[cs - 1:cs, sl]
        o = _dot(atts[h], v_s[:, sl].astype(BF16))
        o = o + _dot_nt((q_s[:, sl] * jnp.exp(bh)).astype(BF16), stt.astype(BF16))
        kd_end = k_s[:, sl] * jnp.exp(b_end - bh)
        stt_s[h] = stt * jnp.exp(b_end) + _dot_tn(v_s[:, sl].astype(BF16), kd_end.astype(BF16))
        ms = jnp.mean(o * o, axis=-1, keepdims=True)
        gate = _silu(_rows_of_chunk(x_ref, slice(3 * HG_W + h * HEAD, 3 * HG_W + (h + 1) * HEAD), cs))
        res = o * lax.rsqrt(ms + EPS) * g * gate
        o_ref[0, :, sl] = res[0:t_blk].astype(o_ref.dtype)

    @pl.when(c == nc - 1)
    def _():
        for h in range(N_HEADS):
            st_ref[0, h] = stt_s[h].T


def _hgrn(proj, lb_logits, gnorm, s0, *, layer):
    b, t, _ = proj.shape
    cs = HG_CHUNK
    assert t == 1 or t % cs == 0
    t_blk, n_chunks, t_valid = (1, 1, 1) if t == 1 else (cs, t // cs, None)
    has_s0 = s0 is not None
    depth = lb_logits.shape[0]
    in_specs = [pl.BlockSpec((1, t_blk, 4 * HG_W), lambda i, c: (i, c, COL_HG // (4 * HG_W))),
                pl.BlockSpec((depth, HG_W), lambda i, c: (0, 0)),
                pl.BlockSpec((1, HEAD), lambda i, c: (0, 0))]
    args = [proj, lb_logits.astype(F32), gnorm.reshape(1, HEAD).astype(F32)]
    if has_s0:
        in_specs.append(pl.BlockSpec((1, N_HEADS, HEAD, HEAD), lambda i, c: (i, 0, 0, 0)))
        args.append(s0)
    return pl.pallas_call(
        functools.partial(_hgrn_body, layer=layer, t_valid=t_valid, has_s0=has_s0),
        grid=(b, n_chunks),
        in_specs=in_specs,
        out_specs=[pl.BlockSpec((1, t_blk, HG_W), lambda i, c: (i, c, 0)),
                   pl.BlockSpec((1, N_HEADS, HEAD, HEAD), lambda i, c: (i, 0, 0, 0))],
        out_shape=[jax.ShapeDtypeStruct((b, t, HG_W), BF16),
                   jax.ShapeDtypeStruct((b, N_HEADS, HEAD, HEAD), F32)],
        scratch_shapes=[pltpu.VMEM((cs, HG_W), F32)] * 4 + [pltpu.VMEM((N_HEADS, HEAD, HEAD), F32)],
        compiler_params=_cparams(("parallel", "arbitrary")),
        name="hgrn2",
    )(*args)


def _lane_blocks(cols, width):
    rows = cols[0].shape[0]
    lane = lax.broadcasted_iota(jnp.int32, (rows, width * len(cols)), 1)
    out = jnp.broadcast_to(cols[-1], (rows, width * len(cols)))
    for j in range(len(cols) - 2, -1, -1):
        out = jnp.where(lane < (j + 1) * width, cols[j], out)
    return out


def _ssd_body(*refs, t_valid, has_s0):
    it = iter(refs)
    xbc_ref = next(it)
    z_ref = next(it)
    dts_ref = next(it)
    cw_ref = next(it)
    cb_ref = next(it)
    dtb_ref = next(it)
    alog_ref = next(it)
    dsk_ref = next(it)
    gn_ref = next(it)
    cs0_ref = next(it) if has_s0 else None
    s0_ref = next(it) if has_s0 else None
    y_ref = next(it)
    st_ref = next(it)
    prev_s = next(it)
    act_s = next(it)
    c = pl.program_id(1)
    cs = SSD_CHUNK
    t_blk = y_ref.shape[1]
    tail0 = 8 - (CONV_W - 1)

    @pl.when(c == 0)
    def _():
        prev_s[...] = jnp.zeros(prev_s.shape, F32)
        if has_s0:
            prev_s[tail0:8, :] = cs0_ref[0].astype(F32)
            st_ref[...] = s0_ref[...].astype(F32)
        else:
            st_ref[...] = jnp.zeros(st_ref.shape, F32)

    cur = _rows_of_chunk(xbc_ref, slice(0, CONV_DIM), cs)
    win = jnp.concatenate([prev_s[...], cur[0:8]], axis=0)
    conv = cb_ref[...] + cur * cw_ref[CONV_W - 1:CONV_W, :]
    head = cb_ref[...] + win[8:16] * cw_ref[CONV_W - 1:CONV_W, :]
    for d in range(1, CONV_W):
        w_d = cw_ref[CONV_W - 1 - d:CONV_W - d, :]
        conv = conv + pltpu.roll(cur, d, axis=0) * w_d
        head = head + pltpu.roll(win, d, axis=0)[8:16] * w_d
    act_s[...] = _silu(conv)
    act_s[0:8, :] = _silu(head)
    prev_s[...] = cur[cs - 8:cs]

    dt = _softplus(_rows_of_chunk(dts_ref, slice(0, 128), cs) + dtb_ref[...])
    if t_valid is not None:
        live = (c * cs + lax.broadcasted_iota(jnp.int32, (cs, 1), 0)) < t_valid
        dt = jnp.where(live, dt, 0.0)
    a = _cumsum_rows(dt * (-jnp.exp(alog_ref[...])))
    a_t = a.T
    dt_t = dt.T
    a_end = a[cs - 1:cs, :]
    w_upd = dt * jnp.exp(a_end - a)
    e_a = jnp.exp(a)
    e_end = jnp.exp(a_end)
    causal = _tri_incl(cs)
    lane_g = lax.broadcasted_iota(jnp.int32, (cs, SSM_GW), 1)
    row_g = lax.broadcasted_iota(jnp.int32, (SSM_GW, 1), 0)

    for g in range(SSM_GROUPS):
        xs = slice(g * SSM_GW, (g + 1) * SSM_GW)
        bsl = slice(SSM_W + g * SSM_N, SSM_W + (g + 1) * SSM_N)
        csl = slice(SSM_W + SSM_GROUPS * SSM_N + g * SSM_N, SSM_W + SSM_GROUPS * SSM_N + (g + 1) * SSM_N)
        xg = act_s[:, xs]
        bm = act_s[:, bsl].astype(BF16)
        cm = act_s[:, csl].astype(BF16)
        cbm = jnp.where(causal, _dot_nt(cm, bm), 0.0)
        s_g = st_ref[0, g]
        lanes = [DT_LANE0 + g * SSM_HPG + j for j in range(SSM_HPG)]
        yg = _dot_nt(cm, s_g.astype(BF16)) * _lane_blocks([e_a[:, ln:ln + 1] for ln in lanes], SSM_P)
        for j, ln in enumerate(lanes):
            diff = a[:, ln:ln + 1] - a_t[ln:ln + 1, :]
            mh = cbm * jnp.exp(jnp.minimum(diff, 0.0)) * dt_t[ln:ln + 1, :]
            xm = jnp.where(jnp.logical_and(lane_g >= j * SSM_P, lane_g < (j + 1) * SSM_P), xg, 0.0)
            yg = yg + _dot(mh.astype(BF16), xm.astype(BF16))
        xw = xg * _lane_blocks([w_upd[:, ln:ln + 1] for ln in lanes], SSM_P)
        decay = jnp.broadcast_to(e_end[:, lanes[-1]:lanes[-1] + 1], (SSM_GW, 1))
        for j in range(SSM_HPG - 2, -1, -1):
            decay = jnp.where(row_g < (j + 1) * SSM_P, e_end[:, lanes[j]:lanes[j] + 1], decay)
        st_ref[0, g] = s_g * decay + _dot_tn(xw.astype(BF16), bm)

        y = yg + dsk_ref[:, xs] * xg
        y = y * _silu(_rows_of_chunk(z_ref, xs, cs))
        ms = jnp.mean(y * y, axis=-1, keepdims=True)
        res = y * lax.rsqrt(ms + EPS) * gn_ref[:, xs]
        y_ref[0, :, xs] = res[0:t_blk].astype(y_ref.dtype)


def _ssd(proj, small, conv_w, conv_b, dt_bias, a_log, d_skip, gnorm, conv_s0, s0):
    b, t, _ = proj.shape
    cs = SSD_CHUNK
    assert t == 1 or t % cs == 0
    t_blk, n_chunks, t_valid = (1, 1, 1) if t == 1 else (cs, t // cs, None)
    has_s0 = s0 is not None
    pad = jnp.zeros((DT_LANE0,), F32)
    tail = jnp.zeros((128 - DT_LANE0 - SSM_HEADS,), F32)
    on_dt_lanes = lambda v: jnp.concatenate([pad, v.astype(F32), tail]).reshape(1, 128)
    const = lambda shape: pl.BlockSpec(shape, lambda i, c: (0,) * len(shape))
    in_specs = [pl.BlockSpec((1, t_blk, CONV_DIM), lambda i, c: (i, c, COL_XBC // CONV_DIM)),
                pl.BlockSpec((1, t_blk, SSM_W), lambda i, c: (i, c, COL_Z // SSM_W)),
                pl.BlockSpec((1, t_blk, 128), lambda i, c: (i, c, 0)),
                const((CONV_W, CONV_DIM)), const((1, CONV_DIM)), const((1, 128)), const((1, 128)),
                const((1, SSM_W)), const((1, SSM_W))]
    args = [proj, proj, small, conv_w.astype(F32), conv_b.reshape(1, CONV_DIM).astype(F32),
            on_dt_lanes(dt_bias), on_dt_lanes(a_log),
            jnp.repeat(d_skip.astype(F32), SSM_P).reshape(1, SSM_W), gnorm.reshape(1, SSM_W).astype(F32)]
    if has_s0:
        in_specs += [pl.BlockSpec((1, CONV_W - 1, CONV_DIM), lambda i, c: (i, 0, 0)),
                     pl.BlockSpec((1, SSM_GROUPS, SSM_GW, SSM_N), lambda i, c: (i, 0, 0, 0))]
        args += [conv_s0, s0.reshape(b, SSM_GROUPS, SSM_GW, SSM_N)]
    y, st = pl.pallas_call(
        functools.partial(_ssd_body, t_valid=t_valid, has_s0=has_s0),
        grid=(b, n_chunks),
        in_specs=in_specs,
        out_specs=[pl.BlockSpec((1, t_blk, SSM_W), lambda i, c: (i, c, 0)),
                   pl.BlockSpec((1, SSM_GROUPS, SSM_GW, SSM_N), lambda i, c: (i, 0, 0, 0))],
        out_shape=[jax.ShapeDtypeStruct((b, t, SSM_W), BF16),
                   jax.ShapeDtypeStruct((b, SSM_GROUPS, SSM_GW, SSM_N), F32)],
        scratch_shapes=[pltpu.VMEM((8, CONV_DIM), F32), pltpu.VMEM((cs, CONV_DIM), F32)],
        compiler_params=_cparams(("parallel", "arbitrary")),
        name="ssd",
    )(*args)
    return y, st.reshape(b, SSM_HEADS, SSM_P, SSM_N)


def _xattn_body(q_ref, k_ref, v_ref, gq_ref, o_ref):
    scale = HEAD ** -0.5
    g = gq_ref[...]
    for h in range(XA_HEADS):
        sl = slice(h * HEAD, (h + 1) * HEAD)
        q = q_ref[0, :, sl].astype(F32)
        ms = jnp.mean(q * q, axis=-1, keepdims=True)
        qn = q * lax.rsqrt(ms + EPS) * g
        s = _dot_nt(qn.astype(BF16), k_ref[0, :, sl].astype(BF16)) * scale
        p = jnp.exp(s - jnp.max(s, axis=-1, keepdims=True))
        o = _dot(p.astype(BF16), v_ref[0, :, sl].astype(BF16)) / jnp.sum(p, axis=-1, keepdims=True)
        o_ref[0, :, sl] = o.astype(o_ref.dtype)


def _xattn(q, mk, mv, gq):
    b, t, _ = q.shape
    n_mem = mk.shape[1]
    tq = t if t <= 512 else 512
    return pl.pallas_call(
        _xattn_body,
        grid=(b, t // tq),
        in_specs=[pl.BlockSpec((1, tq, XA_W), lambda i, j: (i, j, 0)),
                  pl.BlockSpec((1, n_mem, XA_W), lambda i, j: (i, 0, 0)),
                  pl.BlockSpec((1, n_mem, XA_W), lambda i, j: (i, 0, 0)),
                  pl.BlockSpec((1, HEAD), lambda i, j: (0, 0))],
        out_specs=pl.BlockSpec((1, tq, XA_W), lambda i, j: (i, j, 0)),
        out_shape=jax.ShapeDtypeStruct((b, t, XA_W), BF16),
        compiler_params=_cparams(("parallel", "parallel")),
        name="xattn",
    )(q, mk, mv, gq.reshape(1, HEAD).astype(F32))


def _split_w_in(w):
    o = 0
    parts = {}
    for name, width in (("fq", FOX_W), ("fk", FOX_W), ("fv", FOX_W), ("fg", N_HEADS), ("hg", 4 * HG_W),
                        ("z", SSM_W), ("xbc", CONV_DIM), ("dt", SSM_HEADS)):
        parts[name] = w[:, o:o + width]
        o += width
    main = jnp.concatenate([parts["xbc"], parts["hg"], parts["z"], parts["fq"], parts["fk"], parts["fv"]], axis=1)
    small = jnp.concatenate([parts["fg"], parts["dt"],
                             jnp.zeros((w.shape[0], 128 - N_HEADS - SSM_HEADS), w.dtype)], axis=1)
    return main.astype(BF16), small, small.astype(BF16)


def _mixers(x, lw, l, *, wb, fox_past, hg_s0, ssm_s0, conv_s0, page_table):
    b, t, d = x.shape
    x2 = x.reshape(b * t, d)
    few_rows = wb is None
    proj2 = _matmul(x2, lw["w_main"], gain=lw["norm_mix"], name="in_proj")
    if few_rows:
        small2 = _matmul(x2, lw["w_small_f32"], gain=lw["norm_mix"], precise=True, name="in_proj_small")
    else:
        small2 = _matmul(x2, lw["w_small"], gain=lw["norm_mix"], name="in_proj_small")
    proj = proj2.reshape(b, t, MAIN_W)
    small = small2.reshape(b, t, 128)

    gates_t = jnp.swapaxes(small[:, :, 0:N_HEADS], 1, 2)
    logf_t, cum_t = _fox_gate(gates_t, lw["fox_bf"])
    logf = jnp.swapaxes(logf_t, 1, 2)

    if fox_past is None:
        qn, kn, knb, fv, vb = _fox_prep(proj2, lw["fox_gq"], lw["fox_gk"])
        fo = _fox_flash(qn.reshape(b, t, FOX_W), knb.reshape(b, t, FOX_W), vb.reshape(b, t, FOX_W),
                        jnp.swapaxes(cum_t, 1, 2), cum_t)
    else:
        qn = _headnorm(proj2, lw["fox_gq"], col0=COL_FQ, width=FOX_W)
        kn = _headnorm(proj2, lw["fox_gk"], col0=COL_FK, width=FOX_W)
        fv = proj2[:, COL_FV:COL_FV + FOX_W]
        cache_k, cache_v, cache_lf = fox_past
        fo = _fox_decode(l, qn.reshape(b, N_HEADS, HEAD), kn.reshape(b, N_HEADS, HEAD),
                         fv.reshape(b, N_HEADS, HEAD), logf_t, cache_k, cache_v, cache_lf, page_table)
        fo = fo.reshape(b, t, FOX_W)

    ho, hg_state = _hgrn(proj, lw["hg_lb_logits"], lw["hg_gnorm"], hg_s0, layer=l)
    sy, ssm_state = _ssd(proj, small, lw["conv_w"], lw["conv_b"], lw["dt_bias"], lw["a_log"],
                         lw["d_skip"], lw["ssm_gnorm"], conv_s0, ssm_s0)
    mixed = (fo.reshape(b * t, FOX_W), ho.reshape(b * t, HG_W), sy.reshape(b * t, SSM_W))
    if few_rows:
        x_new, wb_out = _out_proj(*mixed, lw["w_out"], x2, cast_w=True)
    else:
        x_new, wb_out = _out_proj(*mixed, wb["w_out"], x2), None
    x_new = x_new.reshape(b, t, d)

    keep = CONV_W - 1
    if t >= keep:
        conv_state = proj[:, t - keep:, COL_XBC:COL_XBC + CONV_DIM]
    else:
        prev = jnp.zeros((b, keep, CONV_DIM), F32) if conv_s0 is None else conv_s0.astype(F32)
        conv_state = jnp.concatenate([prev[:, t:], proj[:, :, COL_XBC:COL_XBC + CONV_DIM]], axis=1)
    fk = kn.reshape(b, t, N_HEADS, HEAD)
    return x_new, fk, fv.reshape(b, t, N_HEADS, HEAD), logf, hg_state, ssm_state, conv_state, wb_out


def _cross_and_mlp(x, lw, mk, mv, *, wb):
    b, t, d = x.shape
    x2 = x.reshape(b * t, d)
    q = _matmul(x2, lw["xa_wq"], gain=lw["norm_xa"], name="xa_q")
    o = _xattn(q.reshape(b, t, XA_W), mk, mv, lw["xa_gq"])
    x2 = _matmul(o.reshape(b * t, XA_W), lw["xa_wo"], res=x2, tm_max=1024, name="xa_out")
    if wb is None:
        u, wb_up = _matmul(x2, lw["w_up"], gain=lw["norm_mlp"], act="relu2", out_dtype=BF16, tn_max=1024,
                           cast_w=True, name="mlp_up")
        x2, wb_down = _matmul(u, lw["w_down"], res=x2, tk=2048, cast_w=True, name="mlp_down")
        made = dict(w_up=wb_up, w_down=wb_down)
    else:
        u = _matmul(x2, wb["w_up"], gain=lw["norm_mlp"], act="relu2", out_dtype=BF16, name="mlp_up")
        x2 = _matmul(u, wb["w_down"], res=x2, tm_max=1024, tk=2048, name="mlp_down")
        made = None
    return x2.reshape(b, t, d), made


def kernel(x_prompt, x_sample, cache_fox_k, cache_fox_v, cache_fox_logf, cache_mem_k, cache_mem_v, state_hgrn, state_ssm, state_conv, page_table, mem_prompt, norm_mix, w_in, fox_gq, fox_gk, fox_bf, hg_lb_logits, hg_gnorm, conv_w, conv_b, dt_bias, a_log, d_skip, ssm_gnorm, w_out, norm_xa, norm_mem, xa_wq, xa_wk, xa_wv, xa_gq, xa_gk, xa_wo, norm_mlp, w_up, w_down):
    depth = w_in.shape[0]
    bp = x_prompt.shape[0]
    n_mem = mem_prompt.shape[1]
    pool = cache_fox_k.shape[1]
    cache_k = cache_fox_k.reshape(depth, pool, PAGE * N_HEADS, HEAD)
    cache_v = cache_fox_v.reshape(depth, pool, PAGE * N_HEADS, HEAD)
    cache_lf = cache_fox_logf.reshape(depth, pool, 1, PAGE * N_HEADS)
    mem2 = mem_prompt.reshape(bp * n_mem, -1)

    xp, xs = x_prompt, x_sample
    outs = {k: [] for k in ("p_fk", "p_fv", "p_fl", "p_hg", "p_ss", "p_cv", "p_mk", "p_mv",
                            "s_fk", "s_fv", "s_fl", "s_hg", "s_ss", "s_cv")}
    for l in range(depth):
        w_main, w_small_f32, w_small = _split_w_in(w_in[l])
        lw = dict(w_main=w_main, w_small=w_small, w_small_f32=w_small_f32, norm_mix=norm_mix[l],
                  fox_gq=fox_gq[l], fox_gk=fox_gk[l], fox_bf=fox_bf[l], hg_lb_logits=hg_lb_logits,
                  hg_gnorm=hg_gnorm[l], conv_w=conv_w[l], conv_b=conv_b[l], dt_bias=dt_bias[l], a_log=a_log[l],
                  d_skip=d_skip[l], ssm_gnorm=ssm_gnorm[l], w_out=w_out[l], norm_xa=norm_xa[l],
                  xa_wq=xa_wq[l].astype(BF16), xa_gq=xa_gq[l], xa_wo=xa_wo[l].astype(BF16),
                  norm_mlp=norm_mlp[l], w_up=w_up[l], w_down=w_down[l])

        bs = xs.shape[0]
        xs, fk, fv, fl, hg, ss, cv, wb_out = _mixers(xs, lw, l, wb=None, fox_past=(cache_k, cache_v, cache_lf),
                                                     hg_s0=state_hgrn[l], ssm_s0=state_ssm[l],
                                                     conv_s0=state_conv[l], page_table=page_table)
        xs, wb = _cross_and_mlp(xs, lw, cache_mem_k[l].reshape(bs, n_mem, XA_W),
                                cache_mem_v[l].reshape(bs, n_mem, XA_W), wb=None)
        wb["w_out"] = wb_out
        for key, val in zip(("s_fk", "s_fv", "s_fl", "s_hg", "s_ss", "s_cv"), (fk, fv, fl, hg, ss, cv)):
            outs[key].append(val)

        xp, fk, fv, fl, hg, ss, cv, _ = _mixers(xp, lw, l, wb=wb, fox_past=None, hg_s0=None, ssm_s0=None,
                                                conv_s0=None, page_table=None)
        w_kv = jnp.concatenate([xa_wk[l], xa_wv[l]], axis=1).astype(BF16)
        kv = _matmul(mem2, w_kv, gain=norm_mem[l], name="mem_kv")
        mk = _headnorm(kv, xa_gk[l], col0=0, width=XA_W).reshape(bp, n_mem, XA_W)
        mv = kv[:, XA_W:].reshape(bp, n_mem, XA_W)
        xp, _ = _cross_and_mlp(xp, lw, mk, mv, wb=wb)
        for key, val in zip(("p_fk", "p_fv", "p_fl", "p_hg", "p_ss", "p_cv"), (fk, fv, fl, hg, ss, cv)):
            outs[key].append(val)
        outs["p_mk"].append(mk.reshape(bp, n_mem, XA_HEADS, HEAD))
        outs["p_mv"].append(mv.reshape(bp, n_mem, XA_HEADS, HEAD))

    st = {k: jnp.stack(v) for k, v in outs.items()}
    return (xp, xs, st["p_fk"], st["p_fv"], st["p_fl"], st["p_hg"], st["p_ss"], st["p_cv"], st["p_mk"],
            st["p_mv"], st["s_fk"], st["s_fv"], st["s_fl"], st["s_hg"], st["s_ss"], st["s_cv"])
```

```python
import functools

import jax
import jax.numpy as jnp
from jax import lax
from jax.experimental import pallas as pl
from jax.experimental.pallas import tpu as pltpu

F32 = jnp.float32
BF16 = jnp.bfloat16

EPS = 1e-6
MASK_VALUE = -1e30
HEAD = 128
N_HEADS = 8
FOX_W = N_HEADS * HEAD
HG_W = N_HEADS * HEAD
SSM_W = 2048
SSM_P = 64
SSM_HEADS = SSM_W // SSM_P
SSM_GROUPS = 8
SSM_HPG = SSM_HEADS // SSM_GROUPS
SSM_N = 128
SSM_GW = SSM_HPG * SSM_P
CONV_W = 4
CONV_DIM = SSM_W + 2 * SSM_GROUPS * SSM_N
XA_HEADS = 4
XA_W = XA_HEADS * HEAD
HG_CHUNK = 64
HG_SUB = 16
SSD_CHUNK = 128
PAGE = 128
DT_LANE0 = N_HEADS

COL_XBC = 0
COL_HG = CONV_DIM
COL_Z = COL_HG + 4 * HG_W
COL_FQ = COL_Z + SSM_W
COL_FK = COL_FQ + FOX_W
COL_FV = COL_FK + FOX_W
MAIN_W = COL_FV + FOX_W

V7X_VMEM_LIMIT = 56 * 1024 * 1024


def _cparams(sem, vmem=V7X_VMEM_LIMIT):
    return pltpu.CompilerParams(dimension_semantics=sem, vmem_limit_bytes=vmem)


def _sigmoid(x):
    return 1.0 / (1.0 + jnp.exp(-x))


def _silu(x):
    h = 0.5 * x
    return h + h * jnp.tanh(h)


def _softplus(x):
    return jnp.maximum(x, 0.0) + jnp.log1p(jnp.exp(-jnp.abs(x)))


def _log_sigmoid(x):
    return -_softplus(-x)


def _dot_nt(a, b):
    return lax.dot_general(a, b, (((1,), (1,)), ((), ())), preferred_element_type=F32)


def _dot_tn(a, b):
    return lax.dot_general(a, b, (((0,), (0,)), ((), ())), preferred_element_type=F32)


def _dot(a, b):
    return jnp.dot(a, b, preferred_element_type=F32)


def _tri_incl(n):
    r = lax.broadcasted_iota(jnp.int32, (n, n), 0)
    c = lax.broadcasted_iota(jnp.int32, (n, n), 1)
    return r >= c


def _cumsum_rows(x):
    n = x.shape[0]
    tri = _tri_incl(n).astype(BF16)
    hi = x.astype(BF16)
    r1 = x - hi.astype(F32)
    mid = r1.astype(BF16)
    lo = (r1 - mid.astype(F32)).astype(BF16)
    return _dot(tri, hi) + _dot(tri, mid) + _dot(tri, lo)


def _mm_body(*refs, nk, norm, act, has_res, cast_w):
    it = iter(refs)
    a_ref = next(it)
    g_ref = next(it) if norm else None
    w_ref = next(it)
    r_ref = next(it) if has_res else None
    o_ref = next(it)
    wb_ref = next(it) if cast_w else None
    h_ref = next(it) if norm else None
    acc_ref = next(it) if nk > 1 else None
    j = pl.program_id(1)
    k = pl.program_id(2)

    if norm:
        @pl.when(j == 0)
        def _():
            x = a_ref[...].astype(F32)
            ms = jnp.mean(x * x, axis=-1, keepdims=True)
            h_ref[...] = (x * lax.rsqrt(ms + EPS) * g_ref[...]).astype(BF16)
        a = h_ref[...]
    else:
        a = a_ref[...]

    if cast_w:
        wb = w_ref[...].astype(BF16)
        wb_ref[...] = wb
        p = _dot(a, wb)
    else:
        p = _dot(a, w_ref[...])

    def finish(v):
        if act == "relu2":
            v = jnp.square(jnp.maximum(v, 0.0))
        if has_res:
            v = v + r_ref[...]
        o_ref[...] = v.astype(o_ref.dtype)

    if nk == 1:
        finish(p)
    else:
        @pl.when(k == 0)
        def _():
            acc_ref[...] = p

        @pl.when(k > 0)
        def _():
            acc_ref[...] += p

        @pl.when(k == nk - 1)
        def _():
            finish(acc_ref[...])


def _pick_tile(n, candidates):
    for c in candidates:
        if n % c == 0:
            return c
    return n


def _matmul(a, w, *, gain=None, res=None, act=None, out_dtype=F32, tm_max=512, tn_max=None, tk=None,
            cast_w=False, layer=None, name="matmul"):
    m, kdim = a.shape
    n = w.shape[-1]
    small_m = m <= 64
    tm = m if small_m else _pick_tile(m, tuple(c for c in (1024, 512, 256, 128) if c <= tm_max))
    tn_max = tn_max or (2048 if small_m else 1024)
    tn = _pick_tile(n, tuple(c for c in (2048, 1024, 512, 256, 128) if c <= tn_max))
    tk = kdim if tk is None else tk
    nk = kdim // tk
    norm = gain is not None
    assert not (norm and nk > 1)
    assert not cast_w or m == tm
    in_specs = [pl.BlockSpec((tm, tk), lambda i, j, k: (i, k))]
    args = [a]
    if norm:
        in_specs.append(pl.BlockSpec((1, kdim), lambda i, j, k: (0, 0)))
        args.append(gain.reshape(1, kdim).astype(F32))
    if layer is None:
        in_specs.append(pl.BlockSpec((tk, tn), lambda i, j, k: (k, j)))
    else:
        in_specs.append(pl.BlockSpec((None, tk, tn), lambda i, j, k: (layer, k, j)))
    args.append(w)
    if res is not None:
        in_specs.append(pl.BlockSpec((tm, tn), lambda i, j, k: (i, j)))
        args.append(res)
    scratch = []
    if norm:
        scratch.append(pltpu.VMEM((tm, kdim), BF16))
    if nk > 1:
        scratch.append(pltpu.VMEM((tm, tn), F32))
    out_specs = pl.BlockSpec((tm, tn), lambda i, j, k: (i, j))
    out_shape = jax.ShapeDtypeStruct((m, n), out_dtype)
    if cast_w:
        out_specs = [out_specs, pl.BlockSpec((tk, tn), lambda i, j, k: (k, j))]
        out_shape = [out_shape, jax.ShapeDtypeStruct((kdim, n), BF16)]
    return pl.pallas_call(
        functools.partial(_mm_body, nk=nk, norm=norm, act=act, has_res=res is not None, cast_w=cast_w),
        grid=(m // tm, n // tn, nk),
        in_specs=in_specs,
        out_specs=out_specs,
        out_shape=out_shape,
        scratch_shapes=scratch,
        compiler_params=_cparams(("parallel", "arbitrary", "arbitrary")),
        name=name,
    )(*args)


def _outproj_body(fo_ref, ho_ref, sy_ref, w_ref, r_ref, o_ref, *maybe_wb_ref):
    if maybe_wb_ref:
        maybe_wb_ref[0][...] = w_ref[...].astype(BF16)
        w_ref = maybe_wb_ref[0]
    k1 = FOX_W + HG_W
    p = (_dot(fo_ref[...], w_ref[0:FOX_W, :]) + _dot(ho_ref[...], w_ref[FOX_W:k1, :])
         + _dot(sy_ref[...], w_ref[k1:k1 + SSM_W, :]))
    o_ref[...] = p + r_ref[...]


def _out_proj(fo, ho, sy, w, res, *, cast_w=False, layer=None):
    m = fo.shape[0]
    kdim, n = w.shape[-2:]
    tm = m if m <= 64 else _pick_tile(m, (1024, 512, 256, 128))
    tn = _pick_tile(n, (1024, 512, 256, 128))
    assert fo.shape[1] == FOX_W and ho.shape[1] == HG_W and sy.shape[1] == SSM_W
    assert not cast_w or m == tm
    out_specs = pl.BlockSpec((tm, tn), lambda i, j: (i, j))
    out_shape = jax.ShapeDtypeStruct((m, n), F32)
    if cast_w:
        out_specs = [out_specs, pl.BlockSpec((kdim, tn), lambda i, j: (0, j))]
        out_shape = [out_shape, jax.ShapeDtypeStruct((kdim, n), BF16)]
    return pl.pallas_call(
        _outproj_body,
        grid=(m // tm, n // tn),
        in_specs=[pl.BlockSpec((tm, FOX_W), lambda i, j: (i, 0)),
                  pl.BlockSpec((tm, HG_W), lambda i, j: (i, 0)),
                  pl.BlockSpec((tm, SSM_W), lambda i, j: (i, 0)),
                  pl.BlockSpec((kdim, tn), lambda i, j: (0, j)) if layer is None
                  else pl.BlockSpec((None, kdim, tn), lambda i, j: (layer, 0, j)),
                  pl.BlockSpec((tm, tn), lambda i, j: (i, j))],
        out_specs=out_specs,
        out_shape=out_shape,
        compiler_params=_cparams(("parallel", "arbitrary")),
        name="out_proj",
    )(fo, ho, sy, w, res)


def _headnorm_body(x_ref, g_ref, o_ref, *, n_heads):
    g = g_ref[...]
    for h in range(n_heads):
        sl = slice(h * HEAD, (h + 1) * HEAD)
        x = x_ref[:, sl].astype(F32)
        ms = jnp.mean(x * x, axis=-1, keepdims=True)
        o_ref[:, sl] = (x * lax.rsqrt(ms + EPS) * g).astype(o_ref.dtype)


def _headnorm(x, gain, *, col0, width, out_dtype=F32):
    m = x.shape[0]
    tm = m if m <= 64 else _pick_tile(m, (512, 256, 128))
    return pl.pallas_call(
        functools.partial(_headnorm_body, n_heads=width // HEAD),
        grid=(m // tm,),
        in_specs=[pl.BlockSpec((tm, width), lambda i: (i, col0 // width)),
                  pl.BlockSpec((1, HEAD), lambda i: (0, 0))],
        out_specs=pl.BlockSpec((tm, width), lambda i: (i, 0)),
        out_shape=jax.ShapeDtypeStruct((m, width), out_dtype),
        compiler_params=_cparams(("parallel",)),
        name="headnorm",
    )(x, gain.reshape(1, HEAD).astype(F32))


def _fox_gate_body(g_ref, bf_ref, lf_ref, cum_ref, *, t):
    lf = _log_sigmoid(g_ref[0] + bf_ref[...])
    lf_ref[0] = lf
    lane = lax.broadcasted_iota(jnp.int32, lf.shape, 1)
    c = lf
    shift = 1
    while shift < t:
        c = c + jnp.where(lane >= shift, pltpu.roll(c, shift, axis=1), 0.0)
        shift *= 2
    cum_ref[0] = c


def _fox_gate(gates_t, fox_bf):
    b, h, t = gates_t.shape
    spec = pl.BlockSpec((1, h, t), lambda i: (i, 0, 0))
    return pl.pallas_call(
        functools.partial(_fox_gate_body, t=t),
        grid=(b,),
        in_specs=[spec, pl.BlockSpec((h, 1), lambda i: (0, 0))],
        out_specs=[spec, spec],
        out_shape=[jax.ShapeDtypeStruct((b, h, t), F32)] * 2,
        compiler_params=_cparams(("parallel",)),
        name="fox_gate",
    )(gates_t, fox_bf.reshape(h, 1).astype(F32))


FLASH_STRIP = 256


def _fox_flash_body(q_ref, k_ref, v_ref, cq_ref, ck_ref, o_ref, m_ref, l_ref, acc_ref, *, tq):
    qi = pl.program_id(1)
    ki = pl.program_id(2)
    scale = HEAD ** -0.5

    @pl.when(ki == 0)
    def _():
        m_ref[...] = jnp.full(m_ref.shape, MASK_VALUE, F32)
        l_ref[...] = jnp.zeros(l_ref.shape, F32)
        acc_ref[...] = jnp.zeros(acc_ref.shape, F32)

    def update(diagonal):
        for r0 in range(0, tq, FLASH_STRIP):
            rows = slice(r0, r0 + FLASH_STRIP)
            if diagonal:
                keep = (r0 + lax.broadcasted_iota(jnp.int32, (FLASH_STRIP, tq), 0)
                        >= lax.broadcasted_iota(jnp.int32, (FLASH_STRIP, tq), 1))
            for h in range(N_HEADS):
                sl = slice(h * HEAD, (h + 1) * HEAD)
                s = _dot_nt(q_ref[0, rows, sl], k_ref[0, :, sl]) * scale
                s = s + cq_ref[0, rows, h:h + 1] - ck_ref[0, h:h + 1, :]
                if diagonal:
                    s = jnp.where(keep, s, MASK_VALUE)
                m_prev = m_ref[h, rows]
                m_new = jnp.maximum(m_prev, jnp.max(s, axis=-1, keepdims=True))
                alpha = jnp.exp(m_prev - m_new)
                p = jnp.exp(s - m_new)
                l_ref[h, rows] = alpha * l_ref[h, rows] + jnp.sum(p, axis=-1, keepdims=True)
                acc_ref[rows, sl] = alpha * acc_ref[rows, sl] + _dot(p.astype(BF16), v_ref[0, :, sl])
                m_ref[h, rows] = m_new

    @pl.when(ki < qi)
    def _():
        update(diagonal=False)

    @pl.when(ki == qi)
    def _():
        update(diagonal=True)
        for h in range(N_HEADS):
            sl = slice(h * HEAD, (h + 1) * HEAD)
            o_ref[0, :, sl] = (acc_ref[:, sl] / l_ref[h]).astype(o_ref.dtype)


def _fox_prep_body(q_ref, k_ref, v_ref, gq_ref, gk_ref, qn_ref, kn_ref, knb_ref, vf_ref, vb_ref):
    for h in range(N_HEADS):
        sl = slice(h * HEAD, (h + 1) * HEAD)
        q = q_ref[:, sl]
        qn_ref[:, sl] = (q * lax.rsqrt(jnp.mean(q * q, axis=-1, keepdims=True) + EPS) * gq_ref[...]).astype(BF16)
        k = k_ref[:, sl]
        kn = k * lax.rsqrt(jnp.mean(k * k, axis=-1, keepdims=True) + EPS) * gk_ref[...]
        kn_ref[:, sl] = kn
        knb_ref[:, sl] = kn.astype(BF16)
    v = v_ref[...]
    vf_ref[...] = v
    vb_ref[...] = v.astype(BF16)


def _fox_prep(proj2, gq, gk):
    m = proj2.shape[0]
    tm = _pick_tile(m, (512, 256, 128))
    col = lambda c: pl.BlockSpec((tm, FOX_W), lambda i: (i, c // FOX_W))
    out = pl.BlockSpec((tm, FOX_W), lambda i: (i, 0))
    gain = pl.BlockSpec((1, HEAD), lambda i: (0, 0))
    return pl.pallas_call(
        _fox_prep_body,
        grid=(m // tm,),
        in_specs=[col(COL_FQ), col(COL_FK), col(COL_FV), gain, gain],
        out_specs=[out] * 5,
        out_shape=[jax.ShapeDtypeStruct((m, FOX_W), dt) for dt in (BF16, F32, BF16, F32, BF16)],
        compiler_params=_cparams(("parallel",)),
        name="fox_prep",
    )(proj2, proj2, proj2, gq.reshape(1, HEAD).astype(F32), gk.reshape(1, HEAD).astype(F32))


def _fox_flash(qn, kn, vb, cum_col, cum_row, *, tq=512):
    b, t, _ = qn.shape
    nq = t // tq
    return pl.pallas_call(
        functools.partial(_fox_flash_body, tq=tq),
        grid=(b, nq, nq),
        in_specs=[
            pl.BlockSpec((1, tq, FOX_W), lambda i, q, k: (i, q, 0)),
            pl.BlockSpec((1, tq, FOX_W), lambda i, q, k: (i, jnp.minimum(k, q), 0)),
            pl.BlockSpec((1, tq, FOX_W), lambda i, q, k: (i, jnp.minimum(k, q), 0)),
            pl.BlockSpec((1, tq, N_HEADS), lambda i, q, k: (i, q, 0)),
            pl.BlockSpec((1, N_HEADS, tq), lambda i, q, k: (i, 0, jnp.minimum(k, q))),
        ],
        out_specs=pl.BlockSpec((1, tq, FOX_W), lambda i, q, k: (i, q, 0)),
        out_shape=jax.ShapeDtypeStruct((b, t, FOX_W), BF16),
        scratch_shapes=[pltpu.VMEM((N_HEADS, tq, 1), F32), pltpu.VMEM((N_HEADS, tq, 1), F32),
                        pltpu.VMEM((tq, FOX_W), F32)],
        compiler_params=_cparams(("parallel", "parallel", "arbitrary")),
        name="fox_flash",
    )(qn, kn, vb, cum_col, cum_row)


def _fox_decode_body(pt_ref, q_ref, kn_ref, vn_ref, lfn_ref, *rest, pages_per_step, n_steps):
    del pt_ref
    pps = pages_per_step
    k_refs = rest[:pps]
    v_refs = rest[pps:2 * pps]
    lf_refs = rest[2 * pps:3 * pps]
    o_ref, m_ref, l_ref, acc_ref, carry_ref = rest[3 * pps:]
    s = pl.program_id(1)
    scale = HEAD ** -0.5
    rows = PAGE * N_HEADS
    lane = lax.broadcasted_iota(jnp.int32, (1, rows), 1)
    own = (lax.broadcasted_iota(jnp.int32, (N_HEADS, rows), 1) % N_HEADS
           == lax.broadcasted_iota(jnp.int32, (N_HEADS, rows), 0))

    @pl.when(s == 0)
    def _():
        m_ref[...] = jnp.sum(q_ref[0] * kn_ref[0], axis=-1, keepdims=True) * scale
        l_ref[...] = jnp.ones(l_ref.shape, F32)
        acc_ref[...] = vn_ref[0]
        carry_ref[...] = jnp.zeros(carry_ref.shape, F32)

    lf = jnp.concatenate([lf_refs[r][0, 0] for r in range(pps)], axis=0)
    tot = lf
    suf = lf
    shift = N_HEADS
    while shift < rows:
        tot = tot + pltpu.roll(tot, shift, axis=1)
        suf = suf + jnp.where(lane + shift < rows, pltpu.roll(suf, rows - shift, axis=1), 0.0)
        shift *= 2
    page = lax.broadcasted_iota(jnp.int32, (pps, 1), 0)
    newer = tot
    shift = 1
    while shift < pps:
        newer = newer + jnp.where(page >= shift, pltpu.roll(newer, shift, axis=0), 0.0)
        shift *= 2
    after = suf - lf + (newer - tot) + carry_ref[...]
    carry_ref[...] = carry_ref[...] + newer[pps - 1:pps, :]

    q = q_ref[0].astype(BF16)
    scs = []
    for r in range(pps):
        sc = _dot_nt(q, k_refs[r][0, 0].astype(BF16)) * scale + lfn_ref[0] + after[r:r + 1, :]
        scs.append(jnp.where(own, sc, MASK_VALUE))
    m_prev = m_ref[...]
    m_new = m_prev
    for sc in scs:
        m_new = jnp.maximum(m_new, jnp.max(sc, axis=-1, keepdims=True))
    alpha = jnp.exp(m_prev - m_new)
    l_new = alpha * l_ref[...]
    acc = alpha * acc_ref[...]
    for r, sc in enumerate(scs):
        p = jnp.exp(sc - m_new)
        l_new = l_new + jnp.sum(p, axis=-1, keepdims=True)
        acc = acc + _dot(p.astype(BF16), v_refs[r][0, 0].astype(BF16))
    l_ref[...] = l_new
    acc_ref[...] = acc
    m_ref[...] = m_new

    @pl.when(s == n_steps - 1)
    def _():
        o_ref[0] = (acc_ref[...] / l_ref[...]).astype(o_ref.dtype)


def _fox_decode(layer, qn, kn, v_new, lf_new, cache_k, cache_v, cache_lf, page_table, *, pages_per_step=8):
    b = qn.shape[0]
    n_pages = page_table.shape[1]
    pps = pages_per_step
    n_steps = n_pages // pps
    rows = PAGE * N_HEADS

    def page_map(r):
        return lambda i, s, pt: (layer, pt[i, n_pages - 1 - (s * pps + r)], 0, 0)

    head = pl.BlockSpec((1, N_HEADS, HEAD), lambda i, s, pt: (i, 0, 0))
    in_specs = [head, head, head, pl.BlockSpec((1, N_HEADS, 1), lambda i, s, pt: (i, 0, 0))]
    in_specs += [pl.BlockSpec((1, 1, rows, HEAD), page_map(r)) for r in range(pps)]
    in_specs += [pl.BlockSpec((1, 1, rows, HEAD), page_map(r)) for r in range(pps)]
    in_specs += [pl.BlockSpec((1, 1, 1, rows), page_map(r)) for r in range(pps)]
    grid_spec = pltpu.PrefetchScalarGridSpec(
        num_scalar_prefetch=1, grid=(b, n_steps), in_specs=in_specs,
        out_specs=pl.BlockSpec((1, N_HEADS, HEAD), lambda i, s, pt: (i, 0, 0)),
        scratch_shapes=[pltpu.VMEM((N_HEADS, 1), F32), pltpu.VMEM((N_HEADS, 1), F32),
                        pltpu.VMEM((N_HEADS, HEAD), F32), pltpu.VMEM((1, rows), F32)])
    return pl.pallas_call(
        functools.partial(_fox_decode_body, pages_per_step=pps, n_steps=n_steps),
        grid_spec=grid_spec,
        out_shape=jax.ShapeDtypeStruct((b, N_HEADS, HEAD), BF16),
        compiler_params=_cparams(("parallel", "arbitrary")),
        name="fox_decode",
    )(page_table, qn, kn, v_new, lf_new, *([cache_k] * pps), *([cache_v] * pps), *([cache_lf] * pps))


def _rows_of_chunk(ref, cols, chunk_rows):
    x = ref[0, :, cols].astype(F32)
    if x.shape[0] == chunk_rows:
        return x
    assert x.shape[0] == 1
    return jnp.broadcast_to(x, (chunk_rows, x.shape[1]))


def _hgrn_body(*refs, layer, t_valid, has_s0):
    it = iter(refs)
    x_ref = next(it)
    lbl_ref = next(it)
    gn_ref = next(it)
    s0_ref = next(it) if has_s0 else None
    o_ref = next(it)
    st_ref = next(it)
    q_s, k_s, v_s, b_s, stt_s = (next(it) for _ in range(5))
    c = pl.program_id(1)
    nc = pl.num_programs(1)
    cs = HG_CHUNK
    t_blk = o_ref.shape[1]

    @pl.when(c == 0)
    def _():
        for h in range(N_HEADS):
            if has_s0:
                stt_s[h] = s0_ref[0, h].astype(F32).T
            else:
                stt_s[h] = jnp.zeros((HEAD, HEAD), F32)

    lg = lbl_ref[...].astype(F32)
    e = jnp.exp(lg - jnp.max(lg, axis=0, keepdims=True))
    pr = e / jnp.sum(e, axis=0, keepdims=True)
    lb = jnp.sum(pr[0:layer + 1], axis=0, keepdims=True) - pr[0:1]

    hf = _rows_of_chunk(x_ref, slice(HG_W, 2 * HG_W), cs)
    logf = jnp.log(lb + (1.0 - lb) * _sigmoid(hf))
    kk = (1.0 - lb) * _sigmoid(-hf)
    qq = _silu(_rows_of_chunk(x_ref, slice(0, HG_W), cs))
    vv = _rows_of_chunk(x_ref, slice(2 * HG_W, 3 * HG_W), cs)
    if t_valid is not None:
        live = (c * cs + lax.broadcasted_iota(jnp.int32, (cs, 1), 0)) < t_valid
        logf = jnp.where(live, logf, 0.0)
        kk = jnp.where(live, kk, 0.0)
        qq = jnp.where(live, qq, 0.0)
        vv = jnp.where(live, vv, 0.0)
    q_s[...] = qq
    k_s[...] = kk
    v_s[...] = vv
    b_s[...] = _cumsum_rows(logf)

    n_sub = cs // HG_SUB
    causal = _tri_incl(cs)
    zeros_sub = jnp.zeros((HG_SUB, HEAD), F32)
    g = gn_ref[...]

    atts = []
    for h in range(N_HEADS):
        sl = slice(h * HEAD, (h + 1) * HEAD)
        q_parts, k_parts = [], []
        for i in range(n_sub):
            r0 = i * HG_SUB
            n_keys = r0 + HG_SUB
            b_i = b_s[r0 - 1:r0, sl] if i > 0 else jnp.zeros((1, HEAD), F32)
            qd = q_s[r0:n_keys, sl] * jnp.exp(b_s[r0:n_keys, sl] - b_i)
            kd = k_s[0:n_keys, sl] * jnp.exp(b_i - b_s[0:n_keys, sl])
            q_parts.append(jnp.concatenate([zeros_sub] * i + [qd] + [zeros_sub] * (n_sub - 1 - i), axis=0))
            k_parts.append(jnp.concatenate([kd] + [zeros_sub] * (n_sub - 1 - i), axis=0))
        q_cat = jnp.concatenate(q_parts, axis=1).astype(BF16)
        k_cat = jnp.concatenate(k_parts, axis=1).astype(BF16)
        atts.append(jnp.where(causal, _dot_nt(q_cat, k_cat), 0.0).astype(BF16))

    for h in range(N_HEADS):
        sl = slice(h * HEAD, (h + 1) * HEAD)
        stt = stt_s[h]
        bh = b_s[:, sl]
        b_end = b_s[cs - 1:cs, sl]
        o = _dot(atts[h], v_s[:, sl].astype(BF16))
        o = o + _dot_nt((q_s[:, sl] * jnp.exp(bh)).astype(BF16), stt.astype(BF16))
        kd_end = k_s[:, sl] * jnp.exp(b_end - bh)
        stt_s[h] = stt * jnp.exp(b_end) + _dot_tn(v_s[:, sl].astype(BF16), kd_end.astype(BF16))
        ms = jnp.mean(o * o, axis=-1, keepdims=True)
        gate = _silu(_rows_of_chunk(x_ref, slice(3 * HG_W + h * HEAD, 3 * HG_W + (h + 1) * HEAD), cs))
        res = o * lax.rsqrt(ms + EPS) * g * gate
        o_ref[0, :, sl] = res[0:t_blk].astype(o_ref.dtype)

    @pl.when(c == nc - 1)
    def _():
        for h in range(N_HEADS):
            st_ref[0, h] = stt_s[h].T


def _hgrn(proj, lb_logits, gnorm, s0, *, layer):
    b, t, _ = proj.shape
    cs = HG_CHUNK
    assert t == 1 or t % cs == 0
    t_blk, n_chunks, t_valid = (1, 1, 1) if t == 1 else (cs, t // cs, None)
    has_s0 = s0 is not None
    depth = lb_logits.shape[0]
    in_specs = [pl.BlockSpec((1, t_blk, 4 * HG_W), lambda i, c: (i, c, COL_HG // (4 * HG_W))),
                pl.BlockSpec((depth, HG_W), lambda i, c: (0, 0)),
                pl.BlockSpec((1, HEAD), lambda i, c: (0, 0))]
    args = [proj, lb_logits.astype(F32), gnorm.reshape(1, HEAD).astype(F32)]
    if has_s0:
        in_specs.append(pl.BlockSpec((1, N_HEADS, HEAD, HEAD), lambda i, c: (i, 0, 0, 0)))
        args.append(s0)
    return pl.pallas_call(
        functools.partial(_hgrn_body, layer=layer, t_valid=t_valid, has_s0=has_s0),
        grid=(b, n_chunks),
        in_specs=in_specs,
        out_specs=[pl.BlockSpec((1, t_blk, HG_W), lambda i, c: (i, c, 0)),
                   pl.BlockSpec((1, N_HEADS, HEAD, HEAD), lambda i, c: (i, 0, 0, 0))],
        out_shape=[jax.ShapeDtypeStruct((b, t, HG_W), BF16),
                   jax.ShapeDtypeStruct((b, N_HEADS, HEAD, HEAD), F32)],
        scratch_shapes=[pltpu.VMEM((cs, HG_W), F32)] * 4 + [pltpu.VMEM((N_HEADS, HEAD, HEAD), F32)],
        compiler_params=_cparams(("parallel", "arbitrary")),
        name="hgrn2",
    )(*args)


def _lane_blocks(cols, width):
    rows = cols[0].shape[0]
    lane = lax.broadcasted_iota(jnp.int32, (rows, width * len(cols)), 1)
    out = jnp.broadcast_to(cols[-1], (rows, width * len(cols)))
    for j in range(len(cols) - 2, -1, -1):
        out = jnp.where(lane < (j + 1) * width, cols[j], out)
    return out


def _ssd_body(*refs, t_valid, has_s0):
    it = iter(refs)
    xbc_ref = next(it)
    z_ref = next(it)
    dts_ref = next(it)
    cw_ref = next(it)
    cb_ref = next(it)
    dtb_ref = next(it)
    alog_ref = next(it)
    dsk_ref = next(it)
    gn_ref = next(it)
    cs0_ref = next(it) if has_s0 else None
    s0_ref = next(it) if has_s0 else None
    y_ref = next(it)
    st_ref = next(it)
    prev_s = next(it)
    act_s = next(it)
    c = pl.program_id(1)
    cs = SSD_CHUNK
    t_blk = y_ref.shape[1]
    tail0 = 8 - (CONV_W - 1)

    @pl.when(c == 0)
    def _():
        prev_s[...] = jnp.zeros(prev_s.shape, F32)
        if has_s0:
            prev_s[tail0:8, :] = cs0_ref[0].astype(F32)
            st_ref[...] = s0_ref[...].astype(F32)
        else:
            st_ref[...] = jnp.zeros(st_ref.shape, F32)

    cur = _rows_of_chunk(xbc_ref, slice(0, CONV_DIM), cs)
    win = jnp.concatenate([prev_s[...], cur[0:8]], axis=0)
    conv = cb_ref[...] + cur * cw_ref[CONV_W - 1:CONV_W, :]
    head = cb_ref[...] + win[8:16] * cw_ref[CONV_W - 1:CONV_W, :]
    for d in range(1, CONV_W):
        w_d = cw_ref[CONV_W - 1 - d:CONV_W - d, :]
        conv = conv + pltpu.roll(cur, d, axis=0) * w_d
        head = head + pltpu.roll(win, d, axis=0)[8:16] * w_d
    act_s[...] = _silu(conv)
    act_s[0:8, :] = _silu(head)
    prev_s[...] = cur[cs - 8:cs]

    dt = _softplus(_rows_of_chunk(dts_ref, slice(0, 128), cs) + dtb_ref[...])
    if t_valid is not None:
        live = (c * cs + lax.broadcasted_iota(jnp.int32, (cs, 1), 0)) < t_valid
        dt = jnp.where(live, dt, 0.0)
    a = _cumsum_rows(dt * (-jnp.exp(alog_ref[...])))
    a_t = a.T
    dt_t = dt.T
    a_end = a[cs - 1:cs, :]
    w_upd = dt * jnp.exp(a_end - a)
    e_a = jnp.exp(a)
    e_end = jnp.exp(a_end)
    causal = _tri_incl(cs)
    lane_g = lax.broadcasted_iota(jnp.int32, (cs, SSM_GW), 1)
    row_g = lax.broadcasted_iota(jnp.int32, (SSM_GW, 1), 0)

    for g in range(SSM_GROUPS):
        xs = slice(g * SSM_GW, (g + 1) * SSM_GW)
        bsl = slice(SSM_W + g * SSM_N, SSM_W + (g + 1) * SSM_N)
        csl = slice(SSM_W + SSM_GROUPS * SSM_N + g * SSM_N, SSM_W + SSM_GROUPS * SSM_N + (g + 1) * SSM_N)
        xg = act_s[:, xs]
        bm = act_s[:, bsl].astype(BF16)
        cm = act_s[:, csl].astype(BF16)
        cbm = jnp.where(causal, _dot_nt(cm, bm), 0.0)
        s_g = st_ref[0, g]
        lanes = [DT_LANE0 + g * SSM_HPG + j for j in range(SSM_HPG)]
        yg = _dot_nt(cm, s_g.astype(BF16)) * _lane_blocks([e_a[:, ln:ln + 1] for ln in lanes], SSM_P)
        for j, ln in enumerate(lanes):
            diff = a[:, ln:ln + 1] - a_t[ln:ln + 1, :]
            mh = cbm * jnp.exp(jnp.minimum(diff, 0.0)) * dt_t[ln:ln + 1, :]
            xm = jnp.where(jnp.logical_and(lane_g >= j * SSM_P, lane_g < (j + 1) * SSM_P), xg, 0.0)
            yg = yg + _dot(mh.astype(BF16), xm.astype(BF16))
        xw = xg * _lane_blocks([w_upd[:, ln:ln + 1] for ln in lanes], SSM_P)
        decay = jnp.broadcast_to(e_end[:, lanes[-1]:lanes[-1] + 1], (SSM_GW, 1))
        for j in range(SSM_HPG - 2, -1, -1):
            decay = jnp.where(row_g < (j + 1) * SSM_P, e_end[:, lanes[j]:lanes[j] + 1], decay)
        st_ref[0, g] = s_g * decay + _dot_tn(xw.astype(BF16), bm)

        y = yg + dsk_ref[:, xs] * xg
        y = y * _silu(_rows_of_chunk(z_ref, xs, cs))
        ms = jnp.mean(y * y, axis=-1, keepdims=True)
        res = y * lax.rsqrt(ms + EPS) * gn_ref[:, xs]
        y_ref[0, :, xs] = res[0:t_blk].astype(y_ref.dtype)


def _ssd(proj, small, conv_w, conv_b, dt_bias, a_log, d_skip, gnorm, conv_s0, s0):
    b, t, _ = proj.shape
    cs = SSD_CHUNK
    assert t == 1 or t % cs == 0
    t_blk, n_chunks, t_valid = (1, 1, 1) if t == 1 else (cs, t // cs, None)
    has_s0 = s0 is not None
    pad = jnp.zeros((DT_LANE0,), F32)
    tail = jnp.zeros((128 - DT_LANE0 - SSM_HEADS,), F32)
    on_dt_lanes = lambda v: jnp.concatenate([pad, v.astype(F32), tail]).reshape(1, 128)
    const = lambda shape: pl.BlockSpec(shape, lambda i, c: (0,) * len(shape))
    in_specs = [pl.BlockSpec((1, t_blk, CONV_DIM), lambda i, c: (i, c, COL_XBC // CONV_DIM)),
                pl.BlockSpec((1, t_blk, SSM_W), lambda i, c: (i, c, COL_Z // SSM_W)),
                pl.BlockSpec((1, t_blk, 128), lambda i, c: (i, c, 0)),
                const((CONV_W, CONV_DIM)), const((1, CONV_DIM)), const((1, 128)), const((1, 128)),
                const((1, SSM_W)), const((1, SSM_W))]
    args = [proj, proj, small, conv_w.astype(F32), conv_b.reshape(1, CONV_DIM).astype(F32),
            on_dt_lanes(dt_bias), on_dt_lanes(a_log),
            jnp.repeat(d_skip.astype(F32), SSM_P).reshape(1, SSM_W), gnorm.reshape(1, SSM_W).astype(F32)]
    if has_s0:
        in_specs += [pl.BlockSpec((1, CONV_W - 1, CONV_DIM), lambda i, c: (i, 0, 0)),
                     pl.BlockSpec((1, SSM_GROUPS, SSM_GW, SSM_N), lambda i, c: (i, 0, 0, 0))]
        args += [conv_s0, s0.reshape(b, SSM_GROUPS, SSM_GW, SSM_N)]
    y, st = pl.pallas_call(
        functools.partial(_ssd_body, t_valid=t_valid, has_s0=has_s0),
        grid=(b, n_chunks),
        in_specs=in_specs,
        out_specs=[pl.BlockSpec((1, t_blk, SSM_W), lambda i, c: (i, c, 0)),
                   pl.BlockSpec((1, SSM_GROUPS, SSM_GW, SSM_N), lambda i, c: (i, 0, 0, 0))],
        out_shape=[jax.ShapeDtypeStruct((b, t, SSM_W), BF16),
                   jax.ShapeDtypeStruct((b, SSM_GROUPS, SSM_GW, SSM_N), F32)],
        scratch_shapes=[pltpu.VMEM((8, CONV_DIM), F32), pltpu.VMEM((cs, CONV_DIM), F32)],
        compiler_params=_cparams(("parallel", "arbitrary")),
        name="ssd",
    )(*args)
    return y, st.reshape(b, SSM_HEADS, SSM_P, SSM_N)


def _xattn_body(q_ref, k_ref, v_ref, gq_ref, o_ref):
    scale = HEAD ** -0.5
    g = gq_ref[...]
    for h in range(XA_HEADS):
        sl = slice(h * HEAD, (h + 1) * HEAD)
        q = q_ref[0, :, sl].astype(F32)
        ms = jnp.mean(q * q, axis=-1, keepdims=True)
        qn = q * lax.rsqrt(ms + EPS) * g
        s = _dot_nt(qn.astype(BF16), k_ref[0, :, sl].astype(BF16)) * scale
        p = jnp.exp(s - jnp.max(s, axis=-1, keepdims=True))
        o = _dot(p.astype(BF16), v_ref[0, :, sl].astype(BF16)) / jnp.sum(p, axis=-1, keepdims=True)
        o_ref[0, :, sl] = o.astype(o_ref.dtype)


def _xattn(q, mk, mv, gq):
    b, t, _ = q.shape
    n_mem = mk.shape[1]
    tq = t if t <= 512 else 512
    return pl.pallas_call(
        _xattn_body,
        grid=(b, t // tq),
        in_specs=[pl.BlockSpec((1, tq, XA_W), lambda i, j: (i, j, 0)),
                  pl.BlockSpec((1, n_mem, XA_W), lambda i, j: (i, 0, 0)),
                  pl.BlockSpec((1, n_mem, XA_W), lambda i, j: (i, 0, 0)),
                  pl.BlockSpec((1, HEAD), lambda i, j: (0, 0))],
        out_specs=pl.BlockSpec((1, tq, XA_W), lambda i, j: (i, j, 0)),
        out_shape=jax.ShapeDtypeStruct((b, t, XA_W), BF16),
        compiler_params=_cparams(("parallel", "parallel")),
        name="xattn",
    )(q, mk, mv, gq.reshape(1, HEAD).astype(F32))


IN_PIECES = (("fq", FOX_W), ("fk", FOX_W), ("fv", FOX_W), ("fg", N_HEADS), ("hg", 4 * HG_W), ("z", SSM_W),
             ("xbc", CONV_DIM), ("dt", SSM_HEADS))
IN_WIDTH = sum(width for _, width in IN_PIECES)
MAIN_COLS = dict(xbc=COL_XBC, hg=COL_HG, z=COL_Z, fq=COL_FQ, fk=COL_FK, fv=COL_FV)
W_TILE = 1024
LANES = 128


def _in_piece_offsets():
    offs, o = {}, 0
    for name, width in IN_PIECES:
        offs[name] = (o, width)
        o += width
    return offs


def _w_main_tables():
    offs = _in_piece_offsets()
    a_blk, b_blk, shift = [0] * (MAIN_W // W_TILE), [0] * (MAIN_W // W_TILE), [0] * (MAIN_W // W_TILE)
    for name, dst in MAIN_COLS.items():
        src_off, width = offs[name]
        assert dst % W_TILE == 0 and width % W_TILE == 0
        for i in range(width // W_TILE):
            start = src_off + i * W_TILE
            t = dst // W_TILE + i
            a_blk[t], shift[t] = start // W_TILE, start % W_TILE
            assert shift[t] < LANES
            b_blk[t] = (start - shift[t] + W_TILE) // LANES if shift[t] else 0
    return a_blk, b_blk, shift


def _w_main_body(tbl_ref, a_ref, b_ref, o_ref, *, shift):
    j = pl.program_id(1)

    @pl.when(tbl_ref[2, j] == 0)
    def _():
        o_ref[...] = a_ref[...].astype(BF16)

    @pl.when(tbl_ref[2, j] != 0)
    def _():
        x = jnp.concatenate([a_ref[...], b_ref[...]], axis=1)
        o_ref[...] = pltpu.roll(x, W_TILE + LANES - shift, axis=1)[:, 0:W_TILE].astype(BF16)


def _w_small_body(g_ref, d_ref, o_ref, *, g_lane, d_lane):
    lane = lax.broadcasted_iota(jnp.int32, o_ref.shape, 1)
    take_g = jnp.logical_and(lane >= g_lane, lane < g_lane + N_HEADS)
    take_d = jnp.logical_and(lane >= d_lane, lane < d_lane + SSM_HEADS)
    o_ref[...] = jnp.where(take_g, g_ref[...], jnp.where(take_d, d_ref[...], 0.0)).astype(BF16)


def _prep_w_in(w_in, layer):
    depth, d, width = w_in.shape
    assert width == IN_WIDTH
    offs = _in_piece_offsets()
    a_blk, b_blk, shift = _w_main_tables()
    shifts = {s for s in shift if s}
    assert len(shifts) == 1, "one common lane shift expected for the pieces behind the gate columns"
    tr = _pick_tile(d, (512, 256, 128))
    tables = jnp.asarray([a_blk, b_blk, shift], jnp.int32)
    main = pl.pallas_call(
        functools.partial(_w_main_body, shift=shifts.pop()),
        grid_spec=pltpu.PrefetchScalarGridSpec(
            num_scalar_prefetch=1, grid=(d // tr, MAIN_W // W_TILE),
            in_specs=[pl.BlockSpec((None, tr, W_TILE), lambda r, j, tbl: (layer, r, tbl[0, j])),
                      pl.BlockSpec((None, tr, LANES), lambda r, j, tbl: (layer, r, tbl[1, j]))],
            out_specs=pl.BlockSpec((tr, W_TILE), lambda r, j, tbl: (r, j))),
        out_shape=jax.ShapeDtypeStruct((d, MAIN_W), BF16),
        compiler_params=_cparams(("parallel", "arbitrary")),
        name="w_in_main",
    )(tables, w_in, w_in)

    g_off, d_off = offs["fg"][0], offs["dt"][0]
    assert g_off % LANES == 0 and d_off % LANES == DT_LANE0
    small = pl.pallas_call(
        functools.partial(_w_small_body, g_lane=g_off % LANES, d_lane=d_off % LANES),
        grid=(d // tr,),
        in_specs=[pl.BlockSpec((None, tr, LANES), lambda r: (layer, r, g_off // LANES)),
                  pl.BlockSpec((None, tr, LANES), lambda r: (layer, r, d_off // LANES))],
        out_specs=pl.BlockSpec((tr, LANES), lambda r: (r, 0)),
        out_shape=jax.ShapeDtypeStruct((d, LANES), BF16),
        compiler_params=_cparams(("parallel",)),
        name="w_in_small",
    )(w_in, w_in)
    return main, small


def _mixers(x, lw, l, *, wb, fox_past, hg_s0, ssm_s0, conv_s0, page_table):
    b, t, d = x.shape
    x2 = x.reshape(b * t, d)
    few_rows = wb is None
    proj2 = _matmul(x2, lw["w_main"], gain=lw["norm_mix"], name="in_proj")
    small2 = _matmul(x2, lw["w_small"], gain=lw["norm_mix"], name="in_proj_small")
    proj = proj2.reshape(b, t, MAIN_W)
    small = small2.reshape(b, t, 128)

    gates_t = jnp.swapaxes(small[:, :, 0:N_HEADS], 1, 2)
    logf_t, cum_t = _fox_gate(gates_t, lw["fox_bf"])
    logf = jnp.swapaxes(logf_t, 1, 2)

    if fox_past is None:
        qn, kn, knb, fv, vb = _fox_prep(proj2, lw["fox_gq"], lw["fox_gk"])
        fo = _fox_flash(qn.reshape(b, t, FOX_W), knb.reshape(b, t, FOX_W), vb.reshape(b, t, FOX_W),
                        jnp.swapaxes(cum_t, 1, 2), cum_t)
    else:
        qn = _headnorm(proj2, lw["fox_gq"], col0=COL_FQ, width=FOX_W)
        kn = _headnorm(proj2, lw["fox_gk"], col0=COL_FK, width=FOX_W)
        fv = proj2[:, COL_FV:COL_FV + FOX_W]
        cache_k, cache_v, cache_lf = fox_past
        fo = _fox_decode(l, qn.reshape(b, N_HEADS, HEAD), kn.reshape(b, N_HEADS, HEAD),
                         fv.reshape(b, N_HEADS, HEAD), logf_t, cache_k, cache_v, cache_lf, page_table)
        fo = fo.reshape(b, t, FOX_W)

    ho, hg_state = _hgrn(proj, lw["hg_lb_logits"], lw["hg_gnorm"], hg_s0, layer=l)
    sy, ssm_state = _ssd(proj, small, lw["conv_w"], lw["conv_b"], lw["dt_bias"], lw["a_log"],
                         lw["d_skip"], lw["ssm_gnorm"], conv_s0, ssm_s0)
    mixed = (fo.reshape(b * t, FOX_W), ho.reshape(b * t, HG_W), sy.reshape(b * t, SSM_W))
    if few_rows:
        x_new, wb_out = _out_proj(*mixed, lw["w_out"], x2, cast_w=True, layer=l)
    else:
        x_new, wb_out = _out_proj(*mixed, wb["w_out"], x2), None
    x_new = x_new.reshape(b, t, d)

    keep = CONV_W - 1
    if t >= keep:
        conv_state = proj[:, t - keep:, COL_XBC:COL_XBC + CONV_DIM]
    else:
        prev = jnp.zeros((b, keep, CONV_DIM), F32) if conv_s0 is None else conv_s0.astype(F32)
        conv_state = jnp.concatenate([prev[:, t:], proj[:, :, COL_XBC:COL_XBC + CONV_DIM]], axis=1)
    fk = kn.reshape(b, t, N_HEADS, HEAD)
    return x_new, fk, fv.reshape(b, t, N_HEADS, HEAD), logf, hg_state, ssm_state, conv_state, wb_out


def _cross_and_mlp(x, lw, l, mk, mv, *, wb):
    b, t, d = x.shape
    x2 = x.reshape(b * t, d)
    q = _matmul(x2, lw["xa_wq"], gain=lw["norm_xa"], name="xa_q")
    o = _xattn(q.reshape(b, t, XA_W), mk, mv, lw["xa_gq"])
    x2 = _matmul(o.reshape(b * t, XA_W), lw["xa_wo"], res=x2, tm_max=1024, name="xa_out")
    if wb is None:
        u, wb_up = _matmul(x2, lw["w_up"], gain=lw["norm_mlp"], act="relu2", out_dtype=BF16, tn_max=1024,
                           cast_w=True, layer=l, name="mlp_up")
        x2, wb_down = _matmul(u, lw["w_down"], res=x2, tk=2048, cast_w=True, layer=l, name="mlp_down")
        made = dict(w_up=wb_up, w_down=wb_down)
    else:
        u = _matmul(x2, wb["w_up"], gain=lw["norm_mlp"], act="relu2", out_dtype=BF16, name="mlp_up")
        x2 = _matmul(u, wb["w_down"], res=x2, tm_max=1024, tk=2048, name="mlp_down")
        made = None
    return x2.reshape(b, t, d), made


def kernel(x_prompt, x_sample, cache_fox_k, cache_fox_v, cache_fox_logf, cache_mem_k, cache_mem_v, state_hgrn, state_ssm, state_conv, page_table, mem_prompt, norm_mix, w_in, fox_gq, fox_gk, fox_bf, hg_lb_logits, hg_gnorm, conv_w, conv_b, dt_bias, a_log, d_skip, ssm_gnorm, w_out, norm_xa, norm_mem, xa_wq, xa_wk, xa_wv, xa_gq, xa_gk, xa_wo, norm_mlp, w_up, w_down):
    depth = w_in.shape[0]
    bp = x_prompt.shape[0]
    n_mem = mem_prompt.shape[1]
    pool = cache_fox_k.shape[1]
    cache_k = cache_fox_k.reshape(depth, pool, PAGE * N_HEADS, HEAD)
    cache_v = cache_fox_v.reshape(depth, pool, PAGE * N_HEADS, HEAD)
    cache_lf = cache_fox_logf.reshape(depth, pool, 1, PAGE * N_HEADS)
    mem2 = mem_prompt.reshape(bp * n_mem, -1)

    xp, xs = x_prompt, x_sample
    outs = {k: [] for k in ("p_fk", "p_fv", "p_fl", "p_hg", "p_ss", "p_cv", "p_mk", "p_mv",
                            "s_fk", "s_fv", "s_fl", "s_hg", "s_ss", "s_cv")}
    for l in range(depth):
        w_main, w_small = _prep_w_in(w_in, l)
        lw = dict(w_main=w_main, w_small=w_small, norm_mix=norm_mix[l],
                  fox_gq=fox_gq[l], fox_gk=fox_gk[l], fox_bf=fox_bf[l], hg_lb_logits=hg_lb_logits,
                  hg_gnorm=hg_gnorm[l], conv_w=conv_w[l], conv_b=conv_b[l], dt_bias=dt_bias[l], a_log=a_log[l],
                  d_skip=d_skip[l], ssm_gnorm=ssm_gnorm[l], w_out=w_out, norm_xa=norm_xa[l],
                  xa_wq=xa_wq[l].astype(BF16), xa_gq=xa_gq[l], xa_wo=xa_wo[l].astype(BF16),
                  norm_mlp=norm_mlp[l], w_up=w_up, w_down=w_down)

        bs = xs.shape[0]
        xs, fk, fv, fl, hg, ss, cv, wb_out = _mixers(xs, lw, l, wb=None, fox_past=(cache_k, cache_v, cache_lf),
                                                     hg_s0=state_hgrn[l], ssm_s0=state_ssm[l],
                                                     conv_s0=state_conv[l], page_table=page_table)
        xs, wb = _cross_and_mlp(xs, lw, l, cache_mem_k[l].reshape(bs, n_mem, XA_W),
                                cache_mem_v[l].reshape(bs, n_mem, XA_W), wb=None)
        wb["w_out"] = wb_out
        for key, val in zip(("s_fk", "s_fv", "s_fl", "s_hg", "s_ss", "s_cv"), (fk, fv, fl, hg, ss, cv)):
            outs[key].append(val)

        xp, fk, fv, fl, hg, ss, cv, _ = _mixers(xp, lw, l, wb=wb, fox_past=None, hg_s0=None, ssm_s0=None,
                                                conv_s0=None, page_table=None)
        w_kv = jnp.concatenate([xa_wk[l], xa_wv[l]], axis=1).astype(BF16)
        kv = _matmul(mem2, w_kv, gain=norm_mem[l], name="mem_kv")
        mk = _headnorm(kv, xa_gk[l], col0=0, width=XA_W).reshape(bp, n_mem, XA_W)
        mv = kv[:, XA_W:].reshape(bp, n_mem, XA_W)
        xp, _ = _cross_and_mlp(xp, lw, l, mk, mv, wb=wb)
        for key, val in zip(("p_fk", "p_fv", "p_fl", "p_hg", "p_ss", "p_cv"), (fk, fv, fl, hg, ss, cv)):
            outs[key].append(val)
        outs["p_mk"].append(mk.reshape(bp, n_mem, XA_HEADS, HEAD))
        outs["p_mv"].append(mv.reshape(bp, n_mem, XA_HEADS, HEAD))

    st = {k: jnp.stack(v) for k, v in outs.items()}
    return (xp, xs, st["p_fk"], st["p_fv"], st["p_fl"], st["p_hg"], st["p_ss"], st["p_cv"], st["p_mk"],
            st["p_mv"], st["s_fk"], st["s_fv"], st["s_fl"], st["s_hg"], st["s_ss"], st["s_cv"])
```

```python
import functools

import jax
import jax.numpy as jnp
from jax import lax
from jax.experimental import pallas as pl
from jax.experimental.pallas import tpu as pltpu

F32 = jnp.float32
BF16 = jnp.bfloat16

EPS = 1e-6
MASK_VALUE = -1e30
HEAD = 128
N_HEADS = 8
FOX_W = N_HEADS * HEAD
HG_W = N_HEADS * HEAD
SSM_W = 2048
SSM_P = 64
SSM_HEADS = SSM_W // SSM_P
SSM_GROUPS = 8
SSM_HPG = SSM_HEADS // SSM_GROUPS
SSM_N = 128
SSM_GW = SSM_HPG * SSM_P
CONV_W = 4
CONV_DIM = SSM_W + 2 * SSM_GROUPS * SSM_N
XA_HEADS = 4
XA_W = XA_HEADS * HEAD
HG_CHUNK = 64
HG_SUB = 16
SSD_CHUNK = 128
PAGE = 128
DT_LANE0 = N_HEADS

COL_XBC = 0
COL_HG = CONV_DIM
COL_Z = COL_HG + 4 * HG_W
COL_FQ = COL_Z + SSM_W
COL_FK = COL_FQ + FOX_W
COL_FV = COL_FK + FOX_W
MAIN_W = COL_FV + FOX_W

V7X_VMEM_LIMIT = 56 * 1024 * 1024


def _cparams(sem, vmem=V7X_VMEM_LIMIT):
    return pltpu.CompilerParams(dimension_semantics=sem, vmem_limit_bytes=vmem)


def _sigmoid(x):
    return 1.0 / (1.0 + jnp.exp(-x))


def _silu(x):
    h = 0.5 * x
    return h + h * jnp.tanh(h)


def _softplus(x):
    return jnp.maximum(x, 0.0) + jnp.log1p(jnp.exp(-jnp.abs(x)))


def _log_sigmoid(x):
    return -_softplus(-x)


def _dot_nt(a, b):
    return lax.dot_general(a, b, (((1,), (1,)), ((), ())), preferred_element_type=F32)


def _dot_tn(a, b):
    return lax.dot_general(a, b, (((0,), (0,)), ((), ())), preferred_element_type=F32)


def _dot(a, b):
    return jnp.dot(a, b, preferred_element_type=F32)


def _tri_incl(n):
    r = lax.broadcasted_iota(jnp.int32, (n, n), 0)
    c = lax.broadcasted_iota(jnp.int32, (n, n), 1)
    return r >= c


def _cumsum_rows(x):
    n = x.shape[0]
    tri = _tri_incl(n).astype(BF16)
    hi = x.astype(BF16)
    r1 = x - hi.astype(F32)
    mid = r1.astype(BF16)
    lo = (r1 - mid.astype(F32)).astype(BF16)
    return _dot(tri, hi) + _dot(tri, mid) + _dot(tri, lo)


def _mm_body(*refs, nk, norm, act, has_res, cast_w, trans_w):
    it = iter(refs)
    a_ref = next(it)
    g_ref = next(it) if norm else None
    w_ref = next(it)
    r_ref = next(it) if has_res else None
    o_ref = next(it)
    wb_ref = next(it) if cast_w else None
    h_ref = next(it) if norm else None
    acc_ref = next(it) if nk > 1 else None
    j = pl.program_id(1)
    k = pl.program_id(2)

    if norm:
        @pl.when(j == 0)
        def _():
            x = a_ref[...].astype(F32)
            ms = jnp.mean(x * x, axis=-1, keepdims=True)
            h_ref[...] = (x * lax.rsqrt(ms + EPS) * g_ref[...]).astype(BF16)
        a = h_ref[...]
    else:
        a = a_ref[...]

    if cast_w:
        wb = w_ref[...].astype(BF16)
        wb_ref[...] = wb
        p = _dot(a, wb)
    elif trans_w:
        p = _dot_nt(a, w_ref[...])
    else:
        p = _dot(a, w_ref[...])

    def finish(v):
        if act == "relu2":
            v = jnp.square(jnp.maximum(v, 0.0))
        if has_res:
            v = v + r_ref[...]
        o_ref[...] = v.astype(o_ref.dtype)

    if nk == 1:
        finish(p)
    else:
        @pl.when(k == 0)
        def _():
            acc_ref[...] = p

        @pl.when(k > 0)
        def _():
            acc_ref[...] += p

        @pl.when(k == nk - 1)
        def _():
            finish(acc_ref[...])


def _pick_tile(n, candidates):
    for c in candidates:
        if n % c == 0:
            return c
    return n


def _matmul(a, w, *, gain=None, res=None, act=None, out_dtype=F32, tm_max=512, tn_max=None, tk=None,
            cast_w=False, layer=None, trans_w=False, name="matmul"):
    m, kdim = a.shape
    n = w.shape[0] if trans_w else w.shape[-1]
    small_m = m <= 64
    tm = m if small_m else _pick_tile(m, tuple(c for c in (1024, 512, 256, 128) if c <= tm_max))
    tn_max = tn_max or (2048 if small_m else 1024)
    tn = _pick_tile(n, tuple(c for c in (2048, 1024, 512, 256, 128) if c <= tn_max))
    tk = kdim if tk is None else tk
    nk = kdim // tk
    norm = gain is not None
    assert not (norm and nk > 1)
    assert not cast_w or m == tm
    assert not trans_w or (layer is None and not cast_w)
    in_specs = [pl.BlockSpec((tm, tk), lambda i, j, k: (i, k))]
    args = [a]
    if norm:
        in_specs.append(pl.BlockSpec((1, kdim), lambda i, j, k: (0, 0)))
        args.append(gain.reshape(1, kdim).astype(F32))
    if trans_w:
        in_specs.append(pl.BlockSpec((tn, tk), lambda i, j, k: (j, k)))
    elif layer is None:
        in_specs.append(pl.BlockSpec((tk, tn), lambda i, j, k: (k, j)))
    else:
        in_specs.append(pl.BlockSpec((None, tk, tn), lambda i, j, k: (layer, k, j)))
    args.append(w)
    if res is not None:
        in_specs.append(pl.BlockSpec((tm, tn), lambda i, j, k: (i, j)))
        args.append(res)
    scratch = []
    if norm:
        scratch.append(pltpu.VMEM((tm, kdim), BF16))
    if nk > 1:
        scratch.append(pltpu.VMEM((tm, tn), F32))
    out_specs = pl.BlockSpec((tm, tn), lambda i, j, k: (i, j))
    out_shape = jax.ShapeDtypeStruct((m, n), out_dtype)
    if cast_w:
        out_specs = [out_specs, pl.BlockSpec((tk, tn), lambda i, j, k: (k, j))]
        out_shape = [out_shape, jax.ShapeDtypeStruct((kdim, n), BF16)]
    return pl.pallas_call(
        functools.partial(_mm_body, nk=nk, norm=norm, act=act, has_res=res is not None, cast_w=cast_w,
                          trans_w=trans_w),
        grid=(m // tm, n // tn, nk),
        in_specs=in_specs,
        out_specs=out_specs,
        out_shape=out_shape,
        scratch_shapes=scratch,
        compiler_params=_cparams(("parallel", "arbitrary", "arbitrary")),
        name=name,
    )(*args)


def _outproj_body(fo_ref, ho_ref, sy_ref, w_ref, r_ref, o_ref, *maybe_wb_ref):
    if maybe_wb_ref:
        maybe_wb_ref[0][...] = w_ref[...].astype(BF16)
        w_ref = maybe_wb_ref[0]
    k1 = FOX_W + HG_W
    p = (_dot(fo_ref[...], w_ref[0:FOX_W, :]) + _dot(ho_ref[...], w_ref[FOX_W:k1, :])
         + _dot(sy_ref[...], w_ref[k1:k1 + SSM_W, :]))
    o_ref[...] = p + r_ref[...]


def _out_proj(fo, ho, sy, w, res, *, cast_w=False, layer=None):
    m = fo.shape[0]
    kdim, n = w.shape[-2:]
    tm = m if m <= 64 else _pick_tile(m, (1024, 512, 256, 128))
    tn = _pick_tile(n, (1024, 512, 256, 128))
    assert fo.shape[1] == FOX_W and ho.shape[1] == HG_W and sy.shape[1] == SSM_W
    assert not cast_w or m == tm
    out_specs = pl.BlockSpec((tm, tn), lambda i, j: (i, j))
    out_shape = jax.ShapeDtypeStruct((m, n), F32)
    if cast_w:
        out_specs = [out_specs, pl.BlockSpec((kdim, tn), lambda i, j: (0, j))]
        out_shape = [out_shape, jax.ShapeDtypeStruct((kdim, n), BF16)]
    return pl.pallas_call(
        _outproj_body,
        grid=(m // tm, n // tn),
        in_specs=[pl.BlockSpec((tm, FOX_W), lambda i, j: (i, 0)),
                  pl.BlockSpec((tm, HG_W), lambda i, j: (i, 0)),
                  pl.BlockSpec((tm, SSM_W), lambda i, j: (i, 0)),
                  pl.BlockSpec((kdim, tn), lambda i, j: (0, j)) if layer is None
                  else pl.BlockSpec((None, kdim, tn), lambda i, j: (layer, 0, j)),
                  pl.BlockSpec((tm, tn), lambda i, j: (i, j))],
        out_specs=out_specs,
        out_shape=out_shape,
        compiler_params=_cparams(("parallel", "arbitrary")),
        name="out_proj",
    )(fo, ho, sy, w, res)


def _headnorm_body(x_ref, g_ref, o_ref, *, n_heads):
    g = g_ref[...]
    for h in range(n_heads):
        sl = slice(h * HEAD, (h + 1) * HEAD)
        x = x_ref[:, sl].astype(F32)
        ms = jnp.mean(x * x, axis=-1, keepdims=True)
        o_ref[:, sl] = (x * lax.rsqrt(ms + EPS) * g).astype(o_ref.dtype)


def _headnorm(x, gain, *, col0, width, out_dtype=F32):
    m = x.shape[0]
    tm = m if m <= 64 else _pick_tile(m, (512, 256, 128))
    return pl.pallas_call(
        functools.partial(_headnorm_body, n_heads=width // HEAD),
        grid=(m // tm,),
        in_specs=[pl.BlockSpec((tm, width), lambda i: (i, col0 // width)),
                  pl.BlockSpec((1, HEAD), lambda i: (0, 0))],
        out_specs=pl.BlockSpec((tm, width), lambda i: (i, 0)),
        out_shape=jax.ShapeDtypeStruct((m, width), out_dtype),
        compiler_params=_cparams(("parallel",)),
        name="headnorm",
    )(x, gain.reshape(1, HEAD).astype(F32))


def _fox_gate_body(g_ref, bf_ref, lf_ref, cum_ref, *, t):
    lf = _log_sigmoid(g_ref[0] + bf_ref[...])
    lf_ref[0] = lf
    lane = lax.broadcasted_iota(jnp.int32, lf.shape, 1)
    c = lf
    shift = 1
    while shift < t:
        c = c + jnp.where(lane >= shift, pltpu.roll(c, shift, axis=1), 0.0)
        shift *= 2
    cum_ref[0] = c


def _fox_gate(gates_t, fox_bf):
    b, h, t = gates_t.shape
    spec = pl.BlockSpec((1, h, t), lambda i: (i, 0, 0))
    return pl.pallas_call(
        functools.partial(_fox_gate_body, t=t),
        grid=(b,),
        in_specs=[spec, pl.BlockSpec((h, 1), lambda i: (0, 0))],
        out_specs=[spec, spec],
        out_shape=[jax.ShapeDtypeStruct((b, h, t), F32)] * 2,
        compiler_params=_cparams(("parallel",)),
        name="fox_gate",
    )(gates_t, fox_bf.reshape(h, 1).astype(F32))


FLASH_STRIP = 256


def _fox_flash_body(q_ref, k_ref, v_ref, cq_ref, ck_ref, o_ref, m_ref, l_ref, acc_ref, *, tq):
    qi = pl.program_id(1)
    ki = pl.program_id(2)
    scale = HEAD ** -0.5

    @pl.when(ki == 0)
    def _():
        m_ref[...] = jnp.full(m_ref.shape, MASK_VALUE, F32)
        l_ref[...] = jnp.zeros(l_ref.shape, F32)
        acc_ref[...] = jnp.zeros(acc_ref.shape, F32)

    def update(diagonal):
        for r0 in range(0, tq, FLASH_STRIP):
            rows = slice(r0, r0 + FLASH_STRIP)
            if diagonal:
                keep = (r0 + lax.broadcasted_iota(jnp.int32, (FLASH_STRIP, tq), 0)
                        >= lax.broadcasted_iota(jnp.int32, (FLASH_STRIP, tq), 1))
            for h in range(N_HEADS):
                sl = slice(h * HEAD, (h + 1) * HEAD)
                s = _dot_nt(q_ref[0, rows, sl], k_ref[0, :, sl]) * scale
                s = s + cq_ref[0, rows, h:h + 1] - ck_ref[0, h:h + 1, :]
                if diagonal:
                    s = jnp.where(keep, s, MASK_VALUE)
                m_prev = m_ref[h, rows]
                m_new = jnp.maximum(m_prev, jnp.max(s, axis=-1, keepdims=True))
                alpha = jnp.exp(m_prev - m_new)
                p = jnp.exp(s - m_new)
                l_ref[h, rows] = alpha * l_ref[h, rows] + jnp.sum(p, axis=-1, keepdims=True)
                acc_ref[rows, sl] = alpha * acc_ref[rows, sl] + _dot(p.astype(BF16), v_ref[0, :, sl])
                m_ref[h, rows] = m_new

    @pl.when(ki < qi)
    def _():
        update(diagonal=False)

    @pl.when(ki == qi)
    def _():
        update(diagonal=True)
        for h in range(N_HEADS):
            sl = slice(h * HEAD, (h + 1) * HEAD)
            o_ref[0, :, sl] = (acc_ref[:, sl] / l_ref[h]).astype(o_ref.dtype)


def _fox_prep_body(q_ref, k_ref, v_ref, gq_ref, gk_ref, qn_ref, kn_ref, knb_ref, vf_ref, vb_ref):
    for h in range(N_HEADS):
        sl = slice(h * HEAD, (h + 1) * HEAD)
        q = q_ref[:, sl]
        qn_ref[:, sl] = (q * lax.rsqrt(jnp.mean(q * q, axis=-1, keepdims=True) + EPS) * gq_ref[...]).astype(BF16)
        k = k_ref[:, sl]
        kn = k * lax.rsqrt(jnp.mean(k * k, axis=-1, keepdims=True) + EPS) * gk_ref[...]
        kn_ref[:, sl] = kn
        knb_ref[:, sl] = kn.astype(BF16)
    v = v_ref[...]
    vf_ref[...] = v
    vb_ref[...] = v.astype(BF16)


def _fox_prep(proj2, gq, gk):
    m = proj2.shape[0]
    tm = _pick_tile(m, (512, 256, 128))
    col = lambda c: pl.BlockSpec((tm, FOX_W), lambda i: (i, c // FOX_W))
    out = pl.BlockSpec((tm, FOX_W), lambda i: (i, 0))
    gain = pl.BlockSpec((1, HEAD), lambda i: (0, 0))
    return pl.pallas_call(
        _fox_prep_body,
        grid=(m // tm,),
        in_specs=[col(COL_FQ), col(COL_FK), col(COL_FV), gain, gain],
        out_specs=[out] * 5,
        out_shape=[jax.ShapeDtypeStruct((m, FOX_W), dt) for dt in (BF16, F32, BF16, F32, BF16)],
        compiler_params=_cparams(("parallel",)),
        name="fox_prep",
    )(proj2, proj2, proj2, gq.reshape(1, HEAD).astype(F32), gk.reshape(1, HEAD).astype(F32))


def _fox_flash(qn, kn, vb, cum_col, cum_row, *, tq=512):
    b, t, _ = qn.shape
    nq = t // tq
    return pl.pallas_call(
        functools.partial(_fox_flash_body, tq=tq),
        grid=(b, nq, nq),
        in_specs=[
            pl.BlockSpec((1, tq, FOX_W), lambda i, q, k: (i, q, 0)),
            pl.BlockSpec((1, tq, FOX_W), lambda i, q, k: (i, jnp.minimum(k, q), 0)),
            pl.BlockSpec((1, tq, FOX_W), lambda i, q, k: (i, jnp.minimum(k, q), 0)),
            pl.BlockSpec((1, tq, N_HEADS), lambda i, q, k: (i, q, 0)),
            pl.BlockSpec((1, N_HEADS, tq), lambda i, q, k: (i, 0, jnp.minimum(k, q))),
        ],
        out_specs=pl.BlockSpec((1, tq, FOX_W), lambda i, q, k: (i, q, 0)),
        out_shape=jax.ShapeDtypeStruct((b, t, FOX_W), BF16),
        scratch_shapes=[pltpu.VMEM((N_HEADS, tq, 1), F32), pltpu.VMEM((N_HEADS, tq, 1), F32),
                        pltpu.VMEM((tq, FOX_W), F32)],
        compiler_params=_cparams(("parallel", "parallel", "arbitrary")),
        name="fox_flash",
    )(qn, kn, vb, cum_col, cum_row)


def _fox_decode_body(pt_ref, q_ref, kn_ref, vn_ref, lfn_ref, *rest, pages_per_step, n_steps):
    del pt_ref
    pps = pages_per_step
    k_refs = rest[:pps]
    v_refs = rest[pps:2 * pps]
    lf_refs = rest[2 * pps:3 * pps]
    o_ref, m_ref, l_ref, acc_ref, carry_ref = rest[3 * pps:]
    s = pl.program_id(1)
    scale = HEAD ** -0.5
    rows = PAGE * N_HEADS
    lane = lax.broadcasted_iota(jnp.int32, (1, rows), 1)
    own = (lax.broadcasted_iota(jnp.int32, (N_HEADS, rows), 1) % N_HEADS
           == lax.broadcasted_iota(jnp.int32, (N_HEADS, rows), 0))

    @pl.when(s == 0)
    def _():
        m_ref[...] = jnp.sum(q_ref[0] * kn_ref[0], axis=-1, keepdims=True) * scale
        l_ref[...] = jnp.ones(l_ref.shape, F32)
        acc_ref[...] = vn_ref[0]
        carry_ref[...] = jnp.zeros(carry_ref.shape, F32)

    lf = jnp.concatenate([lf_refs[r][0, 0] for r in range(pps)], axis=0)
    tot = lf
    suf = lf
    shift = N_HEADS
    while shift < rows:
        tot = tot + pltpu.roll(tot, shift, axis=1)
        suf = suf + jnp.where(lane + shift < rows, pltpu.roll(suf, rows - shift, axis=1), 0.0)
        shift *= 2
    page = lax.broadcasted_iota(jnp.int32, (pps, 1), 0)
    newer = tot
    shift = 1
    while shift < pps:
        newer = newer + jnp.where(page >= shift, pltpu.roll(newer, shift, axis=0), 0.0)
        shift *= 2
    after = suf - lf + (newer - tot) + carry_ref[...]
    carry_ref[...] = carry_ref[...] + newer[pps - 1:pps, :]

    q = q_ref[0].astype(BF16)
    scs = []
    for r in range(pps):
        sc = _dot_nt(q, k_refs[r][0, 0].astype(BF16)) * scale + lfn_ref[0] + after[r:r + 1, :]
        scs.append(jnp.where(own, sc, MASK_VALUE))
    m_prev = m_ref[...]
    m_new = m_prev
    for sc in scs:
        m_new = jnp.maximum(m_new, jnp.max(sc, axis=-1, keepdims=True))
    alpha = jnp.exp(m_prev - m_new)
    l_new = alpha * l_ref[...]
    acc = alpha * acc_ref[...]
    for r, sc in enumerate(scs):
        p = jnp.exp(sc - m_new)
        l_new = l_new + jnp.sum(p, axis=-1, keepdims=True)
        acc = acc + _dot(p.astype(BF16), v_refs[r][0, 0].astype(BF16))
    l_ref[...] = l_new
    acc_ref[...] = acc
    m_ref[...] = m_new

    @pl.when(s == n_steps - 1)
    def _():
        o_ref[0] = (acc_ref[...] / l_ref[...]).astype(o_ref.dtype)


def _fox_decode(layer, qn, kn, v_new, lf_new, cache_k, cache_v, cache_lf, page_table, *, pages_per_step=8):
    b = qn.shape[0]
    n_pages = page_table.shape[1]
    pps = pages_per_step
    n_steps = n_pages // pps
    rows = PAGE * N_HEADS

    def page_map(r):
        return lambda i, s, pt: (layer, pt[i, n_pages - 1 - (s * pps + r)], 0, 0)

    head = pl.BlockSpec((1, N_HEADS, HEAD), lambda i, s, pt: (i, 0, 0))
    in_specs = [head, head, head, pl.BlockSpec((1, N_HEADS, 1), lambda i, s, pt: (i, 0, 0))]
    in_specs += [pl.BlockSpec((1, 1, rows, HEAD), page_map(r)) for r in range(pps)]
    in_specs += [pl.BlockSpec((1, 1, rows, HEAD), page_map(r)) for r in range(pps)]
    in_specs += [pl.BlockSpec((1, 1, 1, rows), page_map(r)) for r in range(pps)]
    grid_spec = pltpu.PrefetchScalarGridSpec(
        num_scalar_prefetch=1, grid=(b, n_steps), in_specs=in_specs,
        out_specs=pl.BlockSpec((1, N_HEADS, HEAD), lambda i, s, pt: (i, 0, 0)),
        scratch_shapes=[pltpu.VMEM((N_HEADS, 1), F32), pltpu.VMEM((N_HEADS, 1), F32),
                        pltpu.VMEM((N_HEADS, HEAD), F32), pltpu.VMEM((1, rows), F32)])
    return pl.pallas_call(
        functools.partial(_fox_decode_body, pages_per_step=pps, n_steps=n_steps),
        grid_spec=grid_spec,
        out_shape=jax.ShapeDtypeStruct((b, N_HEADS, HEAD), BF16),
        compiler_params=_cparams(("parallel", "arbitrary")),
        name="fox_decode",
    )(page_table, qn, kn, v_new, lf_new, *([cache_k] * pps), *([cache_v] * pps), *([cache_lf] * pps))


def _rows_of_chunk(ref, cols, chunk_rows):
    x = ref[0, :, cols].astype(F32)
    if x.shape[0] == chunk_rows:
        return x
    assert x.shape[0] == 1
    return jnp.broadcast_to(x, (chunk_rows, x.shape[1]))


def _hgrn_body(*refs, layer, t_valid, has_s0):
    it = iter(refs)
    x_ref = next(it)
    lbl_ref = next(it)
    gn_ref = next(it)
    s0_ref = next(it) if has_s0 else None
    o_ref = next(it)
    st_ref = next(it)
    q_s, k_s, v_s, b_s, stt_s = (next(it) for _ in range(5))
    c = pl.program_id(1)
    nc = pl.num_programs(1)
    cs = HG_CHUNK
    t_blk = o_ref.shape[1]

    @pl.when(c == 0)
    def _():
        for h in range(N_HEADS):
            if has_s0:
                stt_s[h] = s0_ref[0, h].astype(F32).T
            else:
                stt_s[h] = jnp.zeros((HEAD, HEAD), F32)

    lg = lbl_ref[...].astype(F32)
    e = jnp.exp(lg - jnp.max(lg, axis=0, keepdims=True))
    pr = e / jnp.sum(e, axis=0, keepdims=True)
    lb = jnp.sum(pr[0:layer + 1], axis=0, keepdims=True) - pr[0:1]

    hf = _rows_of_chunk(x_ref, slice(HG_W, 2 * HG_W), cs)
    logf = jnp.log(lb + (1.0 - lb) * _sigmoid(hf))
    kk = (1.0 - lb) * _sigmoid(-hf)
    qq = _silu(_rows_of_chunk(x_ref, slice(0, HG_W), cs))
    vv = _rows_of_chunk(x_ref, slice(2 * HG_W, 3 * HG_W), cs)
    if t_valid is not None:
        live = (c * cs + lax.broadcasted_iota(jnp.int32, (cs, 1), 0)) < t_valid
        logf = jnp.where(live, logf, 0.0)
        kk = jnp.where(live, kk, 0.0)
        qq = jnp.where(live, qq, 0.0)
        vv = jnp.where(live, vv, 0.0)
    q_s[...] = qq
    k_s[...] = kk
    v_s[...] = vv
    b_s[...] = _cumsum_rows(logf)

    n_sub = cs // HG_SUB
    causal = _tri_incl(cs)
    zeros_sub = jnp.zeros((HG_SUB, HEAD), F32)
    g = gn_ref[...]

    atts = []
    for h in range(N_HEADS):
        sl = slice(h * HEAD, (h + 1) * HEAD)
        q_parts, k_parts = [], []
        for i in range(n_sub):
            r0 = i * HG_SUB
            n_keys = r0 + HG_SUB
            b_i = b_s[r0 - 1:r0, sl] if i > 0 else jnp.zeros((1, HEAD), F32)
            qd = q_s[r0:n_keys, sl] * jnp.exp(b_s[r0:n_keys, sl] - b_i)
            kd = k_s[0:n_keys, sl] * jnp.exp(b_i - b_s[0:n_keys, sl])
            q_parts.append(jnp.concatenate([zeros_sub] * i + [qd] + [zeros_sub] * (n_sub - 1 - i), axis=0))
            k_parts.append(jnp.concatenate([kd] + [zeros_sub] * (n_sub - 1 - i), axis=0))
        q_cat = jnp.concatenate(q_parts, axis=1).astype(BF16)
        k_cat = jnp.concatenate(k_parts, axis=1).astype(BF16)
        atts.append(jnp.where(causal, _dot_nt(q_cat, k_cat), 0.0).astype(BF16))

    for h in range(N_HEADS):
        sl = slice(h * HEAD, (h + 1) * HEAD)
        stt = stt_s[h]
        bh = b_s[:, sl]
        b_end = b_s[cs - 1:cs, sl]
        o = _dot(atts[h], v_s[:, sl].astype(BF16))
        o = o + _dot_nt((q_s[:, sl] * jnp.exp(bh)).astype(BF16), stt.astype(BF16))
        kd_end = k_s[:, sl] * jnp.exp(b_end - bh)
        stt_s[h] = stt * jnp.exp(b_end) + _dot_tn(v_s[:, sl].astype(BF16), kd_end.astype(BF16))
        ms = jnp.mean(o * o, axis=-1, keepdims=True)
        gate = _silu(_rows_of_chunk(x_ref, slice(3 * HG_W + h * HEAD, 3 * HG_W + (h + 1) * HEAD), cs))
        res = o * lax.rsqrt(ms + EPS) * g * gate
        o_ref[0, :, sl] = res[0:t_blk].astype(o_ref.dtype)

    @pl.when(c == nc - 1)
    def _():
        for h in range(N_HEADS):
            st_ref[0, h] = stt_s[h].T


def _hgrn(proj, lb_logits, gnorm, s0, *, layer):
    b, t, _ = proj.shape
    cs = HG_CHUNK
    assert t == 1 or t % cs == 0
    t_blk, n_chunks, t_valid = (1, 1, 1) if t == 1 else (cs, t // cs, None)
    has_s0 = s0 is not None
    depth = lb_logits.shape[0]
    in_specs = [pl.BlockSpec((1, t_blk, 4 * HG_W), lambda i, c: (i, c, COL_HG // (4 * HG_W))),
                pl.BlockSpec((depth, HG_W), lambda i, c: (0, 0)),
                pl.BlockSpec((1, HEAD), lambda i, c: (0, 0))]
    args = [proj, lb_logits.astype(F32), gnorm.reshape(1, HEAD).astype(F32)]
    if has_s0:
        in_specs.append(pl.BlockSpec((1, N_HEADS, HEAD, HEAD), lambda i, c: (i, 0, 0, 0)))
        args.append(s0)
    return pl.pallas_call(
        functools.partial(_hgrn_body, layer=layer, t_valid=t_valid, has_s0=has_s0),
        grid=(b, n_chunks),
        in_specs=in_specs,
        out_specs=[pl.BlockSpec((1, t_blk, HG_W), lambda i, c: (i, c, 0)),
                   pl.BlockSpec((1, N_HEADS, HEAD, HEAD), lambda i, c: (i, 0, 0, 0))],
        out_shape=[jax.ShapeDtypeStruct((b, t, HG_W), BF16),
                   jax.ShapeDtypeStruct((b, N_HEADS, HEAD, HEAD), F32)],
        scratch_shapes=[pltpu.VMEM((cs, HG_W), F32)] * 4 + [pltpu.VMEM((N_HEADS, HEAD, HEAD), F32)],
        compiler_params=_cparams(("parallel", "arbitrary")),
        name="hgrn2",
    )(*args)


def _lane_blocks(cols, width):
    rows = cols[0].shape[0]
    lane = lax.broadcasted_iota(jnp.int32, (rows, width * len(cols)), 1)
    out = jnp.broadcast_to(cols[-1], (rows, width * len(cols)))
    for j in range(len(cols) - 2, -1, -1):
        out = jnp.where(lane < (j + 1) * width, cols[j], out)
    return out


def _ssd_body(*refs, t_valid, has_s0):
    it = iter(refs)
    xbc_ref = next(it)
    z_ref = next(it)
    dts_ref = next(it)
    cw_ref = next(it)
    cb_ref = next(it)
    dtb_ref = next(it)
    alog_ref = next(it)
    dsk_ref = next(it)
    gn_ref = next(it)
    cs0_ref = next(it) if has_s0 else None
    s0_ref = next(it) if has_s0 else None
    y_ref = next(it)
    st_ref = next(it)
    prev_s = next(it)
    act_s = next(it)
    c = pl.program_id(1)
    cs = SSD_CHUNK
    t_blk = y_ref.shape[1]
    tail0 = 8 - (CONV_W - 1)

    @pl.when(c == 0)
    def _():
        prev_s[...] = jnp.zeros(prev_s.shape, F32)
        if has_s0:
            prev_s[tail0:8, :] = cs0_ref[0].astype(F32)
            st_ref[...] = s0_ref[...].astype(F32)
        else:
            st_ref[...] = jnp.zeros(st_ref.shape, F32)

    cur = _rows_of_chunk(xbc_ref, slice(0, CONV_DIM), cs)
    win = jnp.concatenate([prev_s[...], cur[0:8]], axis=0)
    conv = cb_ref[...] + cur * cw_ref[CONV_W - 1:CONV_W, :]
    head = cb_ref[...] + win[8:16] * cw_ref[CONV_W - 1:CONV_W, :]
    for d in range(1, CONV_W):
        w_d = cw_ref[CONV_W - 1 - d:CONV_W - d, :]
        conv = conv + pltpu.roll(cur, d, axis=0) * w_d
        head = head + pltpu.roll(win, d, axis=0)[8:16] * w_d
    act_s[...] = _silu(conv)
    act_s[0:8, :] = _silu(head)
    prev_s[...] = cur[cs - 8:cs]

    dt = _softplus(_rows_of_chunk(dts_ref, slice(0, 128), cs) + dtb_ref[...])
    if t_valid is not None:
        live = (c * cs + lax.broadcasted_iota(jnp.int32, (cs, 1), 0)) < t_valid
        dt = jnp.where(live, dt, 0.0)
    a = _cumsum_rows(dt * (-jnp.exp(alog_ref[...])))
    a_t = a.T
    dt_t = dt.T
    a_end = a[cs - 1:cs, :]
    w_upd = dt * jnp.exp(a_end - a)
    e_a = jnp.exp(a)
    e_end = jnp.exp(a_end)
    causal = _tri_incl(cs)
    lane_g = lax.broadcasted_iota(jnp.int32, (cs, SSM_GW), 1)
    row_g = lax.broadcasted_iota(jnp.int32, (SSM_GW, 1), 0)

    for g in range(SSM_GROUPS):
        xs = slice(g * SSM_GW, (g + 1) * SSM_GW)
        bsl = slice(SSM_W + g * SSM_N, SSM_W + (g + 1) * SSM_N)
        csl = slice(SSM_W + SSM_GROUPS * SSM_N + g * SSM_N, SSM_W + SSM_GROUPS * SSM_N + (g + 1) * SSM_N)
        xg = act_s[:, xs]
        bm = act_s[:, bsl].astype(BF16)
        cm = act_s[:, csl].astype(BF16)
        cbm = jnp.where(causal, _dot_nt(cm, bm), 0.0)
        s_g = st_ref[0, g]
        lanes = [DT_LANE0 + g * SSM_HPG + j for j in range(SSM_HPG)]
        yg = _dot_nt(cm, s_g.astype(BF16)) * _lane_blocks([e_a[:, ln:ln + 1] for ln in lanes], SSM_P)
        for j, ln in enumerate(lanes):
            diff = a[:, ln:ln + 1] - a_t[ln:ln + 1, :]
            mh = cbm * jnp.exp(jnp.minimum(diff, 0.0)) * dt_t[ln:ln + 1, :]
            xm = jnp.where(jnp.logical_and(lane_g >= j * SSM_P, lane_g < (j + 1) * SSM_P), xg, 0.0)
            yg = yg + _dot(mh.astype(BF16), xm.astype(BF16))
        xw = xg * _lane_blocks([w_upd[:, ln:ln + 1] for ln in lanes], SSM_P)
        decay = jnp.broadcast_to(e_end[:, lanes[-1]:lanes[-1] + 1], (SSM_GW, 1))
        for j in range(SSM_HPG - 2, -1, -1):
            decay = jnp.where(row_g < (j + 1) * SSM_P, e_end[:, lanes[j]:lanes[j] + 1], decay)
        st_ref[0, g] = s_g * decay + _dot_tn(xw.astype(BF16), bm)

        y = yg + dsk_ref[:, xs] * xg
        y = y * _silu(_rows_of_chunk(z_ref, xs, cs))
        ms = jnp.mean(y * y, axis=-1, keepdims=True)
        res = y * lax.rsqrt(ms + EPS) * gn_ref[:, xs]
        y_ref[0, :, xs] = res[0:t_blk].astype(y_ref.dtype)


def _ssd(proj, small, conv_w, conv_b, dt_bias, a_log, d_skip, gnorm, conv_s0, s0):
    b, t, _ = proj.shape
    cs = SSD_CHUNK
    assert t == 1 or t % cs == 0
    t_blk, n_chunks, t_valid = (1, 1, 1) if t == 1 else (cs, t // cs, None)
    has_s0 = s0 is not None
    pad = jnp.zeros((DT_LANE0,), F32)
    tail = jnp.zeros((128 - DT_LANE0 - SSM_HEADS,), F32)
    on_dt_lanes = lambda v: jnp.concatenate([pad, v.astype(F32), tail]).reshape(1, 128)
    const = lambda shape: pl.BlockSpec(shape, lambda i, c: (0,) * len(shape))
    in_specs = [pl.BlockSpec((1, t_blk, CONV_DIM), lambda i, c: (i, c, COL_XBC // CONV_DIM)),
                pl.BlockSpec((1, t_blk, SSM_W), lambda i, c: (i, c, COL_Z // SSM_W)),
                pl.BlockSpec((1, t_blk, 128), lambda i, c: (i, c, 0)),
                const((CONV_W, CONV_DIM)), const((1, CONV_DIM)), const((1, 128)), const((1, 128)),
                const((1, SSM_W)), const((1, SSM_W))]
    args = [proj, proj, small, conv_w.astype(F32), conv_b.reshape(1, CONV_DIM).astype(F32),
            on_dt_lanes(dt_bias), on_dt_lanes(a_log),
            jnp.repeat(d_skip.astype(F32), SSM_P).reshape(1, SSM_W), gnorm.reshape(1, SSM_W).astype(F32)]
    if has_s0:
        in_specs += [pl.BlockSpec((1, CONV_W - 1, CONV_DIM), lambda i, c: (i, 0, 0)),
                     pl.BlockSpec((1, SSM_GROUPS, SSM_GW, SSM_N), lambda i, c: (i, 0, 0, 0))]
        args += [conv_s0, s0.reshape(b, SSM_GROUPS, SSM_GW, SSM_N)]
    y, st = pl.pallas_call(
        functools.partial(_ssd_body, t_valid=t_valid, has_s0=has_s0),
        grid=(b, n_chunks),
        in_specs=in_specs,
        out_specs=[pl.BlockSpec((1, t_blk, SSM_W), lambda i, c: (i, c, 0)),
                   pl.BlockSpec((1, SSM_GROUPS, SSM_GW, SSM_N), lambda i, c: (i, 0, 0, 0))],
        out_shape=[jax.ShapeDtypeStruct((b, t, SSM_W), BF16),
                   jax.ShapeDtypeStruct((b, SSM_GROUPS, SSM_GW, SSM_N), F32)],
        scratch_shapes=[pltpu.VMEM((8, CONV_DIM), F32), pltpu.VMEM((cs, CONV_DIM), F32)],
        compiler_params=_cparams(("parallel", "arbitrary")),
        name="ssd",
    )(*args)
    return y, st.reshape(b, SSM_HEADS, SSM_P, SSM_N)


def _xattn_body(q_ref, k_ref, v_ref, gq_ref, o_ref):
    scale = HEAD ** -0.5
    g = gq_ref[...]
    for h in range(XA_HEADS):
        sl = slice(h * HEAD, (h + 1) * HEAD)
        q = q_ref[0, :, sl].astype(F32)
        ms = jnp.mean(q * q, axis=-1, keepdims=True)
        qn = q * lax.rsqrt(ms + EPS) * g
        s = _dot_nt(qn.astype(BF16), k_ref[0, :, sl].astype(BF16)) * scale
        p = jnp.exp(s - jnp.max(s, axis=-1, keepdims=True))
        o = _dot(p.astype(BF16), v_ref[0, :, sl].astype(BF16)) / jnp.sum(p, axis=-1, keepdims=True)
        o_ref[0, :, sl] = o.astype(o_ref.dtype)


def _xattn(q, mk, mv, gq):
    b, t, _ = q.shape
    n_mem = mk.shape[1]
    tq = t if t <= 512 else 512
    return pl.pallas_call(
        _xattn_body,
        grid=(b, t // tq),
        in_specs=[pl.BlockSpec((1, tq, XA_W), lambda i, j: (i, j, 0)),
                  pl.BlockSpec((1, n_mem, XA_W), lambda i, j: (i, 0, 0)),
                  pl.BlockSpec((1, n_mem, XA_W), lambda i, j: (i, 0, 0)),
                  pl.BlockSpec((1, HEAD), lambda i, j: (0, 0))],
        out_specs=pl.BlockSpec((1, tq, XA_W), lambda i, j: (i, j, 0)),
        out_shape=jax.ShapeDtypeStruct((b, t, XA_W), BF16),
        compiler_params=_cparams(("parallel", "parallel")),
        name="xattn",
    )(q, mk, mv, gq.reshape(1, HEAD).astype(F32))


IN_PIECES = (("fq", FOX_W), ("fk", FOX_W), ("fv", FOX_W), ("fg", N_HEADS), ("hg", 4 * HG_W), ("z", SSM_W),
             ("xbc", CONV_DIM), ("dt", SSM_HEADS))
IN_WIDTH = sum(width for _, width in IN_PIECES)
MAIN_COLS = dict(xbc=COL_XBC, hg=COL_HG, z=COL_Z, fq=COL_FQ, fk=COL_FK, fv=COL_FV)
W_TILE = 1024
SUBLANES = 8


def _in_piece_offsets():
    offs, o = {}, 0
    for name, width in IN_PIECES:
        offs[name] = (o, width)
        o += width
    return offs


def _w_main_tables():
    offs = _in_piece_offsets()
    a_blk, b_blk, shift = [0] * (MAIN_W // W_TILE), [0] * (MAIN_W // W_TILE), [0] * (MAIN_W // W_TILE)
    for name, dst in MAIN_COLS.items():
        src_off, width = offs[name]
        assert dst % W_TILE == 0 and width % W_TILE == 0
        for i in range(width // W_TILE):
            start = src_off + i * W_TILE
            t = dst // W_TILE + i
            a_blk[t], shift[t] = start // W_TILE, start % W_TILE
            assert shift[t] in (0, SUBLANES)
            b_blk[t] = (start - shift[t] + W_TILE) // SUBLANES if shift[t] else 0
    return a_blk, b_blk, shift


def _w_main_body(tbl_ref, a_ref, b_ref, o_ref):
    j = pl.program_id(0)

    @pl.when(tbl_ref[2, j] == 0)
    def _():
        o_ref[...] = a_ref[...].astype(BF16)

    @pl.when(tbl_ref[2, j] != 0)
    def _():
        o_ref[...] = jnp.concatenate([a_ref[SUBLANES:W_TILE, :], b_ref[...]], axis=0).astype(BF16)


def _w_small_body(g_ref, *rest):
    d_refs, o_ref = rest[:-1], rest[-1]
    used = SUBLANES * (1 + len(d_refs))
    pad = jnp.zeros((o_ref.shape[0] - used, o_ref.shape[1]), F32)
    o_ref[...] = jnp.concatenate([g_ref[...]] + [d_ref[...] for d_ref in d_refs] + [pad], axis=0).astype(BF16)


def _prep_w_in(w_in_t, layer):
    depth, width, d = w_in_t.shape
    assert width == IN_WIDTH and N_HEADS == SUBLANES
    offs = _in_piece_offsets()
    a_blk, b_blk, shift = _w_main_tables()
    tc = _pick_tile(d, (1024, 512, 256, 128))
    tables = jnp.asarray([a_blk, b_blk, shift], jnp.int32)
    main = pl.pallas_call(
        _w_main_body,
        grid_spec=pltpu.PrefetchScalarGridSpec(
            num_scalar_prefetch=1, grid=(MAIN_W // W_TILE, d // tc),
            in_specs=[pl.BlockSpec((None, W_TILE, tc), lambda j, c, tbl: (layer, tbl[0, j], c)),
                      pl.BlockSpec((None, SUBLANES, tc), lambda j, c, tbl: (layer, tbl[1, j], c))],
            out_specs=pl.BlockSpec((W_TILE, tc), lambda j, c, tbl: (j, c))),
        out_shape=jax.ShapeDtypeStruct((MAIN_W, d), BF16),
        compiler_params=_cparams(("parallel", "arbitrary")),
        name="w_in_main",
    )(tables, w_in_t, w_in_t)

    g_off, (d_off, d_rows) = offs["fg"][0], offs["dt"]
    assert g_off % SUBLANES == 0 and d_off % SUBLANES == 0 and d_rows % SUBLANES == 0
    rows8 = lambda off: pl.BlockSpec((None, SUBLANES, tc), lambda c: (layer, off // SUBLANES, c))
    n_dt = d_rows // SUBLANES
    small = pl.pallas_call(
        _w_small_body,
        grid=(d // tc,),
        in_specs=[rows8(g_off)] + [rows8(d_off + i * SUBLANES) for i in range(n_dt)],
        out_specs=pl.BlockSpec((128, tc), lambda c: (0, c)),
        out_shape=jax.ShapeDtypeStruct((128, d), BF16),
        compiler_params=_cparams(("parallel",)),
        name="w_in_small",
    )(*([w_in_t] * (1 + n_dt)))
    return main, small


def _mixers(x, lw, l, *, wb, fox_past, hg_s0, ssm_s0, conv_s0, page_table):
    b, t, d = x.shape
    x2 = x.reshape(b * t, d)
    few_rows = wb is None
    proj2 = _matmul(x2, lw["w_main"], gain=lw["norm_mix"], trans_w=True, name="in_proj")
    small2 = _matmul(x2, lw["w_small"], gain=lw["norm_mix"], trans_w=True, name="in_proj_small")
    proj = proj2.reshape(b, t, MAIN_W)
    small = small2.reshape(b, t, 128)

    gates_t = jnp.swapaxes(small[:, :, 0:N_HEADS], 1, 2)
    logf_t, cum_t = _fox_gate(gates_t, lw["fox_bf"])
    logf = jnp.swapaxes(logf_t, 1, 2)

    if fox_past is None:
        qn, kn, knb, fv, vb = _fox_prep(proj2, lw["fox_gq"], lw["fox_gk"])
        fo = _fox_flash(qn.reshape(b, t, FOX_W), knb.reshape(b, t, FOX_W), vb.reshape(b, t, FOX_W),
                        jnp.swapaxes(cum_t, 1, 2), cum_t)
    else:
        qn = _headnorm(proj2, lw["fox_gq"], col0=COL_FQ, width=FOX_W)
        kn = _headnorm(proj2, lw["fox_gk"], col0=COL_FK, width=FOX_W)
        fv = proj2[:, COL_FV:COL_FV + FOX_W]
        cache_k, cache_v, cache_lf = fox_past
        fo = _fox_decode(l, qn.reshape(b, N_HEADS, HEAD), kn.reshape(b, N_HEADS, HEAD),
                         fv.reshape(b, N_HEADS, HEAD), logf_t, cache_k, cache_v, cache_lf, page_table)
        fo = fo.reshape(b, t, FOX_W)

    ho, hg_state = _hgrn(proj, lw["hg_lb_logits"], lw["hg_gnorm"], hg_s0, layer=l)
    sy, ssm_state = _ssd(proj, small, lw["conv_w"], lw["conv_b"], lw["dt_bias"], lw["a_log"],
                         lw["d_skip"], lw["ssm_gnorm"], conv_s0, ssm_s0)
    mixed = (fo.reshape(b * t, FOX_W), ho.reshape(b * t, HG_W), sy.reshape(b * t, SSM_W))
    if few_rows:
        x_new, wb_out = _out_proj(*mixed, lw["w_out"], x2, cast_w=True, layer=l)
    else:
        x_new, wb_out = _out_proj(*mixed, wb["w_out"], x2), None
    x_new = x_new.reshape(b, t, d)

    keep = CONV_W - 1
    if t >= keep:
        conv_state = proj[:, t - keep:, COL_XBC:COL_XBC + CONV_DIM]
    else:
        prev = jnp.zeros((b, keep, CONV_DIM), F32) if conv_s0 is None else conv_s0.astype(F32)
        conv_state = jnp.concatenate([prev[:, t:], proj[:, :, COL_XBC:COL_XBC + CONV_DIM]], axis=1)
    fk = kn.reshape(b, t, N_HEADS, HEAD)
    return x_new, fk, fv.reshape(b, t, N_HEADS, HEAD), logf, hg_state, ssm_state, conv_state, wb_out


def _cross_and_mlp(x, lw, l, mk, mv, *, wb):
    b, t, d = x.shape
    x2 = x.reshape(b * t, d)
    q = _matmul(x2, lw["xa_wq"], gain=lw["norm_xa"], name="xa_q")
    o = _xattn(q.reshape(b, t, XA_W), mk, mv, lw["xa_gq"])
    x2 = _matmul(o.reshape(b * t, XA_W), lw["xa_wo"], res=x2, tm_max=1024, name="xa_out")
    if wb is None:
        u, wb_up = _matmul(x2, lw["w_up"], gain=lw["norm_mlp"], act="relu2", out_dtype=BF16, tn_max=1024,
                           cast_w=True, layer=l, name="mlp_up")
        x2, wb_down = _matmul(u, lw["w_down"], res=x2, tk=2048, cast_w=True, layer=l, name="mlp_down")
        made = dict(w_up=wb_up, w_down=wb_down)
    else:
        u = _matmul(x2, wb["w_up"], gain=lw["norm_mlp"], act="relu2", out_dtype=BF16, name="mlp_up")
        x2 = _matmul(u, wb["w_down"], res=x2, tm_max=1024, tk=2048, name="mlp_down")
        made = None
    return x2.reshape(b, t, d), made


def kernel(x_prompt, x_sample, cache_fox_k, cache_fox_v, cache_fox_logf, cache_mem_k, cache_mem_v, state_hgrn, state_ssm, state_conv, page_table, mem_prompt, norm_mix, w_in, fox_gq, fox_gk, fox_bf, hg_lb_logits, hg_gnorm, conv_w, conv_b, dt_bias, a_log, d_skip, ssm_gnorm, w_out, norm_xa, norm_mem, xa_wq, xa_wk, xa_wv, xa_gq, xa_gk, xa_wo, norm_mlp, w_up, w_down):
    depth = w_in.shape[0]
    bp = x_prompt.shape[0]
    n_mem = mem_prompt.shape[1]
    pool = cache_fox_k.shape[1]
    cache_k = cache_fox_k.reshape(depth, pool, PAGE * N_HEADS, HEAD)
    cache_v = cache_fox_v.reshape(depth, pool, PAGE * N_HEADS, HEAD)
    cache_lf = cache_fox_logf.reshape(depth, pool, 1, PAGE * N_HEADS)
    mem2 = mem_prompt.reshape(bp * n_mem, -1)
    w_in_t = jnp.swapaxes(w_in, 1, 2)

    xp, xs = x_prompt, x_sample
    outs = {k: [] for k in ("p_fk", "p_fv", "p_fl", "p_hg", "p_ss", "p_cv", "p_mk", "p_mv",
                            "s_fk", "s_fv", "s_fl", "s_hg", "s_ss", "s_cv")}
    for l in range(depth):
        w_main, w_small = _prep_w_in(w_in_t, l)
        lw = dict(w_main=w_main, w_small=w_small, norm_mix=norm_mix[l],
                  fox_gq=fox_gq[l], fox_gk=fox_gk[l], fox_bf=fox_bf[l], hg_lb_logits=hg_lb_logits,
                  hg_gnorm=hg_gnorm[l], conv_w=conv_w[l], conv_b=conv_b[l], dt_bias=dt_bias[l], a_log=a_log[l],
                  d_skip=d_skip[l], ssm_gnorm=ssm_gnorm[l], w_out=w_out, norm_xa=norm_xa[l],
                  xa_wq=xa_wq[l].astype(BF16), xa_gq=xa_gq[l], xa_wo=xa_wo[l].astype(BF16),
                  norm_mlp=norm_mlp[l], w_up=w_up, w_down=w_down)

        bs = xs.shape[0]
        xs, fk, fv, fl, hg, ss, cv, wb_out = _mixers(xs, lw, l, wb=None, fox_past=(cache_k, cache_v, cache_lf),
                                                     hg_s0=state_hgrn[l], ssm_s0=state_ssm[l],
                                                     conv_s0=state_conv[l], page_table=page_table)
        xs, wb = _cross_and_mlp(xs, lw, l, cache_mem_k[l].reshape(bs, n_mem, XA_W),
                                cache_mem_v[l].reshape(bs, n_mem, XA_W), wb=None)
        wb["w_out"] = wb_out
        for key, val in zip(("s_fk", "s_fv", "s_fl", "s_hg", "s_ss", "s_cv"), (fk, fv, fl, hg, ss, cv)):
            outs[key].append(val)

        xp, fk, fv, fl, hg, ss, cv, _ = _mixers(xp, lw, l, wb=wb, fox_past=None, hg_s0=None, ssm_s0=None,
                                                conv_s0=None, page_table=None)
        w_kv = jnp.concatenate([xa_wk[l], xa_wv[l]], axis=1).astype(BF16)
        kv = _matmul(mem2, w_kv, gain=norm_mem[l], name="mem_kv")
        mk = _headnorm(kv, xa_gk[l], col0=0, width=XA_W).reshape(bp, n_mem, XA_W)
        mv = kv[:, XA_W:].reshape(bp, n_mem, XA_W)
        xp, _ = _cross_and_mlp(xp, lw, l, mk, mv, wb=wb)
        for key, val in zip(("p_fk", "p_fv", "p_fl", "p_hg", "p_ss", "p_cv"), (fk, fv, fl, hg, ss, cv)):
            outs[key].append(val)
        outs["p_mk"].append(mk.reshape(bp, n_mem, XA_HEADS, HEAD))
        outs["p_mv"].append(mv.reshape(bp, n_mem, XA_HEADS, HEAD))

    st = {k: jnp.stack(v) for k, v in outs.items()}
    return (xp, xs, st["p_fk"], st["p_fv"], st["p_fl"], st["p_hg"], st["p_ss"], st["p_cv"], st["p_mk"],
            st["p_mv"], st["s_fk"], st["s_fv"], st["s_fl"], st["s_hg"], st["s_ss"], st["s_cv"])
```

```python
import functools

import jax
import jax.numpy as jnp
from jax import lax
from jax.experimental import pallas as pl
from jax.experimental.pallas import tpu as pltpu

F32 = jnp.float32
BF16 = jnp.bfloat16

EPS = 1e-6
MASK_VALUE = -1e30
HEAD = 128
N_HEADS = 8
FOX_W = N_HEADS * HEAD
HG_W = N_HEADS * HEAD
SSM_W = 2048
SSM_P = 64
SSM_HEADS = SSM_W // SSM_P
SSM_GROUPS = 8
SSM_HPG = SSM_HEADS // SSM_GROUPS
SSM_N = 128
SSM_GW = SSM_HPG * SSM_P
CONV_W = 4
CONV_DIM = SSM_W + 2 * SSM_GROUPS * SSM_N
XA_HEADS = 4
XA_W = XA_HEADS * HEAD
HG_CHUNK = 64
HG_SUB = 16
SSD_CHUNK = 128
PAGE = 128
DT_LANE0 = N_HEADS

COL_XBC = 0
COL_HG = CONV_DIM
COL_Z = COL_HG + 4 * HG_W
COL_FQ = COL_Z + SSM_W
COL_FK = COL_FQ + FOX_W
COL_FV = COL_FK + FOX_W
MAIN_W = COL_FV + FOX_W

V7X_VMEM_LIMIT = 56 * 1024 * 1024


def _cparams(sem, vmem=V7X_VMEM_LIMIT):
    return pltpu.CompilerParams(dimension_semantics=sem, vmem_limit_bytes=vmem)


def _sigmoid(x):
    return 1.0 / (1.0 + jnp.exp(-x))


def _silu(x):
    h = 0.5 * x
    return h + h * jnp.tanh(h)


def _softplus(x):
    return jnp.maximum(x, 0.0) + jnp.log1p(jnp.exp(-jnp.abs(x)))


def _log_sigmoid(x):
    return -_softplus(-x)


def _dot_nt(a, b):
    return lax.dot_general(a, b, (((1,), (1,)), ((), ())), preferred_element_type=F32)


def _dot_tn(a, b):
    return lax.dot_general(a, b, (((0,), (0,)), ((), ())), preferred_element_type=F32)


def _dot(a, b):
    return jnp.dot(a, b, preferred_element_type=F32)


def _tri_incl(n):
    r = lax.broadcasted_iota(jnp.int32, (n, n), 0)
    c = lax.broadcasted_iota(jnp.int32, (n, n), 1)
    return r >= c


def _cumsum_rows(x):
    n = x.shape[0]
    tri = _tri_incl(n).astype(BF16)
    hi = x.astype(BF16)
    r1 = x - hi.astype(F32)
    mid = r1.astype(BF16)
    lo = (r1 - mid.astype(F32)).astype(BF16)
    return _dot(tri, hi) + _dot(tri, mid) + _dot(tri, lo)


def _side_cast_specs(side_casts, grid):
    n_steps = 1
    for g in grid:
        n_steps *= g

    def flat(*idx):
        step = idx[0]
        for g, i in zip(grid[1:], idx[1:]):
            step = step * g + i
        return step

    in_specs, args, out_specs, out_shapes = [], [], [], []
    for src, layer in side_casts:
        _, k2, n2 = src.shape
        rows = next(r for r in range(16, k2 + 1, 16) if k2 % r == 0 and k2 // r <= n_steps)
        last = k2 // rows - 1
        in_specs.append(pl.BlockSpec((None, rows, n2), lambda *idx, layer=layer, last=last:
                                     (layer, jnp.minimum(flat(*idx), last), 0)))
        args.append(src)
        out_specs.append(pl.BlockSpec((rows, n2), lambda *idx, last=last: (jnp.minimum(flat(*idx), last), 0)))
        out_shapes.append(jax.ShapeDtypeStruct((k2, n2), BF16))
    return in_specs, args, out_specs, out_shapes


def _do_side_casts(side_in, side_out):
    for s_in, s_out in zip(side_in, side_out):
        s_out[...] = s_in[...].astype(BF16)


def _mm_body(*refs, nk, norm, act, has_res, trans_w, n_sides):
    it = iter(refs)
    a_ref = next(it)
    g_ref = next(it) if norm else None
    w_ref = next(it)
    r_ref = next(it) if has_res else None
    side_in = [next(it) for _ in range(n_sides)]
    o_ref = next(it)
    side_out = [next(it) for _ in range(n_sides)]
    h_ref = next(it) if norm else None
    acc_ref = next(it) if nk > 1 else None
    j = pl.program_id(1)
    k = pl.program_id(2)

    if norm:
        @pl.when(j == 0)
        def _():
            x = a_ref[...].astype(F32)
            ms = jnp.mean(x * x, axis=-1, keepdims=True)
            h_ref[...] = (x * lax.rsqrt(ms + EPS) * g_ref[...]).astype(BF16)
        a = h_ref[...]
    else:
        a = a_ref[...]

    _do_side_casts(side_in, side_out)

    if trans_w:
        p = _dot_nt(a, w_ref[...])
    else:
        p = _dot(a, w_ref[...])

    def finish(v):
        if act == "relu2":
            v = jnp.square(jnp.maximum(v, 0.0))
        if has_res:
            v = v + r_ref[...]
        o_ref[...] = v.astype(o_ref.dtype)

    if nk == 1:
        finish(p)
    else:
        @pl.when(k == 0)
        def _():
            acc_ref[...] = p

        @pl.when(k > 0)
        def _():
            acc_ref[...] += p

        @pl.when(k == nk - 1)
        def _():
            finish(acc_ref[...])


def _pick_tile(n, candidates):
    for c in candidates:
        if n % c == 0:
            return c
    return n


def _matmul(a, w, *, gain=None, res=None, act=None, out_dtype=F32, tm_max=512, tn_max=None, tk=None,
            trans_w=False, side_casts=(), name="matmul"):
    m, kdim = a.shape
    n = w.shape[0] if trans_w else w.shape[-1]
    small_m = m <= 64
    tm = m if small_m else _pick_tile(m, tuple(c for c in (1024, 512, 256, 128) if c <= tm_max))
    tn_max = tn_max or (2048 if small_m else 1024)
    tn = _pick_tile(n, tuple(c for c in (2048, 1024, 512, 256, 128) if c <= tn_max))
    tk = kdim if tk is None else tk
    nk = kdim // tk
    norm = gain is not None
    assert not (norm and nk > 1)
    in_specs = [pl.BlockSpec((tm, tk), lambda i, j, k: (i, k))]
    args = [a]
    if norm:
        in_specs.append(pl.BlockSpec((1, kdim), lambda i, j, k: (0, 0)))
        args.append(gain.reshape(1, kdim).astype(F32))
    if trans_w:
        in_specs.append(pl.BlockSpec((tn, tk), lambda i, j, k: (j, k)))
    else:
        in_specs.append(pl.BlockSpec((tk, tn), lambda i, j, k: (k, j)))
    args.append(w)
    if res is not None:
        in_specs.append(pl.BlockSpec((tm, tn), lambda i, j, k: (i, j)))
        args.append(res)
    grid = (m // tm, n // tn, nk)
    side_in_specs, side_args, side_specs, side_shapes = _side_cast_specs(side_casts, grid)
    in_specs += side_in_specs
    args += side_args
    scratch = []
    if norm:
        scratch.append(pltpu.VMEM((tm, kdim), BF16))
    if nk > 1:
        scratch.append(pltpu.VMEM((tm, tn), F32))
    out_specs = [pl.BlockSpec((tm, tn), lambda i, j, k: (i, j))] + side_specs
    out_shape = [jax.ShapeDtypeStruct((m, n), out_dtype)] + side_shapes
    outs = pl.pallas_call(
        functools.partial(_mm_body, nk=nk, norm=norm, act=act, has_res=res is not None, trans_w=trans_w,
                          n_sides=len(side_casts)),
        grid=grid,
        in_specs=in_specs,
        out_specs=out_specs,
        out_shape=out_shape,
        scratch_shapes=scratch,
        compiler_params=_cparams(("arbitrary" if side_casts else "parallel", "arbitrary", "arbitrary")),
        name=name,
    )(*args)
    return outs if side_casts else outs[0]


def _outproj_body(fo_ref, ho_ref, sy_ref, w_ref, r_ref, o_ref):
    k1 = FOX_W + HG_W
    p = (_dot(fo_ref[...], w_ref[0:FOX_W, :]) + _dot(ho_ref[...], w_ref[FOX_W:k1, :])
         + _dot(sy_ref[...], w_ref[k1:k1 + SSM_W, :]))
    o_ref[...] = p + r_ref[...]


def _out_proj(fo, ho, sy, w, res):
    m = fo.shape[0]
    kdim, n = w.shape
    tm = m if m <= 64 else _pick_tile(m, (1024, 512, 256, 128))
    tn = _pick_tile(n, (1024, 512, 256, 128))
    assert fo.shape[1] == FOX_W and ho.shape[1] == HG_W and sy.shape[1] == SSM_W
    return pl.pallas_call(
        _outproj_body,
        grid=(m // tm, n // tn),
        in_specs=[pl.BlockSpec((tm, FOX_W), lambda i, j: (i, 0)),
                  pl.BlockSpec((tm, HG_W), lambda i, j: (i, 0)),
                  pl.BlockSpec((tm, SSM_W), lambda i, j: (i, 0)),
                  pl.BlockSpec((kdim, tn), lambda i, j: (0, j)),
                  pl.BlockSpec((tm, tn), lambda i, j: (i, j))],
        out_specs=pl.BlockSpec((tm, tn), lambda i, j: (i, j)),
        out_shape=jax.ShapeDtypeStruct((m, n), F32),
        compiler_params=_cparams(("parallel", "arbitrary")),
        name="out_proj",
    )(fo, ho, sy, w, res)


def _headnorm_body(x_ref, g_ref, o_ref, *, n_heads):
    g = g_ref[...]
    for h in range(n_heads):
        sl = slice(h * HEAD, (h + 1) * HEAD)
        x = x_ref[:, sl].astype(F32)
        ms = jnp.mean(x * x, axis=-1, keepdims=True)
        o_ref[:, sl] = (x * lax.rsqrt(ms + EPS) * g).astype(o_ref.dtype)


def _headnorm(x, gain, *, col0, width, out_dtype=F32):
    m = x.shape[0]
    tm = m if m <= 64 else _pick_tile(m, (512, 256, 128))
    return pl.pallas_call(
        functools.partial(_headnorm_body, n_heads=width // HEAD),
        grid=(m // tm,),
        in_specs=[pl.BlockSpec((tm, width), lambda i: (i, col0 // width)),
                  pl.BlockSpec((1, HEAD), lambda i: (0, 0))],
        out_specs=pl.BlockSpec((tm, width), lambda i: (i, 0)),
        out_shape=jax.ShapeDtypeStruct((m, width), out_dtype),
        compiler_params=_cparams(("parallel",)),
        name="headnorm",
    )(x, gain.reshape(1, HEAD).astype(F32))


def _fox_gate_body(g_ref, bf_ref, lf_ref, cum_ref, *, t):
    lf = _log_sigmoid(g_ref[0] + bf_ref[...])
    lf_ref[0] = lf
    lane = lax.broadcasted_iota(jnp.int32, lf.shape, 1)
    c = lf
    shift = 1
    while shift < t:
        c = c + jnp.where(lane >= shift, pltpu.roll(c, shift, axis=1), 0.0)
        shift *= 2
    cum_ref[0] = c


def _fox_gate(gates_t, fox_bf):
    b, h, t = gates_t.shape
    spec = pl.BlockSpec((1, h, t), lambda i: (i, 0, 0))
    return pl.pallas_call(
        functools.partial(_fox_gate_body, t=t),
        grid=(b,),
        in_specs=[spec, pl.BlockSpec((h, 1), lambda i: (0, 0))],
        out_specs=[spec, spec],
        out_shape=[jax.ShapeDtypeStruct((b, h, t), F32)] * 2,
        compiler_params=_cparams(("parallel",)),
        name="fox_gate",
    )(gates_t, fox_bf.reshape(h, 1).astype(F32))


FLASH_STRIP = 256


def _fox_flash_body(q_ref, k_ref, v_ref, cq_ref, ck_ref, o_ref, m_ref, l_ref, acc_ref, *, tq):
    qi = pl.program_id(1)
    ki = pl.program_id(2)
    scale = HEAD ** -0.5

    @pl.when(ki == 0)
    def _():
        m_ref[...] = jnp.full(m_ref.shape, MASK_VALUE, F32)
        l_ref[...] = jnp.zeros(l_ref.shape, F32)
        acc_ref[...] = jnp.zeros(acc_ref.shape, F32)

    def update(diagonal):
        for r0 in range(0, tq, FLASH_STRIP):
            rows = slice(r0, r0 + FLASH_STRIP)
            if diagonal:
                keep = (r0 + lax.broadcasted_iota(jnp.int32, (FLASH_STRIP, tq), 0)
                        >= lax.broadcasted_iota(jnp.int32, (FLASH_STRIP, tq), 1))
            for h in range(N_HEADS):
                sl = slice(h * HEAD, (h + 1) * HEAD)
                s = _dot_nt(q_ref[0, rows, sl], k_ref[0, :, sl]) * scale
                s = s + cq_ref[0, rows, h:h + 1] - ck_ref[0, h:h + 1, :]
                if diagonal:
                    s = jnp.where(keep, s, MASK_VALUE)
                m_prev = m_ref[h, rows]
                m_new = jnp.maximum(m_prev, jnp.max(s, axis=-1, keepdims=True))
                alpha = jnp.exp(m_prev - m_new)
                p = jnp.exp(s - m_new)
                l_ref[h, rows] = alpha * l_ref[h, rows] + jnp.sum(p, axis=-1, keepdims=True)
                acc_ref[rows, sl] = alpha * acc_ref[rows, sl] + _dot(p.astype(BF16), v_ref[0, :, sl])
                m_ref[h, rows] = m_new

    @pl.when(ki < qi)
    def _():
        update(diagonal=False)

    @pl.when(ki == qi)
    def _():
        update(diagonal=True)
        for h in range(N_HEADS):
            sl = slice(h * HEAD, (h + 1) * HEAD)
            o_ref[0, :, sl] = (acc_ref[:, sl] / l_ref[h]).astype(o_ref.dtype)


def _fox_prep_body(q_ref, k_ref, v_ref, gq_ref, gk_ref, qn_ref, kn_ref, knb_ref, vf_ref, vb_ref):
    for h in range(N_HEADS):
        sl = slice(h * HEAD, (h + 1) * HEAD)
        q = q_ref[:, sl]
        qn_ref[:, sl] = (q * lax.rsqrt(jnp.mean(q * q, axis=-1, keepdims=True) + EPS) * gq_ref[...]).astype(BF16)
        k = k_ref[:, sl]
        kn = k * lax.rsqrt(jnp.mean(k * k, axis=-1, keepdims=True) + EPS) * gk_ref[...]
        kn_ref[:, sl] = kn
        knb_ref[:, sl] = kn.astype(BF16)
    v = v_ref[...]
    vf_ref[...] = v
    vb_ref[...] = v.astype(BF16)


def _fox_prep(proj2, gq, gk):
    m = proj2.shape[0]
    tm = _pick_tile(m, (512, 256, 128))
    col = lambda c: pl.BlockSpec((tm, FOX_W), lambda i: (i, c // FOX_W))
    out = pl.BlockSpec((tm, FOX_W), lambda i: (i, 0))
    gain = pl.BlockSpec((1, HEAD), lambda i: (0, 0))
    return pl.pallas_call(
        _fox_prep_body,
        grid=(m // tm,),
        in_specs=[col(COL_FQ), col(COL_FK), col(COL_FV), gain, gain],
        out_specs=[out] * 5,
        out_shape=[jax.ShapeDtypeStruct((m, FOX_W), dt) for dt in (BF16, F32, BF16, F32, BF16)],
        compiler_params=_cparams(("parallel",)),
        name="fox_prep",
    )(proj2, proj2, proj2, gq.reshape(1, HEAD).astype(F32), gk.reshape(1, HEAD).astype(F32))


def _fox_flash(qn, kn, vb, cum_col, cum_row, *, tq=512):
    b, t, _ = qn.shape
    nq = t // tq
    return pl.pallas_call(
        functools.partial(_fox_flash_body, tq=tq),
        grid=(b, nq, nq),
        in_specs=[
            pl.BlockSpec((1, tq, FOX_W), lambda i, q, k: (i, q, 0)),
            pl.BlockSpec((1, tq, FOX_W), lambda i, q, k: (i, jnp.minimum(k, q), 0)),
            pl.BlockSpec((1, tq, FOX_W), lambda i, q, k: (i, jnp.minimum(k, q), 0)),
            pl.BlockSpec((1, tq, N_HEADS), lambda i, q, k: (i, q, 0)),
            pl.BlockSpec((1, N_HEADS, tq), lambda i, q, k: (i, 0, jnp.minimum(k, q))),
        ],
        out_specs=pl.BlockSpec((1, tq, FOX_W), lambda i, q, k: (i, q, 0)),
        out_shape=jax.ShapeDtypeStruct((b, t, FOX_W), BF16),
        scratch_shapes=[pltpu.VMEM((N_HEADS, tq, 1), F32), pltpu.VMEM((N_HEADS, tq, 1), F32),
                        pltpu.VMEM((tq, FOX_W), F32)],
        compiler_params=_cparams(("parallel", "parallel", "arbitrary")),
        name="fox_flash",
    )(qn, kn, vb, cum_col, cum_row)


def _fox_decode_body(pt_ref, q_ref, kn_ref, vn_ref, lfn_ref, *rest, pages_per_step, n_steps):
    del pt_ref
    pps = pages_per_step
    k_refs = rest[:pps]
    v_refs = rest[pps:2 * pps]
    lf_refs = rest[2 * pps:3 * pps]
    o_ref, m_ref, l_ref, acc_ref, carry_ref = rest[3 * pps:]
    s = pl.program_id(1)
    scale = HEAD ** -0.5
    rows = PAGE * N_HEADS
    lane = lax.broadcasted_iota(jnp.int32, (1, rows), 1)
    own = (lax.broadcasted_iota(jnp.int32, (N_HEADS, rows), 1) % N_HEADS
           == lax.broadcasted_iota(jnp.int32, (N_HEADS, rows), 0))

    @pl.when(s == 0)
    def _():
        m_ref[...] = jnp.sum(q_ref[0] * kn_ref[0], axis=-1, keepdims=True) * scale
        l_ref[...] = jnp.ones(l_ref.shape, F32)
        acc_ref[...] = vn_ref[0]
        carry_ref[...] = jnp.zeros(carry_ref.shape, F32)

    lf = jnp.concatenate([lf_refs[r][0, 0] for r in range(pps)], axis=0)
    tot = lf
    suf = lf
    shift = N_HEADS
    while shift < rows:
        tot = tot + pltpu.roll(tot, shift, axis=1)
        suf = suf + jnp.where(lane + shift < rows, pltpu.roll(suf, rows - shift, axis=1), 0.0)
        shift *= 2
    page = lax.broadcasted_iota(jnp.int32, (pps, 1), 0)
    newer = tot
    shift = 1
    while shift < pps:
        newer = newer + jnp.where(page >= shift, pltpu.roll(newer, shift, axis=0), 0.0)
        shift *= 2
    after = suf - lf + (newer - tot) + carry_ref[...]
    carry_ref[...] = carry_ref[...] + newer[pps - 1:pps, :]

    q = q_ref[0].astype(BF16)
    scs = []
    for r in range(pps):
        sc = _dot_nt(q, k_refs[r][0, 0].astype(BF16)) * scale + lfn_ref[0] + after[r:r + 1, :]
        scs.append(jnp.where(own, sc, MASK_VALUE))
    m_prev = m_ref[...]
    m_new = m_prev
    for sc in scs:
        m_new = jnp.maximum(m_new, jnp.max(sc, axis=-1, keepdims=True))
    alpha = jnp.exp(m_prev - m_new)
    l_new = alpha * l_ref[...]
    acc = alpha * acc_ref[...]
    for r, sc in enumerate(scs):
        p = jnp.exp(sc - m_new)
        l_new = l_new + jnp.sum(p, axis=-1, keepdims=True)
        acc = acc + _dot(p.astype(BF16), v_refs[r][0, 0].astype(BF16))
    l_ref[...] = l_new
    acc_ref[...] = acc
    m_ref[...] = m_new

    @pl.when(s == n_steps - 1)
    def _():
        o_ref[0] = (acc_ref[...] / l_ref[...]).astype(o_ref.dtype)


def _fox_decode(layer, qn, kn, v_new, lf_new, cache_k, cache_v, cache_lf, page_table, *, pages_per_step=8):
    b = qn.shape[0]
    n_pages = page_table.shape[1]
    pps = pages_per_step
    n_steps = n_pages // pps
    rows = PAGE * N_HEADS

    def page_map(r):
        return lambda i, s, pt: (layer, pt[i, n_pages - 1 - (s * pps + r)], 0, 0)

    head = pl.BlockSpec((1, N_HEADS, HEAD), lambda i, s, pt: (i, 0, 0))
    in_specs = [head, head, head, pl.BlockSpec((1, N_HEADS, 1), lambda i, s, pt: (i, 0, 0))]
    in_specs += [pl.BlockSpec((1, 1, rows, HEAD), page_map(r)) for r in range(pps)]
    in_specs += [pl.BlockSpec((1, 1, rows, HEAD), page_map(r)) for r in range(pps)]
    in_specs += [pl.BlockSpec((1, 1, 1, rows), page_map(r)) for r in range(pps)]
    grid_spec = pltpu.PrefetchScalarGridSpec(
        num_scalar_prefetch=1, grid=(b, n_steps), in_specs=in_specs,
        out_specs=pl.BlockSpec((1, N_HEADS, HEAD), lambda i, s, pt: (i, 0, 0)),
        scratch_shapes=[pltpu.VMEM((N_HEADS, 1), F32), pltpu.VMEM((N_HEADS, 1), F32),
                        pltpu.VMEM((N_HEADS, HEAD), F32), pltpu.VMEM((1, rows), F32)])
    return pl.pallas_call(
        functools.partial(_fox_decode_body, pages_per_step=pps, n_steps=n_steps),
        grid_spec=grid_spec,
        out_shape=jax.ShapeDtypeStruct((b, N_HEADS, HEAD), BF16),
        compiler_params=_cparams(("parallel", "arbitrary")),
        name="fox_decode",
    )(page_table, qn, kn, v_new, lf_new, *([cache_k] * pps), *([cache_v] * pps), *([cache_lf] * pps))


def _rows_of_chunk(ref, cols, chunk_rows):
    x = ref[0, :, cols].astype(F32)
    if x.shape[0] == chunk_rows:
        return x
    assert x.shape[0] == 1
    return jnp.broadcast_to(x, (chunk_rows, x.shape[1]))


def _hgrn_body(*refs, layer, t_valid, has_s0):
    it = iter(refs)
    x_ref = next(it)
    lbl_ref = next(it)
    gn_ref = next(it)
    s0_ref = next(it) if has_s0 else None
    o_ref = next(it)
    st_ref = next(it)
    q_s, k_s, v_s, b_s, stt_s = (next(it) for _ in range(5))
    c = pl.program_id(1)
    nc = pl.num_programs(1)
    cs = HG_CHUNK
    t_blk = o_ref.shape[1]

    @pl.when(c == 0)
    def _():
        for h in range(N_HEADS):
            if has_s0:
                stt_s[h] = s0_ref[0, h].astype(F32).T
            else:
                stt_s[h] = jnp.zeros((HEAD, HEAD), F32)

    lg = lbl_ref[...].astype(F32)
    e = jnp.exp(lg - jnp.max(lg, axis=0, keepdims=True))
    pr = e / jnp.sum(e, axis=0, keepdims=True)
    lb = jnp.sum(pr[0:layer + 1], axis=0, keepdims=True) - pr[0:1]

    hf = _rows_of_chunk(x_ref, slice(HG_W, 2 * HG_W), cs)
    logf = jnp.log(lb + (1.0 - lb) * _sigmoid(hf))
    kk = (1.0 - lb) * _sigmoid(-hf)
    qq = _silu(_rows_of_chunk(x_ref, slice(0, HG_W), cs))
    vv = _rows_of_chunk(x_ref, slice(2 * HG_W, 3 * HG_W), cs)
    if t_valid is not None:
        live = (c * cs + lax.broadcasted_iota(jnp.int32, (cs, 1), 0)) < t_valid
        logf = jnp.where(live, logf, 0.0)
        kk = jnp.where(live, kk, 0.0)
        qq = jnp.where(live, qq, 0.0)
        vv = jnp.where(live, vv, 0.0)
    q_s[...] = qq
    k_s[...] = kk
    v_s[...] = vv
    b_s[...] = _cumsum_rows(logf)

    n_sub = cs // HG_SUB
    causal = _tri_incl(cs)
    zeros_sub = jnp.zeros((HG_SUB, HEAD), F32)
    g = gn_ref[...]

    atts = []
    for h in range(N_HEADS):
        sl = slice(h * HEAD, (h + 1) * HEAD)
        q_parts, k_parts = [], []
        for i in range(n_sub):
            r0 = i * HG_SUB
            n_keys = r0 + HG_SUB
            b_i = b_s[r0 - 1:r0, sl] if i > 0 else jnp.zeros((1, HEAD), F32)
            qd = q_s[r0:n_keys, sl] * jnp.exp(b_s[r0:n_keys, sl] - b_i)
            kd = k_s[0:n_keys, sl] * jnp.exp(b_i - b_s[0:n_keys, sl])
            q_parts.append(jnp.concatenate([zeros_sub] * i + [qd] + [zeros_sub] * (n_sub - 1 - i), axis=0))
            k_parts.append(jnp.concatenate([kd] + [zeros_sub] * (n_sub - 1 - i), axis=0))
        q_cat = jnp.concatenate(q_parts, axis=1).astype(BF16)
        k_cat = jnp.concatenate(k_parts, axis=1).astype(BF16)
        atts.append(jnp.where(causal, _dot_nt(q_cat, k_cat), 0.0).astype(BF16))

    for h in range(N_HEADS):
        sl = slice(h * HEAD, (h + 1) * HEAD)
        stt = stt_s[h]
        bh = b_s[:, sl]
        b_end = b_s[cs - 1:cs, sl]
        o = _dot(atts[h], v_s[:, sl].astype(BF16))
        o = o + _dot_nt((q_s[:, sl] * jnp.exp(bh)).astype(BF16), stt.astype(BF16))
        kd_end = k_s[:, sl] * jnp.exp(b_end - bh)
        stt_s[h] = stt * jnp.exp(b_end) + _dot_tn(v_s[:, sl].astype(BF16), kd_end.astype(BF16))
        ms = jnp.mean(o * o, axis=-1, keepdims=True)
        gate = _silu(_rows_of_chunk(x_ref, slice(3 * HG_W + h * HEAD, 3 * HG_W + (h + 1) * HEAD), cs))
        res = o * lax.rsqrt(ms + EPS) * g * gate
        o_ref[0, :, sl] = res[0:t_blk].astype(o_ref.dtype)

    @pl.when(c == nc - 1)
    def _():
        for h in range(N_HEADS):
            st_ref[0, h] = stt_s[h].T


def _hgrn(proj, lb_logits, gnorm, s0, *, layer):
    b, t, _ = proj.shape
    cs = HG_CHUNK
    assert t == 1 or t % cs == 0
    t_blk, n_chunks, t_valid = (1, 1, 1) if t == 1 else (cs, t // cs, None)
    has_s0 = s0 is not None
    depth = lb_logits.shape[0]
    in_specs = [pl.BlockSpec((1, t_blk, 4 * HG_W), lambda i, c: (i, c, COL_HG // (4 * HG_W))),
                pl.BlockSpec((depth, HG_W), lambda i, c: (0, 0)),
                pl.BlockSpec((1, HEAD), lambda i, c: (0, 0))]
    args = [proj, lb_logits.astype(F32), gnorm.reshape(1, HEAD).astype(F32)]
    if has_s0:
        in_specs.append(pl.BlockSpec((1, N_HEADS, HEAD, HEAD), lambda i, c: (i, 0, 0, 0)))
        args.append(s0)
    return pl.pallas_call(
        functools.partial(_hgrn_body, layer=layer, t_valid=t_valid, has_s0=has_s0),
        grid=(b, n_chunks),
        in_specs=in_specs,
        out_specs=[pl.BlockSpec((1, t_blk, HG_W), lambda i, c: (i, c, 0)),
                   pl.BlockSpec((1, N_HEADS, HEAD, HEAD), lambda i, c: (i, 0, 0, 0))],
        out_shape=[jax.ShapeDtypeStruct((b, t, HG_W), BF16),
                   jax.ShapeDtypeStruct((b, N_HEADS, HEAD, HEAD), F32)],
        scratch_shapes=[pltpu.VMEM((cs, HG_W), F32)] * 4 + [pltpu.VMEM((N_HEADS, HEAD, HEAD), F32)],
        compiler_params=_cparams(("parallel", "arbitrary")),
        name="hgrn2",
    )(*args)


def _lane_blocks(cols, width):
    rows = cols[0].shape[0]
    lane = lax.broadcasted_iota(jnp.int32, (rows, width * len(cols)), 1)
    out = jnp.broadcast_to(cols[-1], (rows, width * len(cols)))
    for j in range(len(cols) - 2, -1, -1):
        out = jnp.where(lane < (j + 1) * width, cols[j], out)
    return out


def _ssd_body(*refs, t_valid, has_s0, n_sides):
    it = iter(refs)
    xbc_ref = next(it)
    z_ref = next(it)
    dts_ref = next(it)
    cw_ref = next(it)
    cb_ref = next(it)
    dtb_ref = next(it)
    alog_ref = next(it)
    dsk_ref = next(it)
    gn_ref = next(it)
    cs0_ref = next(it) if has_s0 else None
    s0_ref = next(it) if has_s0 else None
    side_in = [next(it) for _ in range(n_sides)]
    y_ref = next(it)
    st_ref = next(it)
    side_out = [next(it) for _ in range(n_sides)]
    prev_s = next(it)
    act_s = next(it)
    c = pl.program_id(1)
    cs = SSD_CHUNK
    t_blk = y_ref.shape[1]
    tail0 = 8 - (CONV_W - 1)

    @pl.when(c == 0)
    def _():
        prev_s[...] = jnp.zeros(prev_s.shape, F32)
        if has_s0:
            prev_s[tail0:8, :] = cs0_ref[0].astype(F32)
            st_ref[...] = s0_ref[...].astype(F32)
        else:
            st_ref[...] = jnp.zeros(st_ref.shape, F32)

    _do_side_casts(side_in, side_out)

    cur = _rows_of_chunk(xbc_ref, slice(0, CONV_DIM), cs)
    win = jnp.concatenate([prev_s[...], cur[0:8]], axis=0)
    conv = cb_ref[...] + cur * cw_ref[CONV_W - 1:CONV_W, :]
    head = cb_ref[...] + win[8:16] * cw_ref[CONV_W - 1:CONV_W, :]
    for d in range(1, CONV_W):
        w_d = cw_ref[CONV_W - 1 - d:CONV_W - d, :]
        conv = conv + pltpu.roll(cur, d, axis=0) * w_d
        head = head + pltpu.roll(win, d, axis=0)[8:16] * w_d
    act_s[...] = _silu(conv)
    act_s[0:8, :] = _silu(head)
    prev_s[...] = cur[cs - 8:cs]

    dt = _softplus(_rows_of_chunk(dts_ref, slice(0, 128), cs) + dtb_ref[...])
    if t_valid is not None:
        live = (c * cs + lax.broadcasted_iota(jnp.int32, (cs, 1), 0)) < t_valid
        dt = jnp.where(live, dt, 0.0)
    a = _cumsum_rows(dt * (-jnp.exp(alog_ref[...])))
    a_t = a.T
    dt_t = dt.T
    a_end = a[cs - 1:cs, :]
    w_upd = dt * jnp.exp(a_end - a)
    e_a = jnp.exp(a)
    e_end = jnp.exp(a_end)
    causal = _tri_incl(cs)
    lane_g = lax.broadcasted_iota(jnp.int32, (cs, SSM_GW), 1)
    row_g = lax.broadcasted_iota(jnp.int32, (SSM_GW, 1), 0)

    for g in range(SSM_GROUPS):
        xs = slice(g * SSM_GW, (g + 1) * SSM_GW)
        bsl = slice(SSM_W + g * SSM_N, SSM_W + (g + 1) * SSM_N)
        csl = slice(SSM_W + SSM_GROUPS * SSM_N + g * SSM_N, SSM_W + SSM_GROUPS * SSM_N + (g + 1) * SSM_N)
        xg = act_s[:, xs]
        bm = act_s[:, bsl].astype(BF16)
        cm = act_s[:, csl].astype(BF16)
        cbm = jnp.where(causal, _dot_nt(cm, bm), 0.0)
        s_g = st_ref[0, g]
        lanes = [DT_LANE0 + g * SSM_HPG + j for j in range(SSM_HPG)]
        yg = _dot_nt(cm, s_g.astype(BF16)) * _lane_blocks([e_a[:, ln:ln + 1] for ln in lanes], SSM_P)
        for j, ln in enumerate(lanes):
            diff = a[:, ln:ln + 1] - a_t[ln:ln + 1, :]
            mh = cbm * jnp.exp(jnp.minimum(diff, 0.0)) * dt_t[ln:ln + 1, :]
            xm = jnp.where(jnp.logical_and(lane_g >= j * SSM_P, lane_g < (j + 1) * SSM_P), xg, 0.0)
            yg = yg + _dot(mh.astype(BF16), xm.astype(BF16))
        xw = xg * _lane_blocks([w_upd[:, ln:ln + 1] for ln in lanes], SSM_P)
        decay = jnp.broadcast_to(e_end[:, lanes[-1]:lanes[-1] + 1], (SSM_GW, 1))
        for j in range(SSM_HPG - 2, -1, -1):
            decay = jnp.where(row_g < (j + 1) * SSM_P, e_end[:, lanes[j]:lanes[j] + 1], decay)
        st_ref[0, g] = s_g * decay + _dot_tn(xw.astype(BF16), bm)

        y = yg + dsk_ref[:, xs] * xg
        y = y * _silu(_rows_of_chunk(z_ref, xs, cs))
        ms = jnp.mean(y * y, axis=-1, keepdims=True)
        res = y * lax.rsqrt(ms + EPS) * gn_ref[:, xs]
        y_ref[0, :, xs] = res[0:t_blk].astype(y_ref.dtype)


def _ssd(proj, small, conv_w, conv_b, dt_bias, a_log, d_skip, gnorm, conv_s0, s0, *, side_casts=()):
    b, t, _ = proj.shape
    cs = SSD_CHUNK
    assert t == 1 or t % cs == 0
    t_blk, n_chunks, t_valid = (1, 1, 1) if t == 1 else (cs, t // cs, None)
    has_s0 = s0 is not None
    pad = jnp.zeros((DT_LANE0,), F32)
    tail = jnp.zeros((128 - DT_LANE0 - SSM_HEADS,), F32)
    on_dt_lanes = lambda v: jnp.concatenate([pad, v.astype(F32), tail]).reshape(1, 128)
    const = lambda shape: pl.BlockSpec(shape, lambda i, c: (0,) * len(shape))
    in_specs = [pl.BlockSpec((1, t_blk, CONV_DIM), lambda i, c: (i, c, COL_XBC // CONV_DIM)),
                pl.BlockSpec((1, t_blk, SSM_W), lambda i, c: (i, c, COL_Z // SSM_W)),
                pl.BlockSpec((1, t_blk, 128), lambda i, c: (i, c, 0)),
                const((CONV_W, CONV_DIM)), const((1, CONV_DIM)), const((1, 128)), const((1, 128)),
                const((1, SSM_W)), const((1, SSM_W))]
    args = [proj, proj, small, conv_w.astype(F32), conv_b.reshape(1, CONV_DIM).astype(F32),
            on_dt_lanes(dt_bias), on_dt_lanes(a_log),
            jnp.repeat(d_skip.astype(F32), SSM_P).reshape(1, SSM_W), gnorm.reshape(1, SSM_W).astype(F32)]
    if has_s0:
        in_specs += [pl.BlockSpec((1, CONV_W - 1, CONV_DIM), lambda i, c: (i, 0, 0)),
                     pl.BlockSpec((1, SSM_GROUPS, SSM_GW, SSM_N), lambda i, c: (i, 0, 0, 0))]
        args += [conv_s0, s0.reshape(b, SSM_GROUPS, SSM_GW, SSM_N)]
    grid = (b, n_chunks)
    side_in_specs, side_args, side_specs, side_shapes = _side_cast_specs(side_casts, grid)
    y, st, *copies = pl.pallas_call(
        functools.partial(_ssd_body, t_valid=t_valid, has_s0=has_s0, n_sides=len(side_casts)),
        grid=grid,
        in_specs=in_specs + side_in_specs,
        out_specs=[pl.BlockSpec((1, t_blk, SSM_W), lambda i, c: (i, c, 0)),
                   pl.BlockSpec((1, SSM_GROUPS, SSM_GW, SSM_N), lambda i, c: (i, 0, 0, 0))] + side_specs,
        out_shape=[jax.ShapeDtypeStruct((b, t, SSM_W), BF16),
                   jax.ShapeDtypeStruct((b, SSM_GROUPS, SSM_GW, SSM_N), F32)] + side_shapes,
        scratch_shapes=[pltpu.VMEM((8, CONV_DIM), F32), pltpu.VMEM((cs, CONV_DIM), F32)],
        compiler_params=_cparams(("arbitrary" if side_casts else "parallel", "arbitrary")),
        name="ssd",
    )(*(args + side_args))
    return (y, st.reshape(b, SSM_HEADS, SSM_P, SSM_N), *copies)


def _xattn_body(q_ref, k_ref, v_ref, gq_ref, o_ref):
    scale = HEAD ** -0.5
    g = gq_ref[...]
    for h in range(XA_HEADS):
        sl = slice(h * HEAD, (h + 1) * HEAD)
        q = q_ref[0, :, sl].astype(F32)
        ms = jnp.mean(q * q, axis=-1, keepdims=True)
        qn = q * lax.rsqrt(ms + EPS) * g
        s = _dot_nt(qn.astype(BF16), k_ref[0, :, sl].astype(BF16)) * scale
        p = jnp.exp(s - jnp.max(s, axis=-1, keepdims=True))
        o = _dot(p.astype(BF16), v_ref[0, :, sl].astype(BF16)) / jnp.sum(p, axis=-1, keepdims=True)
        o_ref[0, :, sl] = o.astype(o_ref.dtype)


def _xattn(q, mk, mv, gq):
    b, t, _ = q.shape
    n_mem = mk.shape[1]
    tq = t if t <= 512 else 512
    return pl.pallas_call(
        _xattn_body,
        grid=(b, t // tq),
        in_specs=[pl.BlockSpec((1, tq, XA_W), lambda i, j: (i, j, 0)),
                  pl.BlockSpec((1, n_mem, XA_W), lambda i, j: (i, 0, 0)),
                  pl.BlockSpec((1, n_mem, XA_W), lambda i, j: (i, 0, 0)),
                  pl.BlockSpec((1, HEAD), lambda i, j: (0, 0))],
        out_specs=pl.BlockSpec((1, tq, XA_W), lambda i, j: (i, j, 0)),
        out_shape=jax.ShapeDtypeStruct((b, t, XA_W), BF16),
        compiler_params=_cparams(("parallel", "parallel")),
        name="xattn",
    )(q, mk, mv, gq.reshape(1, HEAD).astype(F32))


IN_PIECES = (("fq", FOX_W), ("fk", FOX_W), ("fv", FOX_W), ("fg", N_HEADS), ("hg", 4 * HG_W), ("z", SSM_W),
             ("xbc", CONV_DIM), ("dt", SSM_HEADS))
IN_WIDTH = sum(width for _, width in IN_PIECES)
MAIN_COLS = dict(xbc=COL_XBC, hg=COL_HG, z=COL_Z, fq=COL_FQ, fk=COL_FK, fv=COL_FV)
W_TILE = 1024
SUBLANES = 8


def _in_piece_offsets():
    offs, o = {}, 0
    for name, width in IN_PIECES:
        offs[name] = (o, width)
        o += width
    return offs


def _w_main_tables():
    offs = _in_piece_offsets()
    a_blk, b_blk, shift = [0] * (MAIN_W // W_TILE), [0] * (MAIN_W // W_TILE), [0] * (MAIN_W // W_TILE)
    for name, dst in MAIN_COLS.items():
        src_off, width = offs[name]
        assert dst % W_TILE == 0 and width % W_TILE == 0
        for i in range(width // W_TILE):
            start = src_off + i * W_TILE
            t = dst // W_TILE + i
            a_blk[t], shift[t] = start // W_TILE, start % W_TILE
            assert shift[t] in (0, SUBLANES)
            b_blk[t] = (start - shift[t] + W_TILE) // SUBLANES if shift[t] else 0
    return a_blk, b_blk, shift


def _w_main_body(tbl_ref, a_ref, b_ref, o_ref):
    j = pl.program_id(0)

    @pl.when(tbl_ref[2, j] == 0)
    def _():
        o_ref[...] = a_ref[...].astype(BF16)

    @pl.when(tbl_ref[2, j] != 0)
    def _():
        o_ref[...] = jnp.concatenate([a_ref[SUBLANES:W_TILE, :], b_ref[...]], axis=0).astype(BF16)


def _w_small_body(g_ref, *rest):
    d_refs, o_ref = rest[:-1], rest[-1]
    used = SUBLANES * (1 + len(d_refs))
    pad = jnp.zeros((o_ref.shape[0] - used, o_ref.shape[1]), F32)
    o_ref[...] = jnp.concatenate([g_ref[...]] + [d_ref[...] for d_ref in d_refs] + [pad], axis=0).astype(BF16)


def _prep_w_in(w_in_t, layer):
    depth, width, d = w_in_t.shape
    assert width == IN_WIDTH and N_HEADS == SUBLANES
    offs = _in_piece_offsets()
    a_blk, b_blk, shift = _w_main_tables()
    tc = _pick_tile(d, (1024, 512, 256, 128))
    tables = jnp.asarray([a_blk, b_blk, shift], jnp.int32)
    main = pl.pallas_call(
        _w_main_body,
        grid_spec=pltpu.PrefetchScalarGridSpec(
            num_scalar_prefetch=1, grid=(MAIN_W // W_TILE, d // tc),
            in_specs=[pl.BlockSpec((None, W_TILE, tc), lambda j, c, tbl: (layer, tbl[0, j], c)),
                      pl.BlockSpec((None, SUBLANES, tc), lambda j, c, tbl: (layer, tbl[1, j], c))],
            out_specs=pl.BlockSpec((W_TILE, tc), lambda j, c, tbl: (j, c))),
        out_shape=jax.ShapeDtypeStruct((MAIN_W, d), BF16),
        compiler_params=_cparams(("parallel", "arbitrary")),
        name="w_in_main",
    )(tables, w_in_t, w_in_t)

    g_off, (d_off, d_rows) = offs["fg"][0], offs["dt"]
    assert g_off % SUBLANES == 0 and d_off % SUBLANES == 0 and d_rows % SUBLANES == 0
    rows8 = lambda off: pl.BlockSpec((None, SUBLANES, tc), lambda c: (layer, off // SUBLANES, c))
    n_dt = d_rows // SUBLANES
    small = pl.pallas_call(
        _w_small_body,
        grid=(d // tc,),
        in_specs=[rows8(g_off)] + [rows8(d_off + i * SUBLANES) for i in range(n_dt)],
        out_specs=pl.BlockSpec((128, tc), lambda c: (0, c)),
        out_shape=jax.ShapeDtypeStruct((128, d), BF16),
        compiler_params=_cparams(("parallel",)),
        name="w_in_small",
    )(*([w_in_t] * (1 + n_dt)))
    return main, small


def _mixers(x, lw, l, *, wb, fox_past, hg_s0, ssm_s0, conv_s0, page_table):
    b, t, d = x.shape
    x2 = x.reshape(b * t, d)
    proj2 = _matmul(x2, lw["w_main"], gain=lw["norm_mix"], trans_w=True, name="in_proj")
    small2 = _matmul(x2, lw["w_small"], gain=lw["norm_mix"], trans_w=True, name="in_proj_small")
    proj = proj2.reshape(b, t, MAIN_W)
    small = small2.reshape(b, t, 128)

    gates_t = jnp.swapaxes(small[:, :, 0:N_HEADS], 1, 2)
    logf_t, cum_t = _fox_gate(gates_t, lw["fox_bf"])
    logf = jnp.swapaxes(logf_t, 1, 2)

    if fox_past is None:
        qn, kn, knb, fv, vb = _fox_prep(proj2, lw["fox_gq"], lw["fox_gk"])
        fo = _fox_flash(qn.reshape(b, t, FOX_W), knb.reshape(b, t, FOX_W), vb.reshape(b, t, FOX_W),
                        jnp.swapaxes(cum_t, 1, 2), cum_t)
    else:
        qn = _headnorm(proj2, lw["fox_gq"], col0=COL_FQ, width=FOX_W)
        kn = _headnorm(proj2, lw["fox_gk"], col0=COL_FK, width=FOX_W)
        fv = proj2[:, COL_FV:COL_FV + FOX_W]
        cache_k, cache_v, cache_lf = fox_past
        fo = _fox_decode(l, qn.reshape(b, N_HEADS, HEAD), kn.reshape(b, N_HEADS, HEAD),
                         fv.reshape(b, N_HEADS, HEAD), logf_t, cache_k, cache_v, cache_lf, page_table)
        fo = fo.reshape(b, t, FOX_W)

    ho, hg_state = _hgrn(proj, lw["hg_lb_logits"], lw["hg_gnorm"], hg_s0, layer=l)
    ssd_args = (proj, small, lw["conv_w"], lw["conv_b"], lw["dt_bias"], lw["a_log"], lw["d_skip"],
                lw["ssm_gnorm"], conv_s0, ssm_s0)
    if "w_out" in wb:
        sy, ssm_state = _ssd(*ssd_args)
    else:
        sy, ssm_state, wb["w_out"], wb["w_up"] = _ssd(*ssd_args, side_casts=((lw["w_out"], l), (lw["w_up"], l)))
    x_new = _out_proj(fo.reshape(b * t, FOX_W), ho.reshape(b * t, HG_W), sy.reshape(b * t, SSM_W),
                      wb["w_out"], x2).reshape(b, t, d)

    keep = CONV_W - 1
    if t >= keep:
        conv_state = proj[:, t - keep:, COL_XBC:COL_XBC + CONV_DIM]
    else:
        prev = jnp.zeros((b, keep, CONV_DIM), F32) if conv_s0 is None else conv_s0.astype(F32)
        conv_state = jnp.concatenate([prev[:, t:], proj[:, :, COL_XBC:COL_XBC + CONV_DIM]], axis=1)
    fk = kn.reshape(b, t, N_HEADS, HEAD)
    return x_new, fk, fv.reshape(b, t, N_HEADS, HEAD), logf, hg_state, ssm_state, conv_state


def _cross_and_mlp(x, lw, l, mk, mv, *, wb):
    b, t, d = x.shape
    x2 = x.reshape(b * t, d)
    q = _matmul(x2, lw["xa_wq"], gain=lw["norm_xa"], name="xa_q")
    o = _xattn(q.reshape(b, t, XA_W), mk, mv, lw["xa_gq"])
    x2 = _matmul(o.reshape(b * t, XA_W), lw["xa_wo"], res=x2, tm_max=1024, name="xa_out")
    if "w_down" in wb:
        u = _matmul(x2, wb["w_up"], gain=lw["norm_mlp"], act="relu2", out_dtype=BF16, name="mlp_up")
    else:
        u, wb["w_down"] = _matmul(x2, wb["w_up"], gain=lw["norm_mlp"], act="relu2", out_dtype=BF16,
                                  side_casts=((lw["w_down"], l),), name="mlp_up")
    x2 = _matmul(u, wb["w_down"], res=x2, tm_max=1024, tk=2048, name="mlp_down")
    return x2.reshape(b, t, d)


def kernel(x_prompt, x_sample, cache_fox_k, cache_fox_v, cache_fox_logf, cache_mem_k, cache_mem_v, state_hgrn, state_ssm, state_conv, page_table, mem_prompt, norm_mix, w_in, fox_gq, fox_gk, fox_bf, hg_lb_logits, hg_gnorm, conv_w, conv_b, dt_bias, a_log, d_skip, ssm_gnorm, w_out, norm_xa, norm_mem, xa_wq, xa_wk, xa_wv, xa_gq, xa_gk, xa_wo, norm_mlp, w_up, w_down):
    depth = w_in.shape[0]
    bp = x_prompt.shape[0]
    n_mem = mem_prompt.shape[1]
    pool = cache_fox_k.shape[1]
    cache_k = cache_fox_k.reshape(depth, pool, PAGE * N_HEADS, HEAD)
    cache_v = cache_fox_v.reshape(depth, pool, PAGE * N_HEADS, HEAD)
    cache_lf = cache_fox_logf.reshape(depth, pool, 1, PAGE * N_HEADS)
    mem2 = mem_prompt.reshape(bp * n_mem, -1)
    w_in_t = jnp.swapaxes(w_in, 1, 2)

    xp, xs = x_prompt, x_sample
    outs = {k: [] for k in ("p_fk", "p_fv", "p_fl", "p_hg", "p_ss", "p_cv", "p_mk", "p_mv",
                            "s_fk", "s_fv", "s_fl", "s_hg", "s_ss", "s_cv")}
    for l in range(depth):
        w_main, w_small = _prep_w_in(w_in_t, l)
        lw = dict(w_main=w_main, w_small=w_small, norm_mix=norm_mix[l],
                  fox_gq=fox_gq[l], fox_gk=fox_gk[l], fox_bf=fox_bf[l], hg_lb_logits=hg_lb_logits,
                  hg_gnorm=hg_gnorm[l], conv_w=conv_w[l], conv_b=conv_b[l], dt_bias=dt_bias[l], a_log=a_log[l],
                  d_skip=d_skip[l], ssm_gnorm=ssm_gnorm[l], w_out=w_out, norm_xa=norm_xa[l],
                  xa_wq=xa_wq[l].astype(BF16), xa_gq=xa_gq[l], xa_wo=xa_wo[l].astype(BF16),
                  norm_mlp=norm_mlp[l], w_up=w_up, w_down=w_down)

        wb = {}
        xp, fk, fv, fl, hg, ss, cv = _mixers(xp, lw, l, wb=wb, fox_past=None, hg_s0=None, ssm_s0=None,
                                             conv_s0=None, page_table=None)
        w_kv = jnp.concatenate([xa_wk[l], xa_wv[l]], axis=1).astype(BF16)
        kv = _matmul(mem2, w_kv, gain=norm_mem[l], name="mem_kv")
        mk = _headnorm(kv, xa_gk[l], col0=0, width=XA_W).reshape(bp, n_mem, XA_W)
        mv = kv[:, XA_W:].reshape(bp, n_mem, XA_W)
        xp = _cross_and_mlp(xp, lw, l, mk, mv, wb=wb)
        for key, val in zip(("p_fk", "p_fv", "p_fl", "p_hg", "p_ss", "p_cv"), (fk, fv, fl, hg, ss, cv)):
            outs[key].append(val)
        outs["p_mk"].append(mk.reshape(bp, n_mem, XA_HEADS, HEAD))
        outs["p_mv"].append(mv.reshape(bp, n_mem, XA_HEADS, HEAD))

        bs = xs.shape[0]
        xs, fk, fv, fl, hg, ss, cv = _mixers(xs, lw, l, wb=wb, fox_past=(cache_k, cache_v, cache_lf),
                                             hg_s0=state_hgrn[l], ssm_s0=state_ssm[l],
                                             conv_s0=state_conv[l], page_table=page_table)
        xs = _cross_and_mlp(xs, lw, l, cache_mem_k[l].reshape(bs, n_mem, XA_W),
                            cache_mem_v[l].reshape(bs, n_mem, XA_W), wb=wb)
        for key, val in zip(("s_fk", "s_fv", "s_fl", "s_hg", "s_ss", "s_cv"), (fk, fv, fl, hg, ss, cv)):
            outs[key].append(val)

    st = {k: jnp.stack(v) for k, v in outs.items()}
    return (xp, xs, st["p_fk"], st["p_fv"], st["p_fl"], st["p_hg"], st["p_ss"], st["p_cv"], st["p_mk"],
            st["p_mv"], st["s_fk"], st["s_fv"], st["s_fl"], st["s_hg"], st["s_ss"], st["s_cv"])
```

```python
import functools

import jax
import jax.numpy as jnp
from jax import lax
from jax.experimental import pallas as pl
from jax.experimental.pallas import tpu as pltpu

F32 = jnp.float32
BF16 = jnp.bfloat16

EPS = 1e-6
MASK_VALUE = -1e30
HEAD = 128
N_HEADS = 8
FOX_W = N_HEADS * HEAD
HG_W = N_HEADS * HEAD
SSM_W = 2048
SSM_P = 64
SSM_HEADS = SSM_W // SSM_P
SSM_GROUPS = 8
SSM_HPG = SSM_HEADS // SSM_GROUPS
SSM_N = 128
SSM_GW = SSM_HPG * SSM_P
CONV_W = 4
CONV_DIM = SSM_W + 2 * SSM_GROUPS * SSM_N
XA_HEADS = 4
XA_W = XA_HEADS * HEAD
HG_CHUNK = 64
HG_SUB = 16
SSD_CHUNK = 128
PAGE = 128
DT_LANE0 = N_HEADS

COL_XBC = 0
COL_HG = CONV_DIM
COL_Z = COL_HG + 4 * HG_W
COL_FQ = COL_Z + SSM_W
COL_FK = COL_FQ + FOX_W
COL_FV = COL_FK + FOX_W
MAIN_W = COL_FV + FOX_W

V7X_VMEM_LIMIT = 56 * 1024 * 1024


def _cparams(sem, vmem=V7X_VMEM_LIMIT):
    return pltpu.CompilerParams(dimension_semantics=sem, vmem_limit_bytes=vmem)


def _sigmoid(x):
    return 1.0 / (1.0 + jnp.exp(-x))


def _silu(x):
    h = 0.5 * x
    return h + h * jnp.tanh(h)


def _softplus(x):
    return jnp.maximum(x, 0.0) + jnp.log1p(jnp.exp(-jnp.abs(x)))


def _log_sigmoid(x):
    return -_softplus(-x)


def _dot_nt(a, b):
    return lax.dot_general(a, b, (((1,), (1,)), ((), ())), preferred_element_type=F32)


def _dot_tn(a, b):
    return lax.dot_general(a, b, (((0,), (0,)), ((), ())), preferred_element_type=F32)


def _dot(a, b):
    return jnp.dot(a, b, preferred_element_type=F32)


def _tri_incl(n):
    r = lax.broadcasted_iota(jnp.int32, (n, n), 0)
    c = lax.broadcasted_iota(jnp.int32, (n, n), 1)
    return r >= c


def _cumsum_rows(x):
    n = x.shape[0]
    tri = _tri_incl(n).astype(BF16)
    hi = x.astype(BF16)
    r1 = x - hi.astype(F32)
    mid = r1.astype(BF16)
    lo = (r1 - mid.astype(F32)).astype(BF16)
    return _dot(tri, hi) + _dot(tri, mid) + _dot(tri, lo)


def _side_cast_specs(side_casts, grid):
    n_steps = 1
    for g in grid:
        n_steps *= g

    def flat(*idx):
        step = idx[0]
        for g, i in zip(grid[1:], idx[1:]):
            step = step * g + i
        return step

    in_specs, args, out_specs, out_shapes = [], [], [], []
    for src, layer in side_casts:
        _, k2, n2 = src.shape
        rows = next(r for r in range(16, k2 + 1, 16) if k2 % r == 0 and k2 // r <= n_steps)
        last = k2 // rows - 1
        in_specs.append(pl.BlockSpec((None, rows, n2), lambda *idx, layer=layer, last=last:
                                     (layer, jnp.minimum(flat(*idx), last), 0)))
        args.append(src)
        out_specs.append(pl.BlockSpec((rows, n2), lambda *idx, last=last: (jnp.minimum(flat(*idx), last), 0)))
        out_shapes.append(jax.ShapeDtypeStruct((k2, n2), BF16))
    return in_specs, args, out_specs, out_shapes


def _do_side_casts(side_in, side_out):
    for s_in, s_out in zip(side_in, side_out):
        s_out[...] = s_in[...].astype(BF16)


def _mm_body(*refs, nk, norm, act, has_res, trans_w, n_sides, has_extra):
    it = iter(refs)
    a_ref = next(it)
    g_ref = next(it) if norm else None
    w_ref = next(it)
    we_ref = next(it) if has_extra else None
    r_ref = next(it) if has_res else None
    side_in = [next(it) for _ in range(n_sides)]
    o_ref = next(it)
    oe_ref = next(it) if has_extra else None
    side_out = [next(it) for _ in range(n_sides)]
    h_ref = next(it) if norm else None
    acc_ref = next(it) if nk > 1 else None
    j = pl.program_id(1)
    k = pl.program_id(2)

    if norm:
        @pl.when(j == 0)
        def _():
            x = a_ref[...].astype(F32)
            ms = jnp.mean(x * x, axis=-1, keepdims=True)
            h_ref[...] = (x * lax.rsqrt(ms + EPS) * g_ref[...]).astype(BF16)
            if has_extra:
                oe_ref[...] = _dot_nt(h_ref[...], we_ref[...])
        a = h_ref[...]
    else:
        a = a_ref[...]

    _do_side_casts(side_in, side_out)

    if trans_w:
        p = _dot_nt(a, w_ref[...])
    else:
        p = _dot(a, w_ref[...])

    def finish(v):
        if act == "relu2":
            v = jnp.square(jnp.maximum(v, 0.0))
        if has_res:
            v = v + r_ref[...]
        o_ref[...] = v.astype(o_ref.dtype)

    if nk == 1:
        finish(p)
    else:
        @pl.when(k == 0)
        def _():
            acc_ref[...] = p

        @pl.when(k > 0)
        def _():
            acc_ref[...] += p

        @pl.when(k == nk - 1)
        def _():
            finish(acc_ref[...])


def _pick_tile(n, candidates):
    for c in candidates:
        if n % c == 0:
            return c
    return n


def _matmul(a, w, *, gain=None, res=None, act=None, out_dtype=F32, tm_max=512, tn_max=None, tk=None,
            trans_w=False, side_casts=(), extra_w=None, name="matmul"):
    m, kdim = a.shape
    n = w.shape[0] if trans_w else w.shape[-1]
    small_m = m <= 64
    tm = m if small_m else _pick_tile(m, tuple(c for c in (1024, 512, 256, 128) if c <= tm_max))
    tn_max = tn_max or (2048 if small_m else 1024)
    tn = _pick_tile(n, tuple(c for c in (2048, 1024, 512, 256, 128) if c <= tn_max))
    tk = kdim if tk is None else tk
    nk = kdim // tk
    norm = gain is not None
    assert not (norm and nk > 1)
    assert extra_w is None or norm
    in_specs = [pl.BlockSpec((tm, tk), lambda i, j, k: (i, k))]
    args = [a]
    if norm:
        in_specs.append(pl.BlockSpec((1, kdim), lambda i, j, k: (0, 0)))
        args.append(gain.reshape(1, kdim).astype(F32))
    if trans_w:
        in_specs.append(pl.BlockSpec((tn, tk), lambda i, j, k: (j, k)))
    else:
        in_specs.append(pl.BlockSpec((tk, tn), lambda i, j, k: (k, j)))
    args.append(w)
    extra_specs, extra_shapes = [], []
    if extra_w is not None:
        ne = extra_w.shape[0]
        in_specs.append(pl.BlockSpec((ne, kdim), lambda i, j, k: (0, 0)))
        args.append(extra_w)
        extra_specs.append(pl.BlockSpec((tm, ne), lambda i, j, k: (i, 0)))
        extra_shapes.append(jax.ShapeDtypeStruct((m, ne), F32))
    if res is not None:
        in_specs.append(pl.BlockSpec((tm, tn), lambda i, j, k: (i, j)))
        args.append(res)
    grid = (m // tm, n // tn, nk)
    side_in_specs, side_args, side_specs, side_shapes = _side_cast_specs(side_casts, grid)
    in_specs += side_in_specs
    args += side_args
    scratch = []
    if norm:
        scratch.append(pltpu.VMEM((tm, kdim), BF16))
    if nk > 1:
        scratch.append(pltpu.VMEM((tm, tn), F32))
    out_specs = [pl.BlockSpec((tm, tn), lambda i, j, k: (i, j))] + extra_specs + side_specs
    out_shape = [jax.ShapeDtypeStruct((m, n), out_dtype)] + extra_shapes + side_shapes
    outs = pl.pallas_call(
        functools.partial(_mm_body, nk=nk, norm=norm, act=act, has_res=res is not None, trans_w=trans_w,
                          n_sides=len(side_casts), has_extra=extra_w is not None),
        grid=grid,
        in_specs=in_specs,
        out_specs=out_specs,
        out_shape=out_shape,
        scratch_shapes=scratch,
        compiler_params=_cparams(("arbitrary" if side_casts else "parallel", "arbitrary", "arbitrary")),
        name=name,
    )(*args)
    return outs if len(outs) > 1 else outs[0]


def _outproj_body(fo_ref, ho_ref, sy_ref, w_ref, r_ref, o_ref):
    k1 = FOX_W + HG_W
    p = (_dot(fo_ref[...], w_ref[0:FOX_W, :]) + _dot(ho_ref[...], w_ref[FOX_W:k1, :])
         + _dot(sy_ref[...], w_ref[k1:k1 + SSM_W, :]))
    o_ref[...] = p + r_ref[...]


def _out_proj(fo, ho, sy, w, res):
    m = fo.shape[0]
    kdim, n = w.shape
    tm = m if m <= 64 else _pick_tile(m, (1024, 512, 256, 128))
    tn = _pick_tile(n, (1024, 512, 256, 128))
    assert fo.shape[1] == FOX_W and ho.shape[1] == HG_W and sy.shape[1] == SSM_W
    return pl.pallas_call(
        _outproj_body,
        grid=(m // tm, n // tn),
        in_specs=[pl.BlockSpec((tm, FOX_W), lambda i, j: (i, 0)),
                  pl.BlockSpec((tm, HG_W), lambda i, j: (i, 0)),
                  pl.BlockSpec((tm, SSM_W), lambda i, j: (i, 0)),
                  pl.BlockSpec((kdim, tn), lambda i, j: (0, j)),
                  pl.BlockSpec((tm, tn), lambda i, j: (i, j))],
        out_specs=pl.BlockSpec((tm, tn), lambda i, j: (i, j)),
        out_shape=jax.ShapeDtypeStruct((m, n), F32),
        compiler_params=_cparams(("parallel", "arbitrary")),
        name="out_proj",
    )(fo, ho, sy, w, res)


def _headnorm_body(x_ref, g_ref, o_ref, *, n_heads):
    g = g_ref[...]
    for h in range(n_heads):
        sl = slice(h * HEAD, (h + 1) * HEAD)
        x = x_ref[:, sl].astype(F32)
        ms = jnp.mean(x * x, axis=-1, keepdims=True)
        o_ref[:, sl] = (x * lax.rsqrt(ms + EPS) * g).astype(o_ref.dtype)


def _headnorm(x, gain, *, col0, width, out_dtype=F32):
    m = x.shape[0]
    tm = m if m <= 64 else _pick_tile(m, (512, 256, 128))
    return pl.pallas_call(
        functools.partial(_headnorm_body, n_heads=width // HEAD),
        grid=(m // tm,),
        in_specs=[pl.BlockSpec((tm, width), lambda i: (i, col0 // width)),
                  pl.BlockSpec((1, HEAD), lambda i: (0, 0))],
        out_specs=pl.BlockSpec((tm, width), lambda i: (i, 0)),
        out_shape=jax.ShapeDtypeStruct((m, width), out_dtype),
        compiler_params=_cparams(("parallel",)),
        name="headnorm",
    )(x, gain.reshape(1, HEAD).astype(F32))


def _fox_gate_body(g_ref, bf_ref, lf_ref, cum_ref, *, t):
    lf = _log_sigmoid(g_ref[0] + bf_ref[...])
    lf_ref[0] = lf
    lane = lax.broadcasted_iota(jnp.int32, lf.shape, 1)
    c = lf
    shift = 1
    while shift < t:
        c = c + jnp.where(lane >= shift, pltpu.roll(c, shift, axis=1), 0.0)
        shift *= 2
    cum_ref[0] = c


def _fox_gate(gates_t, fox_bf):
    b, h, t = gates_t.shape
    spec = pl.BlockSpec((1, h, t), lambda i: (i, 0, 0))
    return pl.pallas_call(
        functools.partial(_fox_gate_body, t=t),
        grid=(b,),
        in_specs=[spec, pl.BlockSpec((h, 1), lambda i: (0, 0))],
        out_specs=[spec, spec],
        out_shape=[jax.ShapeDtypeStruct((b, h, t), F32)] * 2,
        compiler_params=_cparams(("parallel",)),
        name="fox_gate",
    )(gates_t, fox_bf.reshape(h, 1).astype(F32))


FLASH_STRIP = 256


def _fox_flash_body(q_ref, k_ref, v_ref, cq_ref, ck_ref, o_ref, m_ref, l_ref, acc_ref, *, tq):
    qi = pl.program_id(1)
    ki = pl.program_id(2)
    scale = HEAD ** -0.5

    @pl.when(ki == 0)
    def _():
        m_ref[...] = jnp.full(m_ref.shape, MASK_VALUE, F32)
        l_ref[...] = jnp.zeros(l_ref.shape, F32)
        acc_ref[...] = jnp.zeros(acc_ref.shape, F32)

    def update(diagonal):
        for r0 in range(0, tq, FLASH_STRIP):
            rows = slice(r0, r0 + FLASH_STRIP)
            if diagonal:
                keep = (r0 + lax.broadcasted_iota(jnp.int32, (FLASH_STRIP, tq), 0)
                        >= lax.broadcasted_iota(jnp.int32, (FLASH_STRIP, tq), 1))
            for h in range(N_HEADS):
                sl = slice(h * HEAD, (h + 1) * HEAD)
                s = _dot_nt(q_ref[0, rows, sl], k_ref[0, :, sl]) * scale
                s = s + cq_ref[0, rows, h:h + 1] - ck_ref[0, h:h + 1, :]
                if diagonal:
                    s = jnp.where(keep, s, MASK_VALUE)
                m_prev = m_ref[h, rows]
                m_new = jnp.maximum(m_prev, jnp.max(s, axis=-1, keepdims=True))
                alpha = jnp.exp(m_prev - m_new)
                p = jnp.exp(s - m_new)
                l_ref[h, rows] = alpha * l_ref[h, rows] + jnp.sum(p, axis=-1, keepdims=True)
                acc_ref[rows, sl] = alpha * acc_ref[rows, sl] + _dot(p.astype(BF16), v_ref[0, :, sl])
                m_ref[h, rows] = m_new

    @pl.when(ki < qi)
    def _():
        update(diagonal=False)

    @pl.when(ki == qi)
    def _():
        update(diagonal=True)
        for h in range(N_HEADS):
            sl = slice(h * HEAD, (h + 1) * HEAD)
            o_ref[0, :, sl] = (acc_ref[:, sl] / l_ref[h]).astype(o_ref.dtype)


def _fox_prep_body(q_ref, k_ref, v_ref, gq_ref, gk_ref, qn_ref, kn_ref, knb_ref, vf_ref, vb_ref):
    for h in range(N_HEADS):
        sl = slice(h * HEAD, (h + 1) * HEAD)
        q = q_ref[:, sl]
        qn_ref[:, sl] = (q * lax.rsqrt(jnp.mean(q * q, axis=-1, keepdims=True) + EPS) * gq_ref[...]).astype(BF16)
        k = k_ref[:, sl]
        kn = k * lax.rsqrt(jnp.mean(k * k, axis=-1, keepdims=True) + EPS) * gk_ref[...]
        kn_ref[:, sl] = kn
        knb_ref[:, sl] = kn.astype(BF16)
    v = v_ref[...]
    vf_ref[...] = v
    vb_ref[...] = v.astype(BF16)


def _fox_prep(proj2, gq, gk):
    m = proj2.shape[0]
    tm = _pick_tile(m, (512, 256, 128))
    col = lambda c: pl.BlockSpec((tm, FOX_W), lambda i: (i, c // FOX_W))
    out = pl.BlockSpec((tm, FOX_W), lambda i: (i, 0))
    gain = pl.BlockSpec((1, HEAD), lambda i: (0, 0))
    return pl.pallas_call(
        _fox_prep_body,
        grid=(m // tm,),
        in_specs=[col(COL_FQ), col(COL_FK), col(COL_FV), gain, gain],
        out_specs=[out] * 5,
        out_shape=[jax.ShapeDtypeStruct((m, FOX_W), dt) for dt in (BF16, F32, BF16, F32, BF16)],
        compiler_params=_cparams(("parallel",)),
        name="fox_prep",
    )(proj2, proj2, proj2, gq.reshape(1, HEAD).astype(F32), gk.reshape(1, HEAD).astype(F32))


def _fox_flash(qn, kn, vb, cum_col, cum_row, *, tq=512):
    b, t, _ = qn.shape
    nq = t // tq
    return pl.pallas_call(
        functools.partial(_fox_flash_body, tq=tq),
        grid=(b, nq, nq),
        in_specs=[
            pl.BlockSpec((1, tq, FOX_W), lambda i, q, k: (i, q, 0)),
            pl.BlockSpec((1, tq, FOX_W), lambda i, q, k: (i, jnp.minimum(k, q), 0)),
            pl.BlockSpec((1, tq, FOX_W), lambda i, q, k: (i, jnp.minimum(k, q), 0)),
            pl.BlockSpec((1, tq, N_HEADS), lambda i, q, k: (i, q, 0)),
            pl.BlockSpec((1, N_HEADS, tq), lambda i, q, k: (i, 0, jnp.minimum(k, q))),
        ],
        out_specs=pl.BlockSpec((1, tq, FOX_W), lambda i, q, k: (i, q, 0)),
        out_shape=jax.ShapeDtypeStruct((b, t, FOX_W), BF16),
        scratch_shapes=[pltpu.VMEM((N_HEADS, tq, 1), F32), pltpu.VMEM((N_HEADS, tq, 1), F32),
                        pltpu.VMEM((tq, FOX_W), F32)],
        compiler_params=_cparams(("parallel", "parallel", "arbitrary")),
        name="fox_flash",
    )(qn, kn, vb, cum_col, cum_row)


def _fox_decode_body(pt_ref, q_ref, kn_ref, vn_ref, lfn_ref, *rest, pages_per_step, n_steps):
    del pt_ref
    pps = pages_per_step
    k_refs = rest[:pps]
    v_refs = rest[pps:2 * pps]
    lf_refs = rest[2 * pps:3 * pps]
    o_ref, m_ref, l_ref, acc_ref, carry_ref = rest[3 * pps:]
    s = pl.program_id(1)
    scale = HEAD ** -0.5
    rows = PAGE * N_HEADS
    lane = lax.broadcasted_iota(jnp.int32, (1, rows), 1)
    own = (lax.broadcasted_iota(jnp.int32, (N_HEADS, rows), 1) % N_HEADS
           == lax.broadcasted_iota(jnp.int32, (N_HEADS, rows), 0))

    @pl.when(s == 0)
    def _():
        m_ref[...] = jnp.sum(q_ref[0] * kn_ref[0], axis=-1, keepdims=True) * scale
        l_ref[...] = jnp.ones(l_ref.shape, F32)
        acc_ref[...] = vn_ref[0]
        carry_ref[...] = jnp.zeros(carry_ref.shape, F32)

    lf = jnp.concatenate([lf_refs[r][0, 0] for r in range(pps)], axis=0)
    tot = lf
    suf = lf
    shift = N_HEADS
    while shift < rows:
        tot = tot + pltpu.roll(tot, shift, axis=1)
        suf = suf + jnp.where(lane + shift < rows, pltpu.roll(suf, rows - shift, axis=1), 0.0)
        shift *= 2
    page = lax.broadcasted_iota(jnp.int32, (pps, 1), 0)
    newer = tot
    shift = 1
    while shift < pps:
        newer = newer + jnp.where(page >= shift, pltpu.roll(newer, shift, axis=0), 0.0)
        shift *= 2
    after = suf - lf + (newer - tot) + carry_ref[...]
    carry_ref[...] = carry_ref[...] + newer[pps - 1:pps, :]

    q = q_ref[0].astype(BF16)
    scs = []
    for r in range(pps):
        sc = _dot_nt(q, k_refs[r][0, 0].astype(BF16)) * scale + lfn_ref[0] + after[r:r + 1, :]
        scs.append(jnp.where(own, sc, MASK_VALUE))
    m_prev = m_ref[...]
    m_new = m_prev
    for sc in scs:
        m_new = jnp.maximum(m_new, jnp.max(sc, axis=-1, keepdims=True))
    alpha = jnp.exp(m_prev - m_new)
    l_new = alpha * l_ref[...]
    acc = alpha * acc_ref[...]
    for r, sc in enumerate(scs):
        p = jnp.exp(sc - m_new)
        l_new = l_new + jnp.sum(p, axis=-1, keepdims=True)
        acc = acc + _dot(p.astype(BF16), v_refs[r][0, 0].astype(BF16))
    l_ref[...] = l_new
    acc_ref[...] = acc
    m_ref[...] = m_new

    @pl.when(s == n_steps - 1)
    def _():
        o_ref[0] = (acc_ref[...] / l_ref[...]).astype(o_ref.dtype)


def _fox_decode(layer, qn, kn, v_new, lf_new, cache_k, cache_v, cache_lf, page_table, *, pages_per_step=16):
    b = qn.shape[0]
    n_pages = page_table.shape[1]
    pps = pages_per_step
    n_steps = n_pages // pps
    rows = PAGE * N_HEADS

    def page_map(r):
        return lambda i, s, pt: (layer, pt[i, n_pages - 1 - (s * pps + r)], 0, 0)

    head = pl.BlockSpec((1, N_HEADS, HEAD), lambda i, s, pt: (i, 0, 0))
    in_specs = [head, head, head, pl.BlockSpec((1, N_HEADS, 1), lambda i, s, pt: (i, 0, 0))]
    in_specs += [pl.BlockSpec((1, 1, rows, HEAD), page_map(r)) for r in range(pps)]
    in_specs += [pl.BlockSpec((1, 1, rows, HEAD), page_map(r)) for r in range(pps)]
    in_specs += [pl.BlockSpec((1, 1, 1, rows), page_map(r)) for r in range(pps)]
    grid_spec = pltpu.PrefetchScalarGridSpec(
        num_scalar_prefetch=1, grid=(b, n_steps), in_specs=in_specs,
        out_specs=pl.BlockSpec((1, N_HEADS, HEAD), lambda i, s, pt: (i, 0, 0)),
        scratch_shapes=[pltpu.VMEM((N_HEADS, 1), F32), pltpu.VMEM((N_HEADS, 1), F32),
                        pltpu.VMEM((N_HEADS, HEAD), F32), pltpu.VMEM((1, rows), F32)])
    return pl.pallas_call(
        functools.partial(_fox_decode_body, pages_per_step=pps, n_steps=n_steps),
        grid_spec=grid_spec,
        out_shape=jax.ShapeDtypeStruct((b, N_HEADS, HEAD), BF16),
        compiler_params=_cparams(("parallel", "arbitrary")),
        name="fox_decode",
    )(page_table, qn, kn, v_new, lf_new, *([cache_k] * pps), *([cache_v] * pps), *([cache_lf] * pps))


def _rows_of_chunk(ref, cols, chunk_rows):
    x = ref[0, :, cols].astype(F32)
    if x.shape[0] == chunk_rows:
        return x
    assert x.shape[0] == 1
    return jnp.broadcast_to(x, (chunk_rows, x.shape[1]))


def _hgrn_body(*refs, layer, t_valid, has_s0):
    it = iter(refs)
    x_ref = next(it)
    lbl_ref = next(it)
    gn_ref = next(it)
    s0_ref = next(it) if has_s0 else None
    o_ref = next(it)
    st_ref = next(it)
    q_s, k_s, v_s, b_s, stt_s = (next(it) for _ in range(5))
    c = pl.program_id(1)
    nc = pl.num_programs(1)
    cs = HG_CHUNK
    t_blk = o_ref.shape[1]

    @pl.when(c == 0)
    def _():
        for h in range(N_HEADS):
            if has_s0:
                stt_s[h] = s0_ref[0, h].astype(F32).T
            else:
                stt_s[h] = jnp.zeros((HEAD, HEAD), F32)

    lg = lbl_ref[...].astype(F32)
    e = jnp.exp(lg - jnp.max(lg, axis=0, keepdims=True))
    pr = e / jnp.sum(e, axis=0, keepdims=True)
    lb = jnp.sum(pr[0:layer + 1], axis=0, keepdims=True) - pr[0:1]

    hf = _rows_of_chunk(x_ref, slice(HG_W, 2 * HG_W), cs)
    logf = jnp.log(lb + (1.0 - lb) * _sigmoid(hf))
    kk = (1.0 - lb) * _sigmoid(-hf)
    qq = _silu(_rows_of_chunk(x_ref, slice(0, HG_W), cs))
    vv = _rows_of_chunk(x_ref, slice(2 * HG_W, 3 * HG_W), cs)
    if t_valid is not None:
        live = (c * cs + lax.broadcasted_iota(jnp.int32, (cs, 1), 0)) < t_valid
        logf = jnp.where(live, logf, 0.0)
        kk = jnp.where(live, kk, 0.0)
        qq = jnp.where(live, qq, 0.0)
        vv = jnp.where(live, vv, 0.0)
    q_s[...] = qq
    k_s[...] = kk
    v_s[...] = vv
    b_s[...] = _cumsum_rows(logf)

    n_sub = cs // HG_SUB
    causal = _tri_incl(cs)
    zeros_sub = jnp.zeros((HG_SUB, HEAD), F32)
    g = gn_ref[...]

    atts = []
    for h in range(N_HEADS):
        sl = slice(h * HEAD, (h + 1) * HEAD)
        q_parts, k_parts = [], []
        for i in range(n_sub):
            r0 = i * HG_SUB
            n_keys = r0 + HG_SUB
            b_i = b_s[r0 - 1:r0, sl] if i > 0 else jnp.zeros((1, HEAD), F32)
            qd = q_s[r0:n_keys, sl] * jnp.exp(b_s[r0:n_keys, sl] - b_i)
            kd = k_s[0:n_keys, sl] * jnp.exp(b_i - b_s[0:n_keys, sl])
            q_parts.append(jnp.concatenate([zeros_sub] * i + [qd] + [zeros_sub] * (n_sub - 1 - i), axis=0))
            k_parts.append(jnp.concatenate([kd] + [zeros_sub] * (n_sub - 1 - i), axis=0))
        q_cat = jnp.concatenate(q_parts, axis=1).astype(BF16)
        k_cat = jnp.concatenate(k_parts, axis=1).astype(BF16)
        atts.append(jnp.where(causal, _dot_nt(q_cat, k_cat), 0.0).astype(BF16))

    for h in range(N_HEADS):
        sl = slice(h * HEAD, (h + 1) * HEAD)
        stt = stt_s[h]
        bh = b_s[:, sl]
        b_end = b_s[cs - 1:cs, sl]
        o = _dot(atts[h], v_s[:, sl].astype(BF16))
        o = o + _dot_nt((q_s[:, sl] * jnp.exp(bh)).astype(BF16), stt.astype(BF16))
        kd_end = k_s[:, sl] * jnp.exp(b_end - bh)
        stt_s[h] = stt * jnp.exp(b_end) + _dot_tn(v_s[:, sl].astype(BF16), kd_end.astype(BF16))
        ms = jnp.mean(o * o, axis=-1, keepdims=True)
        gate = _silu(_rows_of_chunk(x_ref, slice(3 * HG_W + h * HEAD, 3 * HG_W + (h + 1) * HEAD), cs))
        res = o * lax.rsqrt(ms + EPS) * g * gate
        o_ref[0, :, sl] = res[0:t_blk].astype(o_ref.dtype)

    @pl.when(c == nc - 1)
    def _():
        for h in range(N_HEADS):
            st_ref[0, h] = stt_s[h].T


def _hgrn(proj, lb_logits, gnorm, s0, *, layer):
    b, t, _ = proj.shape
    cs = HG_CHUNK
    assert t == 1 or t % cs == 0
    t_blk, n_chunks, t_valid = (1, 1, 1) if t == 1 else (cs, t // cs, None)
    has_s0 = s0 is not None
    depth = lb_logits.shape[0]
    in_specs = [pl.BlockSpec((1, t_blk, 4 * HG_W), lambda i, c: (i, c, COL_HG // (4 * HG_W))),
                pl.BlockSpec((depth, HG_W), lambda i, c: (0, 0)),
                pl.BlockSpec((1, HEAD), lambda i, c: (0, 0))]
    args = [proj, lb_logits.astype(F32), gnorm.reshape(1, HEAD).astype(F32)]
    if has_s0:
        in_specs.append(pl.BlockSpec((1, N_HEADS, HEAD, HEAD), lambda i, c: (i, 0, 0, 0)))
        args.append(s0)
    return pl.pallas_call(
        functools.partial(_hgrn_body, layer=layer, t_valid=t_valid, has_s0=has_s0),
        grid=(b, n_chunks),
        in_specs=in_specs,
        out_specs=[pl.BlockSpec((1, t_blk, HG_W), lambda i, c: (i, c, 0)),
                   pl.BlockSpec((1, N_HEADS, HEAD, HEAD), lambda i, c: (i, 0, 0, 0))],
        out_shape=[jax.ShapeDtypeStruct((b, t, HG_W), BF16),
                   jax.ShapeDtypeStruct((b, N_HEADS, HEAD, HEAD), F32)],
        scratch_shapes=[pltpu.VMEM((cs, HG_W), F32)] * 4 + [pltpu.VMEM((N_HEADS, HEAD, HEAD), F32)],
        compiler_params=_cparams(("parallel", "arbitrary")),
        name="hgrn2",
    )(*args)


def _lane_blocks(cols, width):
    rows = cols[0].shape[0]
    lane = lax.broadcasted_iota(jnp.int32, (rows, width * len(cols)), 1)
    out = jnp.broadcast_to(cols[-1], (rows, width * len(cols)))
    for j in range(len(cols) - 2, -1, -1):
        out = jnp.where(lane < (j + 1) * width, cols[j], out)
    return out


def _ssd_body(*refs, t_valid, has_s0, n_sides):
    it = iter(refs)
    xbc_ref = next(it)
    z_ref = next(it)
    dts_ref = next(it)
    cw_ref = next(it)
    cb_ref = next(it)
    dtb_ref = next(it)
    alog_ref = next(it)
    dsk_ref = next(it)
    gn_ref = next(it)
    cs0_ref = next(it) if has_s0 else None
    s0_ref = next(it) if has_s0 else None
    side_in = [next(it) for _ in range(n_sides)]
    y_ref = next(it)
    st_ref = next(it)
    side_out = [next(it) for _ in range(n_sides)]
    prev_s = next(it)
    act_s = next(it)
    c = pl.program_id(1)
    cs = SSD_CHUNK
    t_blk = y_ref.shape[1]
    tail0 = 8 - (CONV_W - 1)

    @pl.when(c == 0)
    def _():
        prev_s[...] = jnp.zeros(prev_s.shape, F32)
        if has_s0:
            prev_s[tail0:8, :] = cs0_ref[0].astype(F32)
            st_ref[...] = s0_ref[...].astype(F32)
        else:
            st_ref[...] = jnp.zeros(st_ref.shape, F32)

    _do_side_casts(side_in, side_out)

    cur = _rows_of_chunk(xbc_ref, slice(0, CONV_DIM), cs)
    win = jnp.concatenate([prev_s[...], cur[0:8]], axis=0)
    conv = cb_ref[...] + cur * cw_ref[CONV_W - 1:CONV_W, :]
    head = cb_ref[...] + win[8:16] * cw_ref[CONV_W - 1:CONV_W, :]
    for d in range(1, CONV_W):
        w_d = cw_ref[CONV_W - 1 - d:CONV_W - d, :]
        conv = conv + pltpu.roll(cur, d, axis=0) * w_d
        head = head + pltpu.roll(win, d, axis=0)[8:16] * w_d
    act_s[...] = _silu(conv)
    act_s[0:8, :] = _silu(head)
    prev_s[...] = cur[cs - 8:cs]

    dt = _softplus(_rows_of_chunk(dts_ref, slice(0, 128), cs) + dtb_ref[...])
    if t_valid is not None:
        live = (c * cs + lax.broadcasted_iota(jnp.int32, (cs, 1), 0)) < t_valid
        dt = jnp.where(live, dt, 0.0)
    a = _cumsum_rows(dt * (-jnp.exp(alog_ref[...])))
    a_t = a.T
    dt_t = dt.T
    a_end = a[cs - 1:cs, :]
    w_upd = dt * jnp.exp(a_end - a)
    e_a = jnp.exp(a)
    e_end = jnp.exp(a_end)
    causal = _tri_incl(cs)
    lane_g = lax.broadcasted_iota(jnp.int32, (cs, SSM_GW), 1)
    row_g = lax.broadcasted_iota(jnp.int32, (SSM_GW, 1), 0)

    for g in range(SSM_GROUPS):
        xs = slice(g * SSM_GW, (g + 1) * SSM_GW)
        bsl = slice(SSM_W + g * SSM_N, SSM_W + (g + 1) * SSM_N)
        csl = slice(SSM_W + SSM_GROUPS * SSM_N + g * SSM_N, SSM_W + SSM_GROUPS * SSM_N + (g + 1) * SSM_N)
        xg = act_s[:, xs]
        bm = act_s[:, bsl].astype(BF16)
        cm = act_s[:, csl].astype(BF16)
        cbm = jnp.where(causal, _dot_nt(cm, bm), 0.0)
        s_g = st_ref[0, g]
        lanes = [DT_LANE0 + g * SSM_HPG + j for j in range(SSM_HPG)]
        yg = _dot_nt(cm, s_g.astype(BF16)) * _lane_blocks([e_a[:, ln:ln + 1] for ln in lanes], SSM_P)
        for j, ln in enumerate(lanes):
            diff = a[:, ln:ln + 1] - a_t[ln:ln + 1, :]
            mh = cbm * jnp.exp(jnp.minimum(diff, 0.0)) * dt_t[ln:ln + 1, :]
            xm = jnp.where(jnp.logical_and(lane_g >= j * SSM_P, lane_g < (j + 1) * SSM_P), xg, 0.0)
            yg = yg + _dot(mh.astype(BF16), xm.astype(BF16))
        xw = xg * _lane_blocks([w_upd[:, ln:ln + 1] for ln in lanes], SSM_P)
        decay = jnp.broadcast_to(e_end[:, lanes[-1]:lanes[-1] + 1], (SSM_GW, 1))
        for j in range(SSM_HPG - 2, -1, -1):
            decay = jnp.where(row_g < (j + 1) * SSM_P, e_end[:, lanes[j]:lanes[j] + 1], decay)
        st_ref[0, g] = s_g * decay + _dot_tn(xw.astype(BF16), bm)

        y = yg + dsk_ref[:, xs] * xg
        y = y * _silu(_rows_of_chunk(z_ref, xs, cs))
        ms = jnp.mean(y * y, axis=-1, keepdims=True)
        res = y * lax.rsqrt(ms + EPS) * gn_ref[:, xs]
        y_ref[0, :, xs] = res[0:t_blk].astype(y_ref.dtype)


def _ssd(proj, small, conv_w, conv_b, dt_bias, a_log, d_skip, gnorm, conv_s0, s0, *, side_casts=()):
    b, t, _ = proj.shape
    cs = SSD_CHUNK
    assert t == 1 or t % cs == 0
    t_blk, n_chunks, t_valid = (1, 1, 1) if t == 1 else (cs, t // cs, None)
    has_s0 = s0 is not None
    pad = jnp.zeros((DT_LANE0,), F32)
    tail = jnp.zeros((128 - DT_LANE0 - SSM_HEADS,), F32)
    on_dt_lanes = lambda v: jnp.concatenate([pad, v.astype(F32), tail]).reshape(1, 128)
    const = lambda shape: pl.BlockSpec(shape, lambda i, c: (0,) * len(shape))
    in_specs = [pl.BlockSpec((1, t_blk, CONV_DIM), lambda i, c: (i, c, COL_XBC // CONV_DIM)),
                pl.BlockSpec((1, t_blk, SSM_W), lambda i, c: (i, c, COL_Z // SSM_W)),
                pl.BlockSpec((1, t_blk, 128), lambda i, c: (i, c, 0)),
                const((CONV_W, CONV_DIM)), const((1, CONV_DIM)), const((1, 128)), const((1, 128)),
                const((1, SSM_W)), const((1, SSM_W))]
    args = [proj, proj, small, conv_w.astype(F32), conv_b.reshape(1, CONV_DIM).astype(F32),
            on_dt_lanes(dt_bias), on_dt_lanes(a_log),
            jnp.repeat(d_skip.astype(F32), SSM_P).reshape(1, SSM_W), gnorm.reshape(1, SSM_W).astype(F32)]
    if has_s0:
        in_specs += [pl.BlockSpec((1, CONV_W - 1, CONV_DIM), lambda i, c: (i, 0, 0)),
                     pl.BlockSpec((1, SSM_GROUPS, SSM_GW, SSM_N), lambda i, c: (i, 0, 0, 0))]
        args += [conv_s0, s0.reshape(b, SSM_GROUPS, SSM_GW, SSM_N)]
    grid = (b, n_chunks)
    side_in_specs, side_args, side_specs, side_shapes = _side_cast_specs(side_casts, grid)
    y, st, *copies = pl.pallas_call(
        functools.partial(_ssd_body, t_valid=t_valid, has_s0=has_s0, n_sides=len(side_casts)),
        grid=grid,
        in_specs=in_specs + side_in_specs,
        out_specs=[pl.BlockSpec((1, t_blk, SSM_W), lambda i, c: (i, c, 0)),
                   pl.BlockSpec((1, SSM_GROUPS, SSM_GW, SSM_N), lambda i, c: (i, 0, 0, 0))] + side_specs,
        out_shape=[jax.ShapeDtypeStruct((b, t, SSM_W), BF16),
                   jax.ShapeDtypeStruct((b, SSM_GROUPS, SSM_GW, SSM_N), F32)] + side_shapes,
        scratch_shapes=[pltpu.VMEM((8, CONV_DIM), F32), pltpu.VMEM((cs, CONV_DIM), F32)],
        compiler_params=_cparams(("arbitrary" if side_casts else "parallel", "arbitrary")),
        name="ssd",
    )(*(args + side_args))
    return (y, st.reshape(b, SSM_HEADS, SSM_P, SSM_N), *copies)


def _xattn_body(q_ref, k_ref, v_ref, gq_ref, o_ref):
    scale = HEAD ** -0.5
    g = gq_ref[...]
    for h in range(XA_HEADS):
        sl = slice(h * HEAD, (h + 1) * HEAD)
        q = q_ref[0, :, sl].astype(F32)
        ms = jnp.mean(q * q, axis=-1, keepdims=True)
        qn = q * lax.rsqrt(ms + EPS) * g
        s = _dot_nt(qn.astype(BF16), k_ref[0, :, sl].astype(BF16)) * scale
        p = jnp.exp(s - jnp.max(s, axis=-1, keepdims=True))
        o = _dot(p.astype(BF16), v_ref[0, :, sl].astype(BF16)) / jnp.sum(p, axis=-1, keepdims=True)
        o_ref[0, :, sl] = o.astype(o_ref.dtype)


def _xattn(q, mk, mv, gq):
    b, t, _ = q.shape
    n_mem = mk.shape[1]
    tq = t if t <= 512 else 512
    return pl.pallas_call(
        _xattn_body,
        grid=(b, t // tq),
        in_specs=[pl.BlockSpec((1, tq, XA_W), lambda i, j: (i, j, 0)),
                  pl.BlockSpec((1, n_mem, XA_W), lambda i, j: (i, 0, 0)),
                  pl.BlockSpec((1, n_mem, XA_W), lambda i, j: (i, 0, 0)),
                  pl.BlockSpec((1, HEAD), lambda i, j: (0, 0))],
        out_specs=pl.BlockSpec((1, tq, XA_W), lambda i, j: (i, j, 0)),
        out_shape=jax.ShapeDtypeStruct((b, t, XA_W), BF16),
        compiler_params=_cparams(("parallel", "parallel")),
        name="xattn",
    )(q, mk, mv, gq.reshape(1, HEAD).astype(F32))


IN_PIECES = (("fq", FOX_W), ("fk", FOX_W), ("fv", FOX_W), ("fg", N_HEADS), ("hg", 4 * HG_W), ("z", SSM_W),
             ("xbc", CONV_DIM), ("dt", SSM_HEADS))
IN_WIDTH = sum(width for _, width in IN_PIECES)
MAIN_COLS = dict(xbc=COL_XBC, hg=COL_HG, z=COL_Z, fq=COL_FQ, fk=COL_FK, fv=COL_FV)
W_TILE = 1024
SUBLANES = 8


def _in_piece_offsets():
    offs, o = {}, 0
    for name, width in IN_PIECES:
        offs[name] = (o, width)
        o += width
    return offs


def _w_main_tables():
    offs = _in_piece_offsets()
    a_blk, b_blk, shift = [0] * (MAIN_W // W_TILE), [0] * (MAIN_W // W_TILE), [0] * (MAIN_W // W_TILE)
    for name, dst in MAIN_COLS.items():
        src_off, width = offs[name]
        assert dst % W_TILE == 0 and width % W_TILE == 0
        for i in range(width // W_TILE):
            start = src_off + i * W_TILE
            t = dst // W_TILE + i
            a_blk[t], shift[t] = start // W_TILE, start % W_TILE
            assert shift[t] in (0, SUBLANES)
            b_blk[t] = (start - shift[t] + W_TILE) // SUBLANES if shift[t] else 0
    return a_blk, b_blk, shift


def _w_main_body(tbl_ref, a_ref, b_ref, o_ref):
    j = pl.program_id(0)

    @pl.when(tbl_ref[2, j] == 0)
    def _():
        o_ref[...] = a_ref[...].astype(BF16)

    @pl.when(tbl_ref[2, j] != 0)
    def _():
        o_ref[...] = jnp.concatenate([a_ref[SUBLANES:W_TILE, :], b_ref[...]], axis=0).astype(BF16)


def _w_small_body(g_ref, *rest):
    d_refs, o_ref = rest[:-1], rest[-1]
    used = SUBLANES * (1 + len(d_refs))
    pad = jnp.zeros((o_ref.shape[0] - used, o_ref.shape[1]), F32)
    o_ref[...] = jnp.concatenate([g_ref[...]] + [d_ref[...] for d_ref in d_refs] + [pad], axis=0).astype(BF16)


def _prep_w_in(w_in_t, layer):
    depth, width, d = w_in_t.shape
    assert width == IN_WIDTH and N_HEADS == SUBLANES
    offs = _in_piece_offsets()
    a_blk, b_blk, shift = _w_main_tables()
    tc = _pick_tile(d, (1024, 512, 256, 128))
    tables = jnp.asarray([a_blk, b_blk, shift], jnp.int32)
    main = pl.pallas_call(
        _w_main_body,
        grid_spec=pltpu.PrefetchScalarGridSpec(
            num_scalar_prefetch=1, grid=(MAIN_W // W_TILE, d // tc),
            in_specs=[pl.BlockSpec((None, W_TILE, tc), lambda j, c, tbl: (layer, tbl[0, j], c)),
                      pl.BlockSpec((None, SUBLANES, tc), lambda j, c, tbl: (layer, tbl[1, j], c))],
            out_specs=pl.BlockSpec((W_TILE, tc), lambda j, c, tbl: (j, c))),
        out_shape=jax.ShapeDtypeStruct((MAIN_W, d), BF16),
        compiler_params=_cparams(("parallel", "arbitrary")),
        name="w_in_main",
    )(tables, w_in_t, w_in_t)

    g_off, (d_off, d_rows) = offs["fg"][0], offs["dt"]
    assert g_off % SUBLANES == 0 and d_off % SUBLANES == 0 and d_rows % SUBLANES == 0
    rows8 = lambda off: pl.BlockSpec((None, SUBLANES, tc), lambda c: (layer, off // SUBLANES, c))
    n_dt = d_rows // SUBLANES
    small = pl.pallas_call(
        _w_small_body,
        grid=(d // tc,),
        in_specs=[rows8(g_off)] + [rows8(d_off + i * SUBLANES) for i in range(n_dt)],
        out_specs=pl.BlockSpec((128, tc), lambda c: (0, c)),
        out_shape=jax.ShapeDtypeStruct((128, d), BF16),
        compiler_params=_cparams(("parallel",)),
        name="w_in_small",
    )(*([w_in_t] * (1 + n_dt)))
    return main, small


def _mixers(x, lw, l, *, wb, fox_past, hg_s0, ssm_s0, conv_s0, page_table):
    b, t, d = x.shape
    x2 = x.reshape(b * t, d)
    proj2, small2 = _matmul(x2, lw["w_main"], gain=lw["norm_mix"], trans_w=True, extra_w=lw["w_small"],
                            name="in_proj")
    proj = proj2.reshape(b, t, MAIN_W)
    small = small2.reshape(b, t, 128)

    gates_t = jnp.swapaxes(small[:, :, 0:N_HEADS], 1, 2)
    logf_t, cum_t = _fox_gate(gates_t, lw["fox_bf"])
    logf = jnp.swapaxes(logf_t, 1, 2)

    if fox_past is None:
        qn, kn, knb, fv, vb = _fox_prep(proj2, lw["fox_gq"], lw["fox_gk"])
        fo = _fox_flash(qn.reshape(b, t, FOX_W), knb.reshape(b, t, FOX_W), vb.reshape(b, t, FOX_W),
                        jnp.swapaxes(cum_t, 1, 2), cum_t)
    else:
        qn = _headnorm(proj2, lw["fox_gq"], col0=COL_FQ, width=FOX_W)
        kn = _headnorm(proj2, lw["fox_gk"], col0=COL_FK, width=FOX_W)
        fv = proj2[:, COL_FV:COL_FV + FOX_W]
        cache_k, cache_v, cache_lf = fox_past
        fo = _fox_decode(l, qn.reshape(b, N_HEADS, HEAD), kn.reshape(b, N_HEADS, HEAD),
                         fv.reshape(b, N_HEADS, HEAD), logf_t, cache_k, cache_v, cache_lf, page_table)
        fo = fo.reshape(b, t, FOX_W)

    ho, hg_state = _hgrn(proj, lw["hg_lb_logits"], lw["hg_gnorm"], hg_s0, layer=l)
    ssd_args = (proj, small, lw["conv_w"], lw["conv_b"], lw["dt_bias"], lw["a_log"], lw["d_skip"],
                lw["ssm_gnorm"], conv_s0, ssm_s0)
    if "w_out" in wb:
        sy, ssm_state = _ssd(*ssd_args)
    else:
        sy, ssm_state, wb["w_out"], wb["w_up"] = _ssd(*ssd_args, side_casts=((lw["w_out"], l), (lw["w_up"], l)))
    x_new = _out_proj(fo.reshape(b * t, FOX_W), ho.reshape(b * t, HG_W), sy.reshape(b * t, SSM_W),
                      wb["w_out"], x2).reshape(b, t, d)

    keep = CONV_W - 1
    if t >= keep:
        conv_state = proj[:, t - keep:, COL_XBC:COL_XBC + CONV_DIM]
    else:
        prev = jnp.zeros((b, keep, CONV_DIM), F32) if conv_s0 is None else conv_s0.astype(F32)
        conv_state = jnp.concatenate([prev[:, t:], proj[:, :, COL_XBC:COL_XBC + CONV_DIM]], axis=1)
    fk = kn.reshape(b, t, N_HEADS, HEAD)
    return x_new, fk, fv.reshape(b, t, N_HEADS, HEAD), logf, hg_state, ssm_state, conv_state


def _cross_and_mlp(x, lw, l, mk, mv, *, wb):
    b, t, d = x.shape
    x2 = x.reshape(b * t, d)
    q = _matmul(x2, lw["xa_wq"], gain=lw["norm_xa"], name="xa_q")
    o = _xattn(q.reshape(b, t, XA_W), mk, mv, lw["xa_gq"])
    x2 = _matmul(o.reshape(b * t, XA_W), lw["xa_wo"], res=x2, tm_max=1024, name="xa_out")
    if "w_down" in wb:
        u = _matmul(x2, wb["w_up"], gain=lw["norm_mlp"], act="relu2", out_dtype=BF16, name="mlp_up")
    else:
        u, wb["w_down"] = _matmul(x2, wb["w_up"], gain=lw["norm_mlp"], act="relu2", out_dtype=BF16,
                                  side_casts=((lw["w_down"], l),), name="mlp_up")
    x2 = _matmul(u, wb["w_down"], res=x2, tm_max=1024, tk=2048, name="mlp_down")
    return x2.reshape(b, t, d)


def kernel(x_prompt, x_sample, cache_fox_k, cache_fox_v, cache_fox_logf, cache_mem_k, cache_mem_v, state_hgrn, state_ssm, state_conv, page_table, mem_prompt, norm_mix, w_in, fox_gq, fox_gk, fox_bf, hg_lb_logits, hg_gnorm, conv_w, conv_b, dt_bias, a_log, d_skip, ssm_gnorm, w_out, norm_xa, norm_mem, xa_wq, xa_wk, xa_wv, xa_gq, xa_gk, xa_wo, norm_mlp, w_up, w_down):
    depth = w_in.shape[0]
    bp = x_prompt.shape[0]
    n_mem = mem_prompt.shape[1]
    pool = cache_fox_k.shape[1]
    cache_k = cache_fox_k.reshape(depth, pool, PAGE * N_HEADS, HEAD)
    cache_v = cache_fox_v.reshape(depth, pool, PAGE * N_HEADS, HEAD)
    cache_lf = cache_fox_logf.reshape(depth, pool, 1, PAGE * N_HEADS)
    mem2 = mem_prompt.reshape(bp * n_mem, -1)
    w_in_t = jnp.swapaxes(w_in, 1, 2)

    xp, xs = x_prompt, x_sample
    outs = {k: [] for k in ("p_fk", "p_fv", "p_fl", "p_hg", "p_ss", "p_cv", "p_mk", "p_mv",
                            "s_fk", "s_fv", "s_fl", "s_hg", "s_ss", "s_cv")}
    for l in range(depth):
        w_main, w_small = _prep_w_in(w_in_t, l)
        lw = dict(w_main=w_main, w_small=w_small, norm_mix=norm_mix[l],
                  fox_gq=fox_gq[l], fox_gk=fox_gk[l], fox_bf=fox_bf[l], hg_lb_logits=hg_lb_logits,
                  hg_gnorm=hg_gnorm[l], conv_w=conv_w[l], conv_b=conv_b[l], dt_bias=dt_bias[l], a_log=a_log[l],
                  d_skip=d_skip[l], ssm_gnorm=ssm_gnorm[l], w_out=w_out, norm_xa=norm_xa[l],
                  xa_wq=xa_wq[l].astype(BF16), xa_gq=xa_gq[l], xa_wo=xa_wo[l].astype(BF16),
                  norm_mlp=norm_mlp[l], w_up=w_up, w_down=w_down)

        wb = {}
        xp, fk, fv, fl, hg, ss, cv = _mixers(xp, lw, l, wb=wb, fox_past=None, hg_s0=None, ssm_s0=None,
                                             conv_s0=None, page_table=None)
        w_kv = jnp.concatenate([xa_wk[l], xa_wv[l]], axis=1).astype(BF16)
        kv = _matmul(mem2, w_kv, gain=norm_mem[l], name="mem_kv")
        mk = _headnorm(kv, xa_gk[l], col0=0, width=XA_W).reshape(bp, n_mem, XA_W)
        mv = kv[:, XA_W:].reshape(bp, n_mem, XA_W)
        xp = _cross_and_mlp(xp, lw, l, mk, mv, wb=wb)
        for key, val in zip(("p_fk", "p_fv", "p_fl", "p_hg", "p_ss", "p_cv"), (fk, fv, fl, hg, ss, cv)):
            outs[key].append(val)
        outs["p_mk"].append(mk.reshape(bp, n_mem, XA_HEADS, HEAD))
        outs["p_mv"].append(mv.reshape(bp, n_mem, XA_HEADS, HEAD))

        bs = xs.shape[0]
        xs, fk, fv, fl, hg, ss, cv = _mixers(xs, lw, l, wb=wb, fox_past=(cache_k, cache_v, cache_lf),
                                             hg_s0=state_hgrn[l], ssm_s0=state_ssm[l],
                                             conv_s0=state_conv[l], page_table=page_table)
        xs = _cross_and_mlp(xs, lw, l, cache_mem_k[l].reshape(bs, n_mem, XA_W),
                            cache_mem_v[l].reshape(bs, n_mem, XA_W), wb=wb)
        for key, val in zip(("s_fk", "s_fv", "s_fl", "s_hg", "s_ss", "s_cv"), (fk, fv, fl, hg, ss, cv)):
            outs[key].append(val)

    st = {k: jnp.stack(v) for k, v in outs.items()}
    return (xp, xs, st["p_fk"], st["p_fv"], st["p_fl"], st["p_hg"], st["p_ss"], st["p_cv"], st["p_mk"],
            st["p_mv"], st["s_fk"], st["s_fv"], st["s_fl"], st["s_hg"], st["s_ss"], st["s_cv"])
```

```python
import functools

import jax
import jax.numpy as jnp
from jax import lax
from jax.experimental import pallas as pl
from jax.experimental.pallas import tpu as pltpu

F32 = jnp.float32
BF16 = jnp.bfloat16

EPS = 1e-6
MASK_VALUE = -1e30
HEAD = 128
N_HEADS = 8
FOX_W = N_HEADS * HEAD
HG_W = N_HEADS * HEAD
SSM_W = 2048
SSM_P = 64
SSM_HEADS = SSM_W // SSM_P
SSM_GROUPS = 8
SSM_HPG = SSM_HEADS // SSM_GROUPS
SSM_N = 128
SSM_GW = SSM_HPG * SSM_P
CONV_W = 4
CONV_DIM = SSM_W + 2 * SSM_GROUPS * SSM_N
XA_HEADS = 4
XA_W = XA_HEADS * HEAD
HG_CHUNK = 64
HG_SUB = 16
SSD_CHUNK = 128
PAGE = 128
DT_LANE0 = N_HEADS

COL_XBC = 0
COL_HG = CONV_DIM
COL_Z = COL_HG + 4 * HG_W
COL_FQ = COL_Z + SSM_W
COL_FK = COL_FQ + FOX_W
COL_FV = COL_FK + FOX_W
MAIN_W = COL_FV + FOX_W

V7X_VMEM_LIMIT = 56 * 1024 * 1024


def _cparams(sem, vmem=V7X_VMEM_LIMIT):
    return pltpu.CompilerParams(dimension_semantics=sem, vmem_limit_bytes=vmem)


def _sigmoid(x):
    return 1.0 / (1.0 + jnp.exp(-x))


def _silu(x):
    h = 0.5 * x
    return h + h * jnp.tanh(h)


def _softplus(x):
    return jnp.maximum(x, 0.0) + jnp.log1p(jnp.exp(-jnp.abs(x)))


def _log_sigmoid(x):
    return -_softplus(-x)


def _dot_nt(a, b):
    return lax.dot_general(a, b, (((1,), (1,)), ((), ())), preferred_element_type=F32)


def _dot_tn(a, b):
    return lax.dot_general(a, b, (((0,), (0,)), ((), ())), preferred_element_type=F32)


def _dot(a, b):
    return jnp.dot(a, b, preferred_element_type=F32)


def _tri_incl(n):
    r = lax.broadcasted_iota(jnp.int32, (n, n), 0)
    c = lax.broadcasted_iota(jnp.int32, (n, n), 1)
    return r >= c


def _cumsum_rows(x):
    n = x.shape[0]
    tri = _tri_incl(n).astype(BF16)
    hi = x.astype(BF16)
    r1 = x - hi.astype(F32)
    mid = r1.astype(BF16)
    lo = (r1 - mid.astype(F32)).astype(BF16)
    return _dot(tri, hi) + _dot(tri, mid) + _dot(tri, lo)


def _side_cast_specs(side_casts, grid):
    n_steps = 1
    for g in grid:
        n_steps *= g

    def flat(*idx):
        step = idx[0]
        for g, i in zip(grid[1:], idx[1:]):
            step = step * g + i
        return step

    in_specs, args, out_specs, out_shapes = [], [], [], []
    for src, layer in side_casts:
        _, k2, n2 = src.shape
        rows = next(r for r in range(16, k2 + 1, 16) if k2 % r == 0 and k2 // r <= n_steps)
        last = k2 // rows - 1
        in_specs.append(pl.BlockSpec((None, rows, n2), lambda *idx, layer=layer, last=last:
                                     (layer, jnp.minimum(flat(*idx), last), 0)))
        args.append(src)
        out_specs.append(pl.BlockSpec((rows, n2), lambda *idx, last=last: (jnp.minimum(flat(*idx), last), 0)))
        out_shapes.append(jax.ShapeDtypeStruct((k2, n2), BF16))
    return in_specs, args, out_specs, out_shapes


def _do_side_casts(side_in, side_out):
    for s_in, s_out in zip(side_in, side_out):
        s_out[...] = s_in[...].astype(BF16)


def _mm_body(*refs, nk, norm, act, has_res, trans_w, n_sides, has_extra):
    it = iter(refs)
    a_ref = next(it)
    g_ref = next(it) if norm else None
    w_ref = next(it)
    we_ref = next(it) if has_extra else None
    r_ref = next(it) if has_res else None
    side_in = [next(it) for _ in range(n_sides)]
    o_ref = next(it)
    oe_ref = next(it) if has_extra else None
    side_out = [next(it) for _ in range(n_sides)]
    h_ref = next(it) if norm else None
    acc_ref = next(it) if nk > 1 else None
    j = pl.program_id(1)
    k = pl.program_id(2)

    if norm:
        @pl.when(j == 0)
        def _():
            x = a_ref[...].astype(F32)
            ms = jnp.mean(x * x, axis=-1, keepdims=True)
            h_ref[...] = (x * lax.rsqrt(ms + EPS) * g_ref[...]).astype(BF16)
            if has_extra:
                oe_ref[...] = _dot_nt(h_ref[...], we_ref[...])
        a = h_ref[...]
    else:
        a = a_ref[...]

    _do_side_casts(side_in, side_out)

    if trans_w:
        p = _dot_nt(a, w_ref[...])
    else:
        p = _dot(a, w_ref[...])

    def finish(v):
        if act == "relu2":
            v = jnp.square(jnp.maximum(v, 0.0))
        if has_res:
            v = v + r_ref[...]
        o_ref[...] = v.astype(o_ref.dtype)

    if nk == 1:
        finish(p)
    else:
        @pl.when(k == 0)
        def _():
            acc_ref[...] = p

        @pl.when(k > 0)
        def _():
            acc_ref[...] += p

        @pl.when(k == nk - 1)
        def _():
            finish(acc_ref[...])


def _pick_tile(n, candidates):
    for c in candidates:
        if n % c == 0:
            return c
    return n


def _matmul(a, w, *, gain=None, res=None, act=None, out_dtype=F32, tm_max=512, tn_max=None, tk=None,
            trans_w=False, side_casts=(), extra_w=None, name="matmul"):
    m, kdim = a.shape
    n = w.shape[0] if trans_w else w.shape[-1]
    small_m = m <= 64
    tm = m if small_m else _pick_tile(m, tuple(c for c in (1024, 512, 256, 128) if c <= tm_max))
    tn_max = tn_max or (2048 if small_m else 1024)
    tn = _pick_tile(n, tuple(c for c in (2048, 1024, 512, 256, 128) if c <= tn_max))
    tk = kdim if tk is None else tk
    nk = kdim // tk
    norm = gain is not None
    assert not (norm and nk > 1)
    assert extra_w is None or norm
    in_specs = [pl.BlockSpec((tm, tk), lambda i, j, k: (i, k))]
    args = [a]
    if norm:
        in_specs.append(pl.BlockSpec((1, kdim), lambda i, j, k: (0, 0)))
        args.append(gain.reshape(1, kdim).astype(F32))
    if trans_w:
        in_specs.append(pl.BlockSpec((tn, tk), lambda i, j, k: (j, k)))
    else:
        in_specs.append(pl.BlockSpec((tk, tn), lambda i, j, k: (k, j)))
    args.append(w)
    extra_specs, extra_shapes = [], []
    if extra_w is not None:
        ne = extra_w.shape[0]
        in_specs.append(pl.BlockSpec((ne, kdim), lambda i, j, k: (0, 0)))
        args.append(extra_w)
        extra_specs.append(pl.BlockSpec((tm, ne), lambda i, j, k: (i, 0)))
        extra_shapes.append(jax.ShapeDtypeStruct((m, ne), F32))
    if res is not None:
        in_specs.append(pl.BlockSpec((tm, tn), lambda i, j, k: (i, j)))
        args.append(res)
    grid = (m // tm, n // tn, nk)
    side_in_specs, side_args, side_specs, side_shapes = _side_cast_specs(side_casts, grid)
    in_specs += side_in_specs
    args += side_args
    scratch = []
    if norm:
        scratch.append(pltpu.VMEM((tm, kdim), BF16))
    if nk > 1:
        scratch.append(pltpu.VMEM((tm, tn), F32))
    out_specs = [pl.BlockSpec((tm, tn), lambda i, j, k: (i, j))] + extra_specs + side_specs
    out_shape = [jax.ShapeDtypeStruct((m, n), out_dtype)] + extra_shapes + side_shapes
    outs = pl.pallas_call(
        functools.partial(_mm_body, nk=nk, norm=norm, act=act, has_res=res is not None, trans_w=trans_w,
                          n_sides=len(side_casts), has_extra=extra_w is not None),
        grid=grid,
        in_specs=in_specs,
        out_specs=out_specs,
        out_shape=out_shape,
        scratch_shapes=scratch,
        compiler_params=_cparams(("arbitrary" if side_casts else "parallel", "arbitrary", "arbitrary")),
        name=name,
    )(*args)
    return outs if len(outs) > 1 else outs[0]


def _outproj_body(fo_ref, ho_ref, sy_ref, w_ref, r_ref, o_ref):
    k1 = FOX_W + HG_W
    p = (_dot(fo_ref[...], w_ref[0:FOX_W, :]) + _dot(ho_ref[...], w_ref[FOX_W:k1, :])
         + _dot(sy_ref[...], w_ref[k1:k1 + SSM_W, :]))
    o_ref[...] = p + r_ref[...]


def _out_proj(fo, ho, sy, w, res):
    m = fo.shape[0]
    kdim, n = w.shape
    tm = m if m <= 64 else _pick_tile(m, (1024, 512, 256, 128))
    tn = _pick_tile(n, (1024, 512, 256, 128))
    assert fo.shape[1] == FOX_W and ho.shape[1] == HG_W and sy.shape[1] == SSM_W
    return pl.pallas_call(
        _outproj_body,
        grid=(m // tm, n // tn),
        in_specs=[pl.BlockSpec((tm, FOX_W), lambda i, j: (i, 0)),
                  pl.BlockSpec((tm, HG_W), lambda i, j: (i, 0)),
                  pl.BlockSpec((tm, SSM_W), lambda i, j: (i, 0)),
                  pl.BlockSpec((kdim, tn), lambda i, j: (0, j)),
                  pl.BlockSpec((tm, tn), lambda i, j: (i, j))],
        out_specs=pl.BlockSpec((tm, tn), lambda i, j: (i, j)),
        out_shape=jax.ShapeDtypeStruct((m, n), F32),
        compiler_params=_cparams(("parallel", "arbitrary")),
        name="out_proj",
    )(fo, ho, sy, w, res)


def _proj_res_norm_body(a_ref, w_ref, r_ref, g_ref, x_ref, h_ref):
    x = _dot(a_ref[...], w_ref[...]) + r_ref[...]
    x_ref[...] = x
    ms = jnp.mean(x * x, axis=-1, keepdims=True)
    h_ref[...] = (x * lax.rsqrt(ms + EPS) * g_ref[...]).astype(BF16)


def _proj_res_norm(a, w, res, gain):
    m, kdim = a.shape
    n = w.shape[1]
    tm = m if m <= 64 else _pick_tile(m, (256, 128))
    row = lambda width: pl.BlockSpec((tm, width), lambda i: (i, 0))
    return pl.pallas_call(
        _proj_res_norm_body,
        grid=(m // tm,),
        in_specs=[row(kdim), pl.BlockSpec((kdim, n), lambda i: (0, 0)), row(n), pl.BlockSpec((1, n), lambda i: (0, 0))],
        out_specs=[row(n), row(n)],
        out_shape=[jax.ShapeDtypeStruct((m, n), F32), jax.ShapeDtypeStruct((m, n), BF16)],
        compiler_params=_cparams(("parallel",)),
        name="xa_out",
    )(a, w, res, gain.reshape(1, n).astype(F32))


def _headnorm_body(x_ref, g_ref, o_ref, *, n_heads):
    g = g_ref[...]
    for h in range(n_heads):
        sl = slice(h * HEAD, (h + 1) * HEAD)
        x = x_ref[:, sl].astype(F32)
        ms = jnp.mean(x * x, axis=-1, keepdims=True)
        o_ref[:, sl] = (x * lax.rsqrt(ms + EPS) * g).astype(o_ref.dtype)


def _headnorm(x, gain, *, col0, width, out_dtype=F32):
    m = x.shape[0]
    tm = m if m <= 64 else _pick_tile(m, (512, 256, 128))
    return pl.pallas_call(
        functools.partial(_headnorm_body, n_heads=width // HEAD),
        grid=(m // tm,),
        in_specs=[pl.BlockSpec((tm, width), lambda i: (i, col0 // width)),
                  pl.BlockSpec((1, HEAD), lambda i: (0, 0))],
        out_specs=pl.BlockSpec((tm, width), lambda i: (i, 0)),
        out_shape=jax.ShapeDtypeStruct((m, width), out_dtype),
        compiler_params=_cparams(("parallel",)),
        name="headnorm",
    )(x, gain.reshape(1, HEAD).astype(F32))


def _fox_gate_body(g_ref, bf_ref, lf_ref, cum_ref, *, t):
    lf = _log_sigmoid(g_ref[0] + bf_ref[...])
    lf_ref[0] = lf
    lane = lax.broadcasted_iota(jnp.int32, lf.shape, 1)
    c = lf
    shift = 1
    while shift < t:
        c = c + jnp.where(lane >= shift, pltpu.roll(c, shift, axis=1), 0.0)
        shift *= 2
    cum_ref[0] = c


def _fox_gate(gates_t, fox_bf):
    b, h, t = gates_t.shape
    spec = pl.BlockSpec((1, h, t), lambda i: (i, 0, 0))
    return pl.pallas_call(
        functools.partial(_fox_gate_body, t=t),
        grid=(b,),
        in_specs=[spec, pl.BlockSpec((h, 1), lambda i: (0, 0))],
        out_specs=[spec, spec],
        out_shape=[jax.ShapeDtypeStruct((b, h, t), F32)] * 2,
        compiler_params=_cparams(("parallel",)),
        name="fox_gate",
    )(gates_t, fox_bf.reshape(h, 1).astype(F32))


FLASH_STRIP = 256


def _fox_flash_body(q_ref, k_ref, v_ref, cq_ref, ck_ref, o_ref, m_ref, l_ref, acc_ref, *, tq):
    qi = pl.program_id(1)
    ki = pl.program_id(2)
    scale = HEAD ** -0.5

    @pl.when(ki == 0)
    def _():
        m_ref[...] = jnp.full(m_ref.shape, MASK_VALUE, F32)
        l_ref[...] = jnp.zeros(l_ref.shape, F32)
        acc_ref[...] = jnp.zeros(acc_ref.shape, F32)

    def update(diagonal):
        for r0 in range(0, tq, FLASH_STRIP):
            rows = slice(r0, r0 + FLASH_STRIP)
            if diagonal:
                keep = (r0 + lax.broadcasted_iota(jnp.int32, (FLASH_STRIP, tq), 0)
                        >= lax.broadcasted_iota(jnp.int32, (FLASH_STRIP, tq), 1))
            for h in range(N_HEADS):
                sl = slice(h * HEAD, (h + 1) * HEAD)
                s = _dot_nt(q_ref[0, rows, sl], k_ref[0, :, sl]) * scale
                s = s + cq_ref[0, rows, h:h + 1] - ck_ref[0, h:h + 1, :]
                if diagonal:
                    s = jnp.where(keep, s, MASK_VALUE)
                m_prev = m_ref[h, rows]
                m_new = jnp.maximum(m_prev, jnp.max(s, axis=-1, keepdims=True))
                alpha = jnp.exp(m_prev - m_new)
                p = jnp.exp(s - m_new)
                l_ref[h, rows] = alpha * l_ref[h, rows] + jnp.sum(p, axis=-1, keepdims=True)
                acc_ref[rows, sl] = alpha * acc_ref[rows, sl] + _dot(p.astype(BF16), v_ref[0, :, sl])
                m_ref[h, rows] = m_new

    @pl.when(ki < qi)
    def _():
        update(diagonal=False)

    @pl.when(ki == qi)
    def _():
        update(diagonal=True)
        for h in range(N_HEADS):
            sl = slice(h * HEAD, (h + 1) * HEAD)
            o_ref[0, :, sl] = (acc_ref[:, sl] / l_ref[h]).astype(o_ref.dtype)


def _fox_prep_body(q_ref, k_ref, v_ref, gq_ref, gk_ref, qn_ref, kn_ref, knb_ref, vf_ref, vb_ref):
    for h in range(N_HEADS):
        sl = slice(h * HEAD, (h + 1) * HEAD)
        q = q_ref[:, sl]
        qn_ref[:, sl] = (q * lax.rsqrt(jnp.mean(q * q, axis=-1, keepdims=True) + EPS) * gq_ref[...]).astype(BF16)
        k = k_ref[:, sl]
        kn = k * lax.rsqrt(jnp.mean(k * k, axis=-1, keepdims=True) + EPS) * gk_ref[...]
        kn_ref[:, sl] = kn
        knb_ref[:, sl] = kn.astype(BF16)
    v = v_ref[...]
    vf_ref[...] = v
    vb_ref[...] = v.astype(BF16)


def _fox_prep(proj2, gq, gk):
    m = proj2.shape[0]
    tm = _pick_tile(m, (512, 256, 128))
    col = lambda c: pl.BlockSpec((tm, FOX_W), lambda i: (i, c // FOX_W))
    out = pl.BlockSpec((tm, FOX_W), lambda i: (i, 0))
    gain = pl.BlockSpec((1, HEAD), lambda i: (0, 0))
    return pl.pallas_call(
        _fox_prep_body,
        grid=(m // tm,),
        in_specs=[col(COL_FQ), col(COL_FK), col(COL_FV), gain, gain],
        out_specs=[out] * 5,
        out_shape=[jax.ShapeDtypeStruct((m, FOX_W), dt) for dt in (BF16, F32, BF16, F32, BF16)],
        compiler_params=_cparams(("parallel",)),
        name="fox_prep",
    )(proj2, proj2, proj2, gq.reshape(1, HEAD).astype(F32), gk.reshape(1, HEAD).astype(F32))


def _fox_flash(qn, kn, vb, cum_col, cum_row, *, tq=512):
    b, t, _ = qn.shape
    nq = t // tq
    return pl.pallas_call(
        functools.partial(_fox_flash_body, tq=tq),
        grid=(b, nq, nq),
        in_specs=[
            pl.BlockSpec((1, tq, FOX_W), lambda i, q, k: (i, q, 0)),
            pl.BlockSpec((1, tq, FOX_W), lambda i, q, k: (i, jnp.minimum(k, q), 0)),
            pl.BlockSpec((1, tq, FOX_W), lambda i, q, k: (i, jnp.minimum(k, q), 0)),
            pl.BlockSpec((1, tq, N_HEADS), lambda i, q, k: (i, q, 0)),
            pl.BlockSpec((1, N_HEADS, tq), lambda i, q, k: (i, 0, jnp.minimum(k, q))),
        ],
        out_specs=pl.BlockSpec((1, tq, FOX_W), lambda i, q, k: (i, q, 0)),
        out_shape=jax.ShapeDtypeStruct((b, t, FOX_W), BF16),
        scratch_shapes=[pltpu.VMEM((N_HEADS, tq, 1), F32), pltpu.VMEM((N_HEADS, tq, 1), F32),
                        pltpu.VMEM((tq, FOX_W), F32)],
        compiler_params=_cparams(("parallel", "parallel", "arbitrary")),
        name="fox_flash",
    )(qn, kn, vb, cum_col, cum_row)


def _fox_decode_body(pt_ref, q_ref, kn_ref, vn_ref, lfn_ref, *rest, pages_per_step, n_steps):
    del pt_ref
    pps = pages_per_step
    k_refs = rest[:pps]
    v_refs = rest[pps:2 * pps]
    lf_refs = rest[2 * pps:3 * pps]
    o_ref, m_ref, l_ref, acc_ref, carry_ref = rest[3 * pps:]
    s = pl.program_id(1)
    scale = HEAD ** -0.5
    rows = PAGE * N_HEADS
    lane = lax.broadcasted_iota(jnp.int32, (1, rows), 1)
    own = (lax.broadcasted_iota(jnp.int32, (N_HEADS, rows), 1) % N_HEADS
           == lax.broadcasted_iota(jnp.int32, (N_HEADS, rows), 0))

    @pl.when(s == 0)
    def _():
        m_ref[...] = jnp.sum(q_ref[0] * kn_ref[0], axis=-1, keepdims=True) * scale
        l_ref[...] = jnp.ones(l_ref.shape, F32)
        acc_ref[...] = vn_ref[0]
        carry_ref[...] = jnp.zeros(carry_ref.shape, F32)

    lf = jnp.concatenate([lf_refs[r][0, 0] for r in range(pps)], axis=0)
    tot = lf
    suf = lf
    shift = N_HEADS
    while shift < rows:
        tot = tot + pltpu.roll(tot, shift, axis=1)
        suf = suf + jnp.where(lane + shift < rows, pltpu.roll(suf, rows - shift, axis=1), 0.0)
        shift *= 2
    page = lax.broadcasted_iota(jnp.int32, (pps, 1), 0)
    newer = tot
    shift = 1
    while shift < pps:
        newer = newer + jnp.where(page >= shift, pltpu.roll(newer, shift, axis=0), 0.0)
        shift *= 2
    after = suf - lf + (newer - tot) + carry_ref[...]
    carry_ref[...] = carry_ref[...] + newer[pps - 1:pps, :]

    q = q_ref[0].astype(BF16)
    scs = []
    for r in range(pps):
        sc = _dot_nt(q, k_refs[r][0, 0].astype(BF16)) * scale + lfn_ref[0] + after[r:r + 1, :]
        scs.append(jnp.where(own, sc, MASK_VALUE))
    m_prev = m_ref[...]
    m_new = m_prev
    for sc in scs:
        m_new = jnp.maximum(m_new, jnp.max(sc, axis=-1, keepdims=True))
    alpha = jnp.exp(m_prev - m_new)
    l_new = alpha * l_ref[...]
    acc = alpha * acc_ref[...]
    for r, sc in enumerate(scs):
        p = jnp.exp(sc - m_new)
        l_new = l_new + jnp.sum(p, axis=-1, keepdims=True)
        acc = acc + _dot(p.astype(BF16), v_refs[r][0, 0].astype(BF16))
    l_ref[...] = l_new
    acc_ref[...] = acc
    m_ref[...] = m_new

    @pl.when(s == n_steps - 1)
    def _():
        o_ref[0] = (acc_ref[...] / l_ref[...]).astype(o_ref.dtype)


def _fox_decode(layer, qn, kn, v_new, lf_new, cache_k, cache_v, cache_lf, page_table, *, pages_per_step=16):
    b = qn.shape[0]
    n_pages = page_table.shape[1]
    pps = pages_per_step
    n_steps = n_pages // pps
    rows = PAGE * N_HEADS

    def page_map(r):
        return lambda i, s, pt: (layer, pt[i, n_pages - 1 - (s * pps + r)], 0, 0)

    head = pl.BlockSpec((1, N_HEADS, HEAD), lambda i, s, pt: (i, 0, 0))
    in_specs = [head, head, head, pl.BlockSpec((1, N_HEADS, 1), lambda i, s, pt: (i, 0, 0))]
    in_specs += [pl.BlockSpec((1, 1, rows, HEAD), page_map(r)) for r in range(pps)]
    in_specs += [pl.BlockSpec((1, 1, rows, HEAD), page_map(r)) for r in range(pps)]
    in_specs += [pl.BlockSpec((1, 1, 1, rows), page_map(r)) for r in range(pps)]
    grid_spec = pltpu.PrefetchScalarGridSpec(
        num_scalar_prefetch=1, grid=(b, n_steps), in_specs=in_specs,
        out_specs=pl.BlockSpec((1, N_HEADS, HEAD), lambda i, s, pt: (i, 0, 0)),
        scratch_shapes=[pltpu.VMEM((N_HEADS, 1), F32), pltpu.VMEM((N_HEADS, 1), F32),
                        pltpu.VMEM((N_HEADS, HEAD), F32), pltpu.VMEM((1, rows), F32)])
    return pl.pallas_call(
        functools.partial(_fox_decode_body, pages_per_step=pps, n_steps=n_steps),
        grid_spec=grid_spec,
        out_shape=jax.ShapeDtypeStruct((b, N_HEADS, HEAD), BF16),
        compiler_params=_cparams(("parallel", "arbitrary")),
        name="fox_decode",
    )(page_table, qn, kn, v_new, lf_new, *([cache_k] * pps), *([cache_v] * pps), *([cache_lf] * pps))


def _rows_of_chunk(ref, cols, chunk_rows):
    x = ref[0, :, cols].astype(F32)
    if x.shape[0] == chunk_rows:
        return x
    assert x.shape[0] == 1
    return jnp.broadcast_to(x, (chunk_rows, x.shape[1]))


def _hgrn_body(*refs, layer, t_valid, has_s0):
    it = iter(refs)
    x_ref = next(it)
    lbl_ref = next(it)
    gn_ref = next(it)
    s0_ref = next(it) if has_s0 else None
    o_ref = next(it)
    st_ref = next(it)
    q_s, k_s, v_s, b_s, stt_s = (next(it) for _ in range(5))
    c = pl.program_id(1)
    nc = pl.num_programs(1)
    cs = HG_CHUNK
    t_blk = o_ref.shape[1]

    @pl.when(c == 0)
    def _():
        for h in range(N_HEADS):
            if has_s0:
                stt_s[h] = s0_ref[0, h].astype(F32).T
            else:
                stt_s[h] = jnp.zeros((HEAD, HEAD), F32)

    lg = lbl_ref[...].astype(F32)
    e = jnp.exp(lg - jnp.max(lg, axis=0, keepdims=True))
    pr = e / jnp.sum(e, axis=0, keepdims=True)
    lb = jnp.sum(pr[0:layer + 1], axis=0, keepdims=True) - pr[0:1]

    hf = _rows_of_chunk(x_ref, slice(HG_W, 2 * HG_W), cs)
    logf = jnp.log(lb + (1.0 - lb) * _sigmoid(hf))
    kk = (1.0 - lb) * _sigmoid(-hf)
    qq = _silu(_rows_of_chunk(x_ref, slice(0, HG_W), cs))
    vv = _rows_of_chunk(x_ref, slice(2 * HG_W, 3 * HG_W), cs)
    if t_valid is not None:
        live = (c * cs + lax.broadcasted_iota(jnp.int32, (cs, 1), 0)) < t_valid
        logf = jnp.where(live, logf, 0.0)
        kk = jnp.where(live, kk, 0.0)
        qq = jnp.where(live, qq, 0.0)
        vv = jnp.where(live, vv, 0.0)
    q_s[...] = qq
    k_s[...] = kk
    v_s[...] = vv
    b_s[...] = _cumsum_rows(logf)

    n_sub = cs // HG_SUB
    causal = _tri_incl(cs)
    zeros_sub = jnp.zeros((HG_SUB, HEAD), F32)
    g = gn_ref[...]

    atts = []
    for h in range(N_HEADS):
        sl = slice(h * HEAD, (h + 1) * HEAD)
        q_parts, k_parts = [], []
        for i in range(n_sub):
            r0 = i * HG_SUB
            n_keys = r0 + HG_SUB
            b_i = b_s[r0 - 1:r0, sl] if i > 0 else jnp.zeros((1, HEAD), F32)
            qd = q_s[r0:n_keys, sl] * jnp.exp(b_s[r0:n_keys, sl] - b_i)
            kd = k_s[0:n_keys, sl] * jnp.exp(b_i - b_s[0:n_keys, sl])
            q_parts.append(jnp.concatenate([zeros_sub] * i + [qd] + [zeros_sub] * (n_sub - 1 - i), axis=0))
            k_parts.append(jnp.concatenate([kd] + [zeros_sub] * (n_sub - 1 - i), axis=0))
        q_cat = jnp.concatenate(q_parts, axis=1).astype(BF16)
        k_cat = jnp.concatenate(k_parts, axis=1).astype(BF16)
        atts.append(jnp.where(causal, _dot_nt(q_cat, k_cat), 0.0).astype(BF16))

    for h in range(N_HEADS):
        sl = slice(h * HEAD, (h + 1) * HEAD)
        stt = stt_s[h]
        bh = b_s[:, sl]
        b_end = b_s[cs - 1:cs, sl]
        o = _dot(atts[h], v_s[:, sl].astype(BF16))
        o = o + _dot_nt((q_s[:, sl] * jnp.exp(bh)).astype(BF16), stt.astype(BF16))
        kd_end = k_s[:, sl] * jnp.exp(b_end - bh)
        stt_s[h] = stt * jnp.exp(b_end) + _dot_tn(v_s[:, sl].astype(BF16), kd_end.astype(BF16))
        ms = jnp.mean(o * o, axis=-1, keepdims=True)
        gate = _silu(_rows_of_chunk(x_ref, slice(3 * HG_W + h * HEAD, 3 * HG_W + (h + 1) * HEAD), cs))
        res = o * lax.rsqrt(ms + EPS) * g * gate
        o_ref[0, :, sl] = res[0:t_blk].astype(o_ref.dtype)

    @pl.when(c == nc - 1)
    def _():
        for h in range(N_HEADS):
            st_ref[0, h] = stt_s[h].T


def _hgrn(proj, lb_logits, gnorm, s0, *, layer):
    b, t, _ = proj.shape
    cs = HG_CHUNK
    assert t == 1 or t % cs == 0
    t_blk, n_chunks, t_valid = (1, 1, 1) if t == 1 else (cs, t // cs, None)
    has_s0 = s0 is not None
    depth = lb_logits.shape[0]
    in_specs = [pl.BlockSpec((1, t_blk, 4 * HG_W), lambda i, c: (i, c, COL_HG // (4 * HG_W))),
                pl.BlockSpec((depth, HG_W), lambda i, c: (0, 0)),
                pl.BlockSpec((1, HEAD), lambda i, c: (0, 0))]
    args = [proj, lb_logits.astype(F32), gnorm.reshape(1, HEAD).astype(F32)]
    if has_s0:
        in_specs.append(pl.BlockSpec((1, N_HEADS, HEAD, HEAD), lambda i, c: (i, 0, 0, 0)))
        args.append(s0)
    return pl.pallas_call(
        functools.partial(_hgrn_body, layer=layer, t_valid=t_valid, has_s0=has_s0),
        grid=(b, n_chunks),
        in_specs=in_specs,
        out_specs=[pl.BlockSpec((1, t_blk, HG_W), lambda i, c: (i, c, 0)),
                   pl.BlockSpec((1, N_HEADS, HEAD, HEAD), lambda i, c: (i, 0, 0, 0))],
        out_shape=[jax.ShapeDtypeStruct((b, t, HG_W), BF16),
                   jax.ShapeDtypeStruct((b, N_HEADS, HEAD, HEAD), F32)],
        scratch_shapes=[pltpu.VMEM((cs, HG_W), F32)] * 4 + [pltpu.VMEM((N_HEADS, HEAD, HEAD), F32)],
        compiler_params=_cparams(("parallel", "arbitrary")),
        name="hgrn2",
    )(*args)


def _lane_blocks(cols, width):
    rows = cols[0].shape[0]
    lane = lax.broadcasted_iota(jnp.int32, (rows, width * len(cols)), 1)
    out = jnp.broadcast_to(cols[-1], (rows, width * len(cols)))
    for j in range(len(cols) - 2, -1, -1):
        out = jnp.where(lane < (j + 1) * width, cols[j], out)
    return out


def _ssd_body(*refs, t_valid, has_s0, n_sides):
    it = iter(refs)
    xbc_ref = next(it)
    z_ref = next(it)
    dts_ref = next(it)
    cw_ref = next(it)
    cb_ref = next(it)
    dtb_ref = next(it)
    alog_ref = next(it)
    dsk_ref = next(it)
    gn_ref = next(it)
    cs0_ref = next(it) if has_s0 else None
    s0_ref = next(it) if has_s0 else None
    side_in = [next(it) for _ in range(n_sides)]
    y_ref = next(it)
    st_ref = next(it)
    side_out = [next(it) for _ in range(n_sides)]
    prev_s = next(it)
    act_s = next(it)
    c = pl.program_id(1)
    cs = SSD_CHUNK
    t_blk = y_ref.shape[1]
    tail0 = 8 - (CONV_W - 1)

    @pl.when(c == 0)
    def _():
        prev_s[...] = jnp.zeros(prev_s.shape, F32)
        if has_s0:
            prev_s[tail0:8, :] = cs0_ref[0].astype(F32)
            st_ref[...] = s0_ref[...].astype(F32)
        else:
            st_ref[...] = jnp.zeros(st_ref.shape, F32)

    _do_side_casts(side_in, side_out)

    cur = _rows_of_chunk(xbc_ref, slice(0, CONV_DIM), cs)
    win = jnp.concatenate([prev_s[...], cur[0:8]], axis=0)
    conv = cb_ref[...] + cur * cw_ref[CONV_W - 1:CONV_W, :]
    head = cb_ref[...] + win[8:16] * cw_ref[CONV_W - 1:CONV_W, :]
    for d in range(1, CONV_W):
        w_d = cw_ref[CONV_W - 1 - d:CONV_W - d, :]
        conv = conv + pltpu.roll(cur, d, axis=0) * w_d
        head = head + pltpu.roll(win, d, axis=0)[8:16] * w_d
    act_s[...] = _silu(conv)
    act_s[0:8, :] = _silu(head)
    prev_s[...] = cur[cs - 8:cs]

    dt = _softplus(_rows_of_chunk(dts_ref, slice(0, 128), cs) + dtb_ref[...])
    if t_valid is not None:
        live = (c * cs + lax.broadcasted_iota(jnp.int32, (cs, 1), 0)) < t_valid
        dt = jnp.where(live, dt, 0.0)
    a = _cumsum_rows(dt * (-jnp.exp(alog_ref[...])))
    a_t = a.T
    dt_t = dt.T
    a_end = a[cs - 1:cs, :]
    w_upd = dt * jnp.exp(a_end - a)
    e_a = jnp.exp(a)
    e_end = jnp.exp(a_end)
    causal = _tri_incl(cs)
    lane_g = lax.broadcasted_iota(jnp.int32, (cs, SSM_GW), 1)
    row_g = lax.broadcasted_iota(jnp.int32, (SSM_GW, 1), 0)

    for g in range(SSM_GROUPS):
        xs = slice(g * SSM_GW, (g + 1) * SSM_GW)
        bsl = slice(SSM_W + g * SSM_N, SSM_W + (g + 1) * SSM_N)
        csl = slice(SSM_W + SSM_GROUPS * SSM_N + g * SSM_N, SSM_W + SSM_GROUPS * SSM_N + (g + 1) * SSM_N)
        xg = act_s[:, xs]
        bm = act_s[:, bsl].astype(BF16)
        cm = act_s[:, csl].astype(BF16)
        cbm = jnp.where(causal, _dot_nt(cm, bm), 0.0)
        s_g = st_ref[0, g]
        lanes = [DT_LANE0 + g * SSM_HPG + j for j in range(SSM_HPG)]
        yg = _dot_nt(cm, s_g.astype(BF16)) * _lane_blocks([e_a[:, ln:ln + 1] for ln in lanes], SSM_P)
        for j, ln in enumerate(lanes):
            diff = a[:, ln:ln + 1] - a_t[ln:ln + 1, :]
            mh = cbm * jnp.exp(jnp.minimum(diff, 0.0)) * dt_t[ln:ln + 1, :]
            xm = jnp.where(jnp.logical_and(lane_g >= j * SSM_P, lane_g < (j + 1) * SSM_P), xg, 0.0)
            yg = yg + _dot(mh.astype(BF16), xm.astype(BF16))
        xw = xg * _lane_blocks([w_upd[:, ln:ln + 1] for ln in lanes], SSM_P)
        decay = jnp.broadcast_to(e_end[:, lanes[-1]:lanes[-1] + 1], (SSM_GW, 1))
        for j in range(SSM_HPG - 2, -1, -1):
            decay = jnp.where(row_g < (j + 1) * SSM_P, e_end[:, lanes[j]:lanes[j] + 1], decay)
        st_ref[0, g] = s_g * decay + _dot_tn(xw.astype(BF16), bm)

        y = yg + dsk_ref[:, xs] * xg
        y = y * _silu(_rows_of_chunk(z_ref, xs, cs))
        ms = jnp.mean(y * y, axis=-1, keepdims=True)
        res = y * lax.rsqrt(ms + EPS) * gn_ref[:, xs]
        y_ref[0, :, xs] = res[0:t_blk].astype(y_ref.dtype)


def _ssd(proj, small, conv_w, conv_b, dt_bias, a_log, d_skip, gnorm, conv_s0, s0, *, side_casts=()):
    b, t, _ = proj.shape
    cs = SSD_CHUNK
    assert t == 1 or t % cs == 0
    t_blk, n_chunks, t_valid = (1, 1, 1) if t == 1 else (cs, t // cs, None)
    has_s0 = s0 is not None
    pad = jnp.zeros((DT_LANE0,), F32)
    tail = jnp.zeros((128 - DT_LANE0 - SSM_HEADS,), F32)
    on_dt_lanes = lambda v: jnp.concatenate([pad, v.astype(F32), tail]).reshape(1, 128)
    const = lambda shape: pl.BlockSpec(shape, lambda i, c: (0,) * len(shape))
    in_specs = [pl.BlockSpec((1, t_blk, CONV_DIM), lambda i, c: (i, c, COL_XBC // CONV_DIM)),
                pl.BlockSpec((1, t_blk, SSM_W), lambda i, c: (i, c, COL_Z // SSM_W)),
                pl.BlockSpec((1, t_blk, 128), lambda i, c: (i, c, 0)),
                const((CONV_W, CONV_DIM)), const((1, CONV_DIM)), const((1, 128)), const((1, 128)),
                const((1, SSM_W)), const((1, SSM_W))]
    args = [proj, proj, small, conv_w.astype(F32), conv_b.reshape(1, CONV_DIM).astype(F32),
            on_dt_lanes(dt_bias), on_dt_lanes(a_log),
            jnp.repeat(d_skip.astype(F32), SSM_P).reshape(1, SSM_W), gnorm.reshape(1, SSM_W).astype(F32)]
    if has_s0:
        in_specs += [pl.BlockSpec((1, CONV_W - 1, CONV_DIM), lambda i, c: (i, 0, 0)),
                     pl.BlockSpec((1, SSM_GROUPS, SSM_GW, SSM_N), lambda i, c: (i, 0, 0, 0))]
        args += [conv_s0, s0.reshape(b, SSM_GROUPS, SSM_GW, SSM_N)]
    grid = (b, n_chunks)
    side_in_specs, side_args, side_specs, side_shapes = _side_cast_specs(side_casts, grid)
    y, st, *copies = pl.pallas_call(
        functools.partial(_ssd_body, t_valid=t_valid, has_s0=has_s0, n_sides=len(side_casts)),
        grid=grid,
        in_specs=in_specs + side_in_specs,
        out_specs=[pl.BlockSpec((1, t_blk, SSM_W), lambda i, c: (i, c, 0)),
                   pl.BlockSpec((1, SSM_GROUPS, SSM_GW, SSM_N), lambda i, c: (i, 0, 0, 0))] + side_specs,
        out_shape=[jax.ShapeDtypeStruct((b, t, SSM_W), BF16),
                   jax.ShapeDtypeStruct((b, SSM_GROUPS, SSM_GW, SSM_N), F32)] + side_shapes,
        scratch_shapes=[pltpu.VMEM((8, CONV_DIM), F32), pltpu.VMEM((cs, CONV_DIM), F32)],
        compiler_params=_cparams(("arbitrary" if side_casts else "parallel", "arbitrary")),
        name="ssd",
    )(*(args + side_args))
    return (y, st.reshape(b, SSM_HEADS, SSM_P, SSM_N), *copies)


def _xattn_body(q_ref, k_ref, v_ref, gq_ref, o_ref):
    scale = HEAD ** -0.5
    g = gq_ref[...]
    for h in range(XA_HEADS):
        sl = slice(h * HEAD, (h + 1) * HEAD)
        q = q_ref[0, :, sl].astype(F32)
        ms = jnp.mean(q * q, axis=-1, keepdims=True)
        qn = q * lax.rsqrt(ms + EPS) * g
        s = _dot_nt(qn.astype(BF16), k_ref[0, :, sl].astype(BF16)) * scale
        p = jnp.exp(s - jnp.max(s, axis=-1, keepdims=True))
        o = _dot(p.astype(BF16), v_ref[0, :, sl].astype(BF16)) / jnp.sum(p, axis=-1, keepdims=True)
        o_ref[0, :, sl] = o.astype(o_ref.dtype)


def _xattn(q, mk, mv, gq):
    b, t, _ = q.shape
    n_mem = mk.shape[1]
    tq = t if t <= 512 else 512
    return pl.pallas_call(
        _xattn_body,
        grid=(b, t // tq),
        in_specs=[pl.BlockSpec((1, tq, XA_W), lambda i, j: (i, j, 0)),
                  pl.BlockSpec((1, n_mem, XA_W), lambda i, j: (i, 0, 0)),
                  pl.BlockSpec((1, n_mem, XA_W), lambda i, j: (i, 0, 0)),
                  pl.BlockSpec((1, HEAD), lambda i, j: (0, 0))],
        out_specs=pl.BlockSpec((1, tq, XA_W), lambda i, j: (i, j, 0)),
        out_shape=jax.ShapeDtypeStruct((b, t, XA_W), BF16),
        compiler_params=_cparams(("parallel", "parallel")),
        name="xattn",
    )(q, mk, mv, gq.reshape(1, HEAD).astype(F32))


IN_PIECES = (("fq", FOX_W), ("fk", FOX_W), ("fv", FOX_W), ("fg", N_HEADS), ("hg", 4 * HG_W), ("z", SSM_W),
             ("xbc", CONV_DIM), ("dt", SSM_HEADS))
IN_WIDTH = sum(width for _, width in IN_PIECES)
MAIN_COLS = dict(xbc=COL_XBC, hg=COL_HG, z=COL_Z, fq=COL_FQ, fk=COL_FK, fv=COL_FV)
W_TILE = 1024
SUBLANES = 8


def _in_piece_offsets():
    offs, o = {}, 0
    for name, width in IN_PIECES:
        offs[name] = (o, width)
        o += width
    return offs


def _w_main_tables():
    offs = _in_piece_offsets()
    a_blk, b_blk, shift = [0] * (MAIN_W // W_TILE), [0] * (MAIN_W // W_TILE), [0] * (MAIN_W // W_TILE)
    for name, dst in MAIN_COLS.items():
        src_off, width = offs[name]
        assert dst % W_TILE == 0 and width % W_TILE == 0
        for i in range(width // W_TILE):
            start = src_off + i * W_TILE
            t = dst // W_TILE + i
            a_blk[t], shift[t] = start // W_TILE, start % W_TILE
            assert shift[t] in (0, SUBLANES)
            b_blk[t] = (start - shift[t] + W_TILE) // SUBLANES if shift[t] else 0
    return a_blk, b_blk, shift


def _w_main_body(tbl_ref, a_ref, b_ref, o_ref):
    j = pl.program_id(0)

    @pl.when(tbl_ref[2, j] == 0)
    def _():
        o_ref[...] = a_ref[...].astype(BF16)

    @pl.when(tbl_ref[2, j] != 0)
    def _():
        o_ref[...] = jnp.concatenate([a_ref[SUBLANES:W_TILE, :], b_ref[...]], axis=0).astype(BF16)


def _w_small_body(g_ref, *rest):
    d_refs, o_ref = rest[:-1], rest[-1]
    used = SUBLANES * (1 + len(d_refs))
    pad = jnp.zeros((o_ref.shape[0] - used, o_ref.shape[1]), F32)
    o_ref[...] = jnp.concatenate([g_ref[...]] + [d_ref[...] for d_ref in d_refs] + [pad], axis=0).astype(BF16)


def _prep_w_in(w_in_t, layer):
    depth, width, d = w_in_t.shape
    assert width == IN_WIDTH and N_HEADS == SUBLANES
    offs = _in_piece_offsets()
    a_blk, b_blk, shift = _w_main_tables()
    tc = _pick_tile(d, (1024, 512, 256, 128))
    tables = jnp.asarray([a_blk, b_blk, shift], jnp.int32)
    main = pl.pallas_call(
        _w_main_body,
        grid_spec=pltpu.PrefetchScalarGridSpec(
            num_scalar_prefetch=1, grid=(MAIN_W // W_TILE, d // tc),
            in_specs=[pl.BlockSpec((None, W_TILE, tc), lambda j, c, tbl: (layer, tbl[0, j], c)),
                      pl.BlockSpec((None, SUBLANES, tc), lambda j, c, tbl: (layer, tbl[1, j], c))],
            out_specs=pl.BlockSpec((W_TILE, tc), lambda j, c, tbl: (j, c))),
        out_shape=jax.ShapeDtypeStruct((MAIN_W, d), BF16),
        compiler_params=_cparams(("parallel", "arbitrary")),
        name="w_in_main",
    )(tables, w_in_t, w_in_t)

    g_off, (d_off, d_rows) = offs["fg"][0], offs["dt"]
    assert g_off % SUBLANES == 0 and d_off % SUBLANES == 0 and d_rows % SUBLANES == 0
    rows8 = lambda off: pl.BlockSpec((None, SUBLANES, tc), lambda c: (layer, off // SUBLANES, c))
    n_dt = d_rows // SUBLANES
    small = pl.pallas_call(
        _w_small_body,
        grid=(d // tc,),
        in_specs=[rows8(g_off)] + [rows8(d_off + i * SUBLANES) for i in range(n_dt)],
        out_specs=pl.BlockSpec((128, tc), lambda c: (0, c)),
        out_shape=jax.ShapeDtypeStruct((128, d), BF16),
        compiler_params=_cparams(("parallel",)),
        name="w_in_small",
    )(*([w_in_t] * (1 + n_dt)))
    return main, small


def _mixers(x, lw, l, *, wb, fox_past, hg_s0, ssm_s0, conv_s0, page_table):
    b, t, d = x.shape
    x2 = x.reshape(b * t, d)
    proj2, small2 = _matmul(x2, lw["w_main"], gain=lw["norm_mix"], trans_w=True, extra_w=lw["w_small"],
                            name="in_proj")
    proj = proj2.reshape(b, t, MAIN_W)
    small = small2.reshape(b, t, 128)

    gates_t = jnp.swapaxes(small[:, :, 0:N_HEADS], 1, 2)
    logf_t, cum_t = _fox_gate(gates_t, lw["fox_bf"])
    logf = jnp.swapaxes(logf_t, 1, 2)

    if fox_past is None:
        qn, kn, knb, fv, vb = _fox_prep(proj2, lw["fox_gq"], lw["fox_gk"])
        fo = _fox_flash(qn.reshape(b, t, FOX_W), knb.reshape(b, t, FOX_W), vb.reshape(b, t, FOX_W),
                        jnp.swapaxes(cum_t, 1, 2), cum_t)
    else:
        qn = _headnorm(proj2, lw["fox_gq"], col0=COL_FQ, width=FOX_W)
        kn = _headnorm(proj2, lw["fox_gk"], col0=COL_FK, width=FOX_W)
        fv = proj2[:, COL_FV:COL_FV + FOX_W]
        cache_k, cache_v, cache_lf = fox_past
        fo = _fox_decode(l, qn.reshape(b, N_HEADS, HEAD), kn.reshape(b, N_HEADS, HEAD),
                         fv.reshape(b, N_HEADS, HEAD), logf_t, cache_k, cache_v, cache_lf, page_table)
        fo = fo.reshape(b, t, FOX_W)

    ho, hg_state = _hgrn(proj, lw["hg_lb_logits"], lw["hg_gnorm"], hg_s0, layer=l)
    ssd_args = (proj, small, lw["conv_w"], lw["conv_b"], lw["dt_bias"], lw["a_log"], lw["d_skip"],
                lw["ssm_gnorm"], conv_s0, ssm_s0)
    if "w_out" in wb:
        sy, ssm_state = _ssd(*ssd_args)
    else:
        sy, ssm_state, wb["w_out"], wb["w_up"] = _ssd(*ssd_args, side_casts=((lw["w_out"], l), (lw["w_up"], l)))
    x_new = _out_proj(fo.reshape(b * t, FOX_W), ho.reshape(b * t, HG_W), sy.reshape(b * t, SSM_W),
                      wb["w_out"], x2).reshape(b, t, d)

    keep = CONV_W - 1
    if t >= keep:
        conv_state = proj[:, t - keep:, COL_XBC:COL_XBC + CONV_DIM]
    else:
        prev = jnp.zeros((b, keep, CONV_DIM), F32) if conv_s0 is None else conv_s0.astype(F32)
        conv_state = jnp.concatenate([prev[:, t:], proj[:, :, COL_XBC:COL_XBC + CONV_DIM]], axis=1)
    fk = kn.reshape(b, t, N_HEADS, HEAD)
    return x_new, fk, fv.reshape(b, t, N_HEADS, HEAD), logf, hg_state, ssm_state, conv_state


def _cross_and_mlp(x, lw, l, mk, mv, *, wb):
    b, t, d = x.shape
    x2 = x.reshape(b * t, d)
    q = _matmul(x2, lw["xa_wq"], gain=lw["norm_xa"], name="xa_q")
    o = _xattn(q.reshape(b, t, XA_W), mk, mv, lw["xa_gq"])
    x2, h = _proj_res_norm(o.reshape(b * t, XA_W), lw["xa_wo"], x2, lw["norm_mlp"])
    if "w_down" in wb:
        u = _matmul(h, wb["w_up"], act="relu2", out_dtype=BF16, tm_max=1024, name="mlp_up")
    else:
        u, wb["w_down"] = _matmul(h, wb["w_up"], act="relu2", out_dtype=BF16, tm_max=1024,
                                  side_casts=((lw["w_down"], l),), name="mlp_up")
    x2 = _matmul(u, wb["w_down"], res=x2, tm_max=1024, tk=2048, name="mlp_down")
    return x2.reshape(b, t, d)


def kernel(x_prompt, x_sample, cache_fox_k, cache_fox_v, cache_fox_logf, cache_mem_k, cache_mem_v, state_hgrn, state_ssm, state_conv, page_table, mem_prompt, norm_mix, w_in, fox_gq, fox_gk, fox_bf, hg_lb_logits, hg_gnorm, conv_w, conv_b, dt_bias, a_log, d_skip, ssm_gnorm, w_out, norm_xa, norm_mem, xa_wq, xa_wk, xa_wv, xa_gq, xa_gk, xa_wo, norm_mlp, w_up, w_down):
    depth = w_in.shape[0]
    bp = x_prompt.shape[0]
    n_mem = mem_prompt.shape[1]
    pool = cache_fox_k.shape[1]
    cache_k = cache_fox_k.reshape(depth, pool, PAGE * N_HEADS, HEAD)
    cache_v = cache_fox_v.reshape(depth, pool, PAGE * N_HEADS, HEAD)
    cache_lf = cache_fox_logf.reshape(depth, pool, 1, PAGE * N_HEADS)
    mem2 = mem_prompt.reshape(bp * n_mem, -1)
    w_in_t = jnp.swapaxes(w_in, 1, 2)

    xp, xs = x_prompt, x_sample
    outs = {k: [] for k in ("p_fk", "p_fv", "p_fl", "p_hg", "p_ss", "p_cv", "p_mk", "p_mv",
                            "s_fk", "s_fv", "s_fl", "s_hg", "s_ss", "s_cv")}
    for l in range(depth):
        w_main, w_small = _prep_w_in(w_in_t, l)
        lw = dict(w_main=w_main, w_small=w_small, norm_mix=norm_mix[l],
                  fox_gq=fox_gq[l], fox_gk=fox_gk[l], fox_bf=fox_bf[l], hg_lb_logits=hg_lb_logits,
                  hg_gnorm=hg_gnorm[l], conv_w=conv_w[l], conv_b=conv_b[l], dt_bias=dt_bias[l], a_log=a_log[l],
                  d_skip=d_skip[l], ssm_gnorm=ssm_gnorm[l], w_out=w_out, norm_xa=norm_xa[l],
                  xa_wq=xa_wq[l].astype(BF16), xa_gq=xa_gq[l], xa_wo=xa_wo[l].astype(BF16),
                  norm_mlp=norm_mlp[l], w_up=w_up, w_down=w_down)

        wb = {}
        xp, fk, fv, fl, hg, ss, cv = _mixers(xp, lw, l, wb=wb, fox_past=None, hg_s0=None, ssm_s0=None,
                                             conv_s0=None, page_table=None)
        w_kv = jnp.concatenate([xa_wk[l], xa_wv[l]], axis=1).astype(BF16)
        kv = _matmul(mem2, w_kv, gain=norm_mem[l], name="mem_kv")
        mk = _headnorm(kv, xa_gk[l], col0=0, width=XA_W).reshape(bp, n_mem, XA_W)
        mv = kv[:, XA_W:].reshape(bp, n_mem, XA_W)
        xp = _cross_and_mlp(xp, lw, l, mk, mv, wb=wb)
        for key, val in zip(("p_fk", "p_fv", "p_fl", "p_hg", "p_ss", "p_cv"), (fk, fv, fl, hg, ss, cv)):
            outs[key].append(val)
        outs["p_mk"].append(mk.reshape(bp, n_mem, XA_HEADS, HEAD))
        outs["p_mv"].append(mv.reshape(bp, n_mem, XA_HEADS, HEAD))

        bs = xs.shape[0]
        xs, fk, fv, fl, hg, ss, cv = _mixers(xs, lw, l, wb=wb, fox_past=(cache_k, cache_v, cache_lf),
                                             hg_s0=state_hgrn[l], ssm_s0=state_ssm[l],
                                             conv_s0=state_conv[l], page_table=page_table)
        xs = _cross_and_mlp(xs, lw, l, cache_mem_k[l].reshape(bs, n_mem, XA_W),
                            cache_mem_v[l].reshape(bs, n_mem, XA_W), wb=wb)
        for key, val in zip(("s_fk", "s_fv", "s_fl", "s_hg", "s_ss", "s_cv"), (fk, fv, fl, hg, ss, cv)):
            outs[key].append(val)

    st = {k: jnp.stack(v) for k, v in outs.items()}
    return (xp, xs, st["p_fk"], st["p_fv"], st["p_fl"], st["p_hg"], st["p_ss"], st["p_cv"], st["p_mk"],
            st["p_mv"], st["s_fk"], st["s_fv"], st["s_fl"], st["s_hg"], st["s_ss"], st["s_cv"])
```

```python
import functools

import jax
import jax.numpy as jnp
from jax import lax
from jax.experimental import pallas as pl
from jax.experimental.pallas import tpu as pltpu

F32 = jnp.float32
BF16 = jnp.bfloat16

EPS = 1e-6
MASK_VALUE = -1e30
HEAD = 128
N_HEADS = 8
FOX_W = N_HEADS * HEAD
HG_W = N_HEADS * HEAD
SSM_W = 2048
SSM_P = 64
SSM_HEADS = SSM_W // SSM_P
SSM_GROUPS = 8
SSM_HPG = SSM_HEADS // SSM_GROUPS
SSM_N = 128
SSM_GW = SSM_HPG * SSM_P
CONV_W = 4
CONV_DIM = SSM_W + 2 * SSM_GROUPS * SSM_N
XA_HEADS = 4
XA_W = XA_HEADS * HEAD
HG_CHUNK = 64
HG_SUB = 16
SSD_CHUNK = 128
PAGE = 128
DT_LANE0 = N_HEADS

COL_XBC = 0
COL_HG = CONV_DIM
COL_Z = COL_HG + 4 * HG_W
COL_FQ = COL_Z + SSM_W
COL_FK = COL_FQ + FOX_W
COL_FV = COL_FK + FOX_W
MAIN_W = COL_FV + FOX_W

V7X_VMEM_LIMIT = 56 * 1024 * 1024


def _cparams(sem, vmem=V7X_VMEM_LIMIT):
    return pltpu.CompilerParams(dimension_semantics=sem, vmem_limit_bytes=vmem)


def _sigmoid(x):
    return 1.0 / (1.0 + jnp.exp(-x))


def _silu(x):
    h = 0.5 * x
    return h + h * jnp.tanh(h)


def _softplus(x):
    return jnp.maximum(x, 0.0) + jnp.log1p(jnp.exp(-jnp.abs(x)))


def _log_sigmoid(x):
    return -_softplus(-x)


def _dot_nt(a, b):
    return lax.dot_general(a, b, (((1,), (1,)), ((), ())), preferred_element_type=F32)


def _dot_tn(a, b):
    return lax.dot_general(a, b, (((0,), (0,)), ((), ())), preferred_element_type=F32)


def _dot(a, b):
    return jnp.dot(a, b, preferred_element_type=F32)


def _tri_incl(n):
    r = lax.broadcasted_iota(jnp.int32, (n, n), 0)
    c = lax.broadcasted_iota(jnp.int32, (n, n), 1)
    return r >= c


def _cumsum_rows(x):
    n = x.shape[0]
    tri = _tri_incl(n).astype(BF16)
    hi = x.astype(BF16)
    r1 = x - hi.astype(F32)
    mid = r1.astype(BF16)
    lo = (r1 - mid.astype(F32)).astype(BF16)
    return _dot(tri, hi) + _dot(tri, mid) + _dot(tri, lo)


def _side_cast_specs(side_casts, grid):
    n_steps = 1
    for g in grid:
        n_steps *= g

    def flat(*idx):
        step = idx[0]
        for g, i in zip(grid[1:], idx[1:]):
            step = step * g + i
        return step

    in_specs, args, out_specs, out_shapes = [], [], [], []
    for src, layer in side_casts:
        _, k2, n2 = src.shape
        rows = next(r for r in range(16, k2 + 1, 16) if k2 % r == 0 and k2 // r <= n_steps)
        last = k2 // rows - 1
        in_specs.append(pl.BlockSpec((None, rows, n2), lambda *idx, layer=layer, last=last:
                                     (layer, jnp.minimum(flat(*idx), last), 0)))
        args.append(src)
        out_specs.append(pl.BlockSpec((rows, n2), lambda *idx, last=last: (jnp.minimum(flat(*idx), last), 0)))
        out_shapes.append(jax.ShapeDtypeStruct((k2, n2), BF16))
    return in_specs, args, out_specs, out_shapes


def _do_side_casts(side_in, side_out):
    for s_in, s_out in zip(side_in, side_out):
        s_out[...] = s_in[...].astype(BF16)


def _mm_body(*refs, nk, norm, act, has_res, trans_w, n_sides, has_extra):
    it = iter(refs)
    a_ref = next(it)
    g_ref = next(it) if norm else None
    w_ref = next(it)
    we_ref = next(it) if has_extra else None
    r_ref = next(it) if has_res else None
    side_in = [next(it) for _ in range(n_sides)]
    o_ref = next(it)
    oe_ref = next(it) if has_extra else None
    side_out = [next(it) for _ in range(n_sides)]
    h_ref = next(it) if norm else None
    acc_ref = next(it) if nk > 1 else None
    j = pl.program_id(1)
    k = pl.program_id(2)

    if norm:
        @pl.when(j == 0)
        def _():
            x = a_ref[...].astype(F32)
            ms = jnp.mean(x * x, axis=-1, keepdims=True)
            h_ref[...] = (x * lax.rsqrt(ms + EPS) * g_ref[...]).astype(BF16)
            if has_extra:
                oe_ref[...] = _dot_nt(h_ref[...], we_ref[...])
        a = h_ref[...]
    else:
        a = a_ref[...]
        if has_extra:
            @pl.when(j == 0)
            def _():
                oe_ref[...] = _dot_nt(a_ref[...], we_ref[...])

    _do_side_casts(side_in, side_out)

    if trans_w:
        p = _dot_nt(a, w_ref[...])
    else:
        p = _dot(a, w_ref[...])

    def finish(v):
        if act == "relu2":
            v = jnp.square(jnp.maximum(v, 0.0))
        if has_res:
            v = v + r_ref[...]
        o_ref[...] = v.astype(o_ref.dtype)

    if nk == 1:
        finish(p)
    else:
        @pl.when(k == 0)
        def _():
            acc_ref[...] = p

        @pl.when(k > 0)
        def _():
            acc_ref[...] += p

        @pl.when(k == nk - 1)
        def _():
            finish(acc_ref[...])


def _pick_tile(n, candidates):
    for c in candidates:
        if n % c == 0:
            return c
    return n


def _matmul(a, w, *, gain=None, res=None, act=None, out_dtype=F32, tm_max=512, tn_max=None, tk=None,
            trans_w=False, side_casts=(), extra_w=None, name="matmul"):
    m, kdim = a.shape
    n = w.shape[0] if trans_w else w.shape[-1]
    small_m = m <= 64
    tm = m if small_m else _pick_tile(m, tuple(c for c in (1024, 512, 256, 128) if c <= tm_max))
    tn_max = tn_max or (2048 if small_m else 1024)
    tn = _pick_tile(n, tuple(c for c in (2048, 1024, 512, 256, 128) if c <= tn_max))
    tk = kdim if tk is None else tk
    nk = kdim // tk
    norm = gain is not None
    assert not (norm and nk > 1)
    assert extra_w is None or nk == 1
    in_specs = [pl.BlockSpec((tm, tk), lambda i, j, k: (i, k))]
    args = [a]
    if norm:
        in_specs.append(pl.BlockSpec((1, kdim), lambda i, j, k: (0, 0)))
        args.append(gain.reshape(1, kdim).astype(F32))
    if trans_w:
        in_specs.append(pl.BlockSpec((tn, tk), lambda i, j, k: (j, k)))
    else:
        in_specs.append(pl.BlockSpec((tk, tn), lambda i, j, k: (k, j)))
    args.append(w)
    extra_specs, extra_shapes = [], []
    if extra_w is not None:
        ne = extra_w.shape[0]
        in_specs.append(pl.BlockSpec((ne, kdim), lambda i, j, k: (0, 0)))
        args.append(extra_w)
        extra_specs.append(pl.BlockSpec((tm, ne), lambda i, j, k: (i, 0)))
        extra_shapes.append(jax.ShapeDtypeStruct((m, ne), F32))
    if res is not None:
        in_specs.append(pl.BlockSpec((tm, tn), lambda i, j, k: (i, j)))
        args.append(res)
    grid = (m // tm, n // tn, nk)
    side_in_specs, side_args, side_specs, side_shapes = _side_cast_specs(side_casts, grid)
    in_specs += side_in_specs
    args += side_args
    scratch = []
    if norm:
        scratch.append(pltpu.VMEM((tm, kdim), BF16))
    if nk > 1:
        scratch.append(pltpu.VMEM((tm, tn), F32))
    out_specs = [pl.BlockSpec((tm, tn), lambda i, j, k: (i, j))] + extra_specs + side_specs
    out_shape = [jax.ShapeDtypeStruct((m, n), out_dtype)] + extra_shapes + side_shapes
    outs = pl.pallas_call(
        functools.partial(_mm_body, nk=nk, norm=norm, act=act, has_res=res is not None, trans_w=trans_w,
                          n_sides=len(side_casts), has_extra=extra_w is not None),
        grid=grid,
        in_specs=in_specs,
        out_specs=out_specs,
        out_shape=out_shape,
        scratch_shapes=scratch,
        compiler_params=_cparams(("arbitrary" if side_casts else "parallel", "arbitrary", "arbitrary")),
        name=name,
    )(*args)
    return outs if len(outs) > 1 else outs[0]


def _outproj_body(fo_ref, ho_ref, sy_ref, w_ref, r_ref, o_ref):
    k1 = FOX_W + HG_W
    p = (_dot(fo_ref[...], w_ref[0:FOX_W, :]) + _dot(ho_ref[...], w_ref[FOX_W:k1, :])
         + _dot(sy_ref[...], w_ref[k1:k1 + SSM_W, :]))
    o_ref[...] = p + r_ref[...]


def _out_proj(fo, ho, sy, w, res):
    m = fo.shape[0]
    kdim, n = w.shape
    tm = m if m <= 64 else _pick_tile(m, (1024, 512, 256, 128))
    tn = _pick_tile(n, (1024, 512, 256, 128))
    assert fo.shape[1] == FOX_W and ho.shape[1] == HG_W and sy.shape[1] == SSM_W
    return pl.pallas_call(
        _outproj_body,
        grid=(m // tm, n // tn),
        in_specs=[pl.BlockSpec((tm, FOX_W), lambda i, j: (i, 0)),
                  pl.BlockSpec((tm, HG_W), lambda i, j: (i, 0)),
                  pl.BlockSpec((tm, SSM_W), lambda i, j: (i, 0)),
                  pl.BlockSpec((kdim, tn), lambda i, j: (0, j)),
                  pl.BlockSpec((tm, tn), lambda i, j: (i, j))],
        out_specs=pl.BlockSpec((tm, tn), lambda i, j: (i, j)),
        out_shape=jax.ShapeDtypeStruct((m, n), F32),
        compiler_params=_cparams(("parallel", "arbitrary")),
        name="out_proj",
    )(fo, ho, sy, w, res)


def _proj_res_norm_body(a_ref, w_ref, r_ref, g_ref, x_ref, h_ref):
    x = _dot(a_ref[...], w_ref[...]) + r_ref[...]
    x_ref[...] = x
    ms = jnp.mean(x * x, axis=-1, keepdims=True)
    h_ref[...] = (x * lax.rsqrt(ms + EPS) * g_ref[...]).astype(BF16)


def _proj_res_norm(a, w, res, gain):
    m, kdim = a.shape
    n = w.shape[1]
    tm = m if m <= 64 else _pick_tile(m, (256, 128))
    row = lambda width: pl.BlockSpec((tm, width), lambda i: (i, 0))
    return pl.pallas_call(
        _proj_res_norm_body,
        grid=(m // tm,),
        in_specs=[row(kdim), pl.BlockSpec((kdim, n), lambda i: (0, 0)), row(n), pl.BlockSpec((1, n), lambda i: (0, 0))],
        out_specs=[row(n), row(n)],
        out_shape=[jax.ShapeDtypeStruct((m, n), F32), jax.ShapeDtypeStruct((m, n), BF16)],
        compiler_params=_cparams(("parallel",)),
        name="xa_out",
    )(a, w, res, gain.reshape(1, n).astype(F32))


def _rmsnorm_rows_body(x_ref, g_ref, h_ref):
    x = x_ref[...].astype(F32)
    ms = jnp.mean(x * x, axis=-1, keepdims=True)
    h_ref[...] = (x * lax.rsqrt(ms + EPS) * g_ref[...]).astype(BF16)


def _rmsnorm_rows(x, gain):
    m, n = x.shape
    tm = m if m <= 64 else _pick_tile(m, (256, 128))
    return pl.pallas_call(
        _rmsnorm_rows_body,
        grid=(m // tm,),
        in_specs=[pl.BlockSpec((tm, n), lambda i: (i, 0)), pl.BlockSpec((1, n), lambda i: (0, 0))],
        out_specs=pl.BlockSpec((tm, n), lambda i: (i, 0)),
        out_shape=jax.ShapeDtypeStruct((m, n), BF16),
        compiler_params=_cparams(("parallel",)),
        name="rmsnorm_rows",
    )(x, gain.reshape(1, n).astype(F32))


def _headnorm_body(x_ref, g_ref, o_ref, *, n_heads):
    g = g_ref[...]
    for h in range(n_heads):
        sl = slice(h * HEAD, (h + 1) * HEAD)
        x = x_ref[:, sl].astype(F32)
        ms = jnp.mean(x * x, axis=-1, keepdims=True)
        o_ref[:, sl] = (x * lax.rsqrt(ms + EPS) * g).astype(o_ref.dtype)


def _headnorm(x, gain, *, col0, width, out_dtype=F32):
    m = x.shape[0]
    tm = m if m <= 64 else _pick_tile(m, (512, 256, 128))
    return pl.pallas_call(
        functools.partial(_headnorm_body, n_heads=width // HEAD),
        grid=(m // tm,),
        in_specs=[pl.BlockSpec((tm, width), lambda i: (i, col0 // width)),
                  pl.BlockSpec((1, HEAD), lambda i: (0, 0))],
        out_specs=pl.BlockSpec((tm, width), lambda i: (i, 0)),
        out_shape=jax.ShapeDtypeStruct((m, width), out_dtype),
        compiler_params=_cparams(("parallel",)),
        name="headnorm",
    )(x, gain.reshape(1, HEAD).astype(F32))


def _fox_gate_body(g_ref, bf_ref, lf_ref, cum_ref, *, t):
    lf = _log_sigmoid(g_ref[0] + bf_ref[...])
    lf_ref[0] = lf
    lane = lax.broadcasted_iota(jnp.int32, lf.shape, 1)
    c = lf
    shift = 1
    while shift < t:
        c = c + jnp.where(lane >= shift, pltpu.roll(c, shift, axis=1), 0.0)
        shift *= 2
    cum_ref[0] = c


def _fox_gate(gates_t, fox_bf):
    b, h, t = gates_t.shape
    spec = pl.BlockSpec((1, h, t), lambda i: (i, 0, 0))
    return pl.pallas_call(
        functools.partial(_fox_gate_body, t=t),
        grid=(b,),
        in_specs=[spec, pl.BlockSpec((h, 1), lambda i: (0, 0))],
        out_specs=[spec, spec],
        out_shape=[jax.ShapeDtypeStruct((b, h, t), F32)] * 2,
        compiler_params=_cparams(("parallel",)),
        name="fox_gate",
    )(gates_t, fox_bf.reshape(h, 1).astype(F32))


FLASH_STRIP = 256


def _fox_flash_body(q_ref, k_ref, v_ref, cq_ref, ck_ref, o_ref, m_ref, l_ref, acc_ref, *, tq):
    qi = pl.program_id(1)
    ki = pl.program_id(2)
    scale = HEAD ** -0.5

    @pl.when(ki == 0)
    def _():
        m_ref[...] = jnp.full(m_ref.shape, MASK_VALUE, F32)
        l_ref[...] = jnp.zeros(l_ref.shape, F32)
        acc_ref[...] = jnp.zeros(acc_ref.shape, F32)

    def update(diagonal):
        for r0 in range(0, tq, FLASH_STRIP):
            rows = slice(r0, r0 + FLASH_STRIP)
            if diagonal:
                keep = (r0 + lax.broadcasted_iota(jnp.int32, (FLASH_STRIP, tq), 0)
                        >= lax.broadcasted_iota(jnp.int32, (FLASH_STRIP, tq), 1))
            for h in range(N_HEADS):
                sl = slice(h * HEAD, (h + 1) * HEAD)
                s = _dot_nt(q_ref[0, rows, sl], k_ref[0, :, sl]) * scale
                s = s + cq_ref[0, rows, h:h + 1] - ck_ref[0, h:h + 1, :]
                if diagonal:
                    s = jnp.where(keep, s, MASK_VALUE)
                m_prev = m_ref[h, rows]
                m_new = jnp.maximum(m_prev, jnp.max(s, axis=-1, keepdims=True))
                alpha = jnp.exp(m_prev - m_new)
                p = jnp.exp(s - m_new)
                l_ref[h, rows] = alpha * l_ref[h, rows] + jnp.sum(p, axis=-1, keepdims=True)
                acc_ref[rows, sl] = alpha * acc_ref[rows, sl] + _dot(p.astype(BF16), v_ref[0, :, sl])
                m_ref[h, rows] = m_new

    @pl.when(ki < qi)
    def _():
        update(diagonal=False)

    @pl.when(ki == qi)
    def _():
        update(diagonal=True)
        for h in range(N_HEADS):
            sl = slice(h * HEAD, (h + 1) * HEAD)
            o_ref[0, :, sl] = (acc_ref[:, sl] / l_ref[h]).astype(o_ref.dtype)


def _fox_prep_body(q_ref, k_ref, v_ref, gq_ref, gk_ref, qn_ref, kn_ref, knb_ref, vf_ref, vb_ref):
    for h in range(N_HEADS):
        sl = slice(h * HEAD, (h + 1) * HEAD)
        q = q_ref[:, sl]
        qn_ref[:, sl] = (q * lax.rsqrt(jnp.mean(q * q, axis=-1, keepdims=True) + EPS) * gq_ref[...]).astype(BF16)
        k = k_ref[:, sl]
        kn = k * lax.rsqrt(jnp.mean(k * k, axis=-1, keepdims=True) + EPS) * gk_ref[...]
        kn_ref[:, sl] = kn
        knb_ref[:, sl] = kn.astype(BF16)
    v = v_ref[...]
    vf_ref[...] = v
    vb_ref[...] = v.astype(BF16)


def _fox_prep(proj2, gq, gk):
    m = proj2.shape[0]
    tm = _pick_tile(m, (512, 256, 128))
    col = lambda c: pl.BlockSpec((tm, FOX_W), lambda i: (i, c // FOX_W))
    out = pl.BlockSpec((tm, FOX_W), lambda i: (i, 0))
    gain = pl.BlockSpec((1, HEAD), lambda i: (0, 0))
    return pl.pallas_call(
        _fox_prep_body,
        grid=(m // tm,),
        in_specs=[col(COL_FQ), col(COL_FK), col(COL_FV), gain, gain],
        out_specs=[out] * 5,
        out_shape=[jax.ShapeDtypeStruct((m, FOX_W), dt) for dt in (BF16, F32, BF16, F32, BF16)],
        compiler_params=_cparams(("parallel",)),
        name="fox_prep",
    )(proj2, proj2, proj2, gq.reshape(1, HEAD).astype(F32), gk.reshape(1, HEAD).astype(F32))


def _fox_flash(qn, kn, vb, cum_col, cum_row, *, tq=512):
    b, t, _ = qn.shape
    nq = t // tq
    return pl.pallas_call(
        functools.partial(_fox_flash_body, tq=tq),
        grid=(b, nq, nq),
        in_specs=[
            pl.BlockSpec((1, tq, FOX_W), lambda i, q, k: (i, q, 0)),
            pl.BlockSpec((1, tq, FOX_W), lambda i, q, k: (i, jnp.minimum(k, q), 0)),
            pl.BlockSpec((1, tq, FOX_W), lambda i, q, k: (i, jnp.minimum(k, q), 0)),
            pl.BlockSpec((1, tq, N_HEADS), lambda i, q, k: (i, q, 0)),
            pl.BlockSpec((1, N_HEADS, tq), lambda i, q, k: (i, 0, jnp.minimum(k, q))),
        ],
        out_specs=pl.BlockSpec((1, tq, FOX_W), lambda i, q, k: (i, q, 0)),
        out_shape=jax.ShapeDtypeStruct((b, t, FOX_W), BF16),
        scratch_shapes=[pltpu.VMEM((N_HEADS, tq, 1), F32), pltpu.VMEM((N_HEADS, tq, 1), F32),
                        pltpu.VMEM((tq, FOX_W), F32)],
        compiler_params=_cparams(("parallel", "parallel", "arbitrary")),
        name="fox_flash",
    )(qn, kn, vb, cum_col, cum_row)


def _fox_decode_body(pt_ref, q_ref, kn_ref, vn_ref, lfn_ref, *rest, pages_per_step, n_steps):
    del pt_ref
    pps = pages_per_step
    k_refs = rest[:pps]
    v_refs = rest[pps:2 * pps]
    lf_refs = rest[2 * pps:3 * pps]
    o_ref, m_ref, l_ref, acc_ref, carry_ref = rest[3 * pps:]
    s = pl.program_id(1)
    scale = HEAD ** -0.5
    rows = PAGE * N_HEADS
    lane = lax.broadcasted_iota(jnp.int32, (1, rows), 1)
    own = (lax.broadcasted_iota(jnp.int32, (N_HEADS, rows), 1) % N_HEADS
           == lax.broadcasted_iota(jnp.int32, (N_HEADS, rows), 0))

    @pl.when(s == 0)
    def _():
        m_ref[...] = jnp.sum(q_ref[0] * kn_ref[0], axis=-1, keepdims=True) * scale
        l_ref[...] = jnp.ones(l_ref.shape, F32)
        acc_ref[...] = vn_ref[0]
        carry_ref[...] = jnp.zeros(carry_ref.shape, F32)

    lf = jnp.concatenate([lf_refs[r][0, 0] for r in range(pps)], axis=0)
    tot = lf
    suf = lf
    shift = N_HEADS
    while shift < rows:
        tot = tot + pltpu.roll(tot, shift, axis=1)
        suf = suf + jnp.where(lane + shift < rows, pltpu.roll(suf, rows - shift, axis=1), 0.0)
        shift *= 2
    page = lax.broadcasted_iota(jnp.int32, (pps, 1), 0)
    newer = tot
    shift = 1
    while shift < pps:
        newer = newer + jnp.where(page >= shift, pltpu.roll(newer, shift, axis=0), 0.0)
        shift *= 2
    after = suf - lf + (newer - tot) + carry_ref[...]
    carry_ref[...] = carry_ref[...] + newer[pps - 1:pps, :]

    q = q_ref[0].astype(BF16)
    scs = []
    for r in range(pps):
        sc = _dot_nt(q, k_refs[r][0, 0].astype(BF16)) * scale + lfn_ref[0] + after[r:r + 1, :]
        scs.append(jnp.where(own, sc, MASK_VALUE))
    m_prev = m_ref[...]
    m_new = m_prev
    for sc in scs:
        m_new = jnp.maximum(m_new, jnp.max(sc, axis=-1, keepdims=True))
    alpha = jnp.exp(m_prev - m_new)
    l_new = alpha * l_ref[...]
    acc = alpha * acc_ref[...]
    for r, sc in enumerate(scs):
        p = jnp.exp(sc - m_new)
        l_new = l_new + jnp.sum(p, axis=-1, keepdims=True)
        acc = acc + _dot(p.astype(BF16), v_refs[r][0, 0].astype(BF16))
    l_ref[...] = l_new
    acc_ref[...] = acc
    m_ref[...] = m_new

    @pl.when(s == n_steps - 1)
    def _():
        o_ref[0] = (acc_ref[...] / l_ref[...]).astype(o_ref.dtype)


def _fox_decode(layer, qn, kn, v_new, lf_new, cache_k, cache_v, cache_lf, page_table, *, pages_per_step=16):
    b = qn.shape[0]
    n_pages = page_table.shape[1]
    pps = pages_per_step
    n_steps = n_pages // pps
    rows = PAGE * N_HEADS

    def page_map(r):
        return lambda i, s, pt: (layer, pt[i, n_pages - 1 - (s * pps + r)], 0, 0)

    head = pl.BlockSpec((1, N_HEADS, HEAD), lambda i, s, pt: (i, 0, 0))
    in_specs = [head, head, head, pl.BlockSpec((1, N_HEADS, 1), lambda i, s, pt: (i, 0, 0))]
    in_specs += [pl.BlockSpec((1, 1, rows, HEAD), page_map(r)) for r in range(pps)]
    in_specs += [pl.BlockSpec((1, 1, rows, HEAD), page_map(r)) for r in range(pps)]
    in_specs += [pl.BlockSpec((1, 1, 1, rows), page_map(r)) for r in range(pps)]
    grid_spec = pltpu.PrefetchScalarGridSpec(
        num_scalar_prefetch=1, grid=(b, n_steps), in_specs=in_specs,
        out_specs=pl.BlockSpec((1, N_HEADS, HEAD), lambda i, s, pt: (i, 0, 0)),
        scratch_shapes=[pltpu.VMEM((N_HEADS, 1), F32), pltpu.VMEM((N_HEADS, 1), F32),
                        pltpu.VMEM((N_HEADS, HEAD), F32), pltpu.VMEM((1, rows), F32)])
    return pl.pallas_call(
        functools.partial(_fox_decode_body, pages_per_step=pps, n_steps=n_steps),
        grid_spec=grid_spec,
        out_shape=jax.ShapeDtypeStruct((b, N_HEADS, HEAD), BF16),
        compiler_params=_cparams(("parallel", "arbitrary")),
        name="fox_decode",
    )(page_table, qn, kn, v_new, lf_new, *([cache_k] * pps), *([cache_v] * pps), *([cache_lf] * pps))


def _rows_of_chunk(ref, cols, chunk_rows):
    x = ref[0, :, cols].astype(F32)
    if x.shape[0] == chunk_rows:
        return x
    assert x.shape[0] == 1
    return jnp.broadcast_to(x, (chunk_rows, x.shape[1]))


def _hgrn_body(*refs, layer, t_valid, has_s0):
    it = iter(refs)
    x_ref = next(it)
    lbl_ref = next(it)
    gn_ref = next(it)
    s0_ref = next(it) if has_s0 else None
    o_ref = next(it)
    st_ref = next(it)
    q_s, k_s, v_s, b_s, stt_s = (next(it) for _ in range(5))
    c = pl.program_id(1)
    nc = pl.num_programs(1)
    cs = HG_CHUNK
    t_blk = o_ref.shape[1]

    @pl.when(c == 0)
    def _():
        for h in range(N_HEADS):
            if has_s0:
                stt_s[h] = s0_ref[0, h].astype(F32).T
            else:
                stt_s[h] = jnp.zeros((HEAD, HEAD), F32)

    lg = lbl_ref[...].astype(F32)
    e = jnp.exp(lg - jnp.max(lg, axis=0, keepdims=True))
    pr = e / jnp.sum(e, axis=0, keepdims=True)
    lb = jnp.sum(pr[0:layer + 1], axis=0, keepdims=True) - pr[0:1]

    hf = _rows_of_chunk(x_ref, slice(HG_W, 2 * HG_W), cs)
    logf = jnp.log(lb + (1.0 - lb) * _sigmoid(hf))
    kk = (1.0 - lb) * _sigmoid(-hf)
    qq = _silu(_rows_of_chunk(x_ref, slice(0, HG_W), cs))
    vv = _rows_of_chunk(x_ref, slice(2 * HG_W, 3 * HG_W), cs)
    if t_valid is not None:
        live = (c * cs + lax.broadcasted_iota(jnp.int32, (cs, 1), 0)) < t_valid
        logf = jnp.where(live, logf, 0.0)
        kk = jnp.where(live, kk, 0.0)
        qq = jnp.where(live, qq, 0.0)
        vv = jnp.where(live, vv, 0.0)
    q_s[...] = qq
    k_s[...] = kk
    v_s[...] = vv
    b_s[...] = _cumsum_rows(logf)

    n_sub = cs // HG_SUB
    causal = _tri_incl(cs)
    zeros_sub = jnp.zeros((HG_SUB, HEAD), F32)
    g = gn_ref[...]

    atts = []
    for h in range(N_HEADS):
        sl = slice(h * HEAD, (h + 1) * HEAD)
        q_parts, k_parts = [], []
        for i in range(n_sub):
            r0 = i * HG_SUB
            n_keys = r0 + HG_SUB
            b_i = b_s[r0 - 1:r0, sl] if i > 0 else jnp.zeros((1, HEAD), F32)
            qd = q_s[r0:n_keys, sl] * jnp.exp(b_s[r0:n_keys, sl] - b_i)
            kd = k_s[0:n_keys, sl] * jnp.exp(b_i - b_s[0:n_keys, sl])
            q_parts.append(jnp.concatenate([zeros_sub] * i + [qd] + [zeros_sub] * (n_sub - 1 - i), axis=0))
            k_parts.append(jnp.concatenate([kd] + [zeros_sub] * (n_sub - 1 - i), axis=0))
        q_cat = jnp.concatenate(q_parts, axis=1).astype(BF16)
        k_cat = jnp.concatenate(k_parts, axis=1).astype(BF16)
        atts.append(jnp.where(causal, _dot_nt(q_cat, k_cat), 0.0).astype(BF16))

    for h in range(N_HEADS):
        sl = slice(h * HEAD, (h + 1) * HEAD)
        stt = stt_s[h]
        bh = b_s[:, sl]
        b_end = b_s[cs - 1:cs, sl]
        o = _dot(atts[h], v_s[:, sl].astype(BF16))
        o = o + _dot_nt((q_s[:, sl] * jnp.exp(bh)).astype(BF16), stt.astype(BF16))
        kd_end = k_s[:, sl] * jnp.exp(b_end - bh)
        stt_s[h] = stt * jnp.exp(b_end) + _dot_tn(v_s[:, sl].astype(BF16), kd_end.astype(BF16))
        ms = jnp.mean(o * o, axis=-1, keepdims=True)
        gate = _silu(_rows_of_chunk(x_ref, slice(3 * HG_W + h * HEAD, 3 * HG_W + (h + 1) * HEAD), cs))
        res = o * lax.rsqrt(ms + EPS) * g * gate
        o_ref[0, :, sl] = res[0:t_blk].astype(o_ref.dtype)

    @pl.when(c == nc - 1)
    def _():
        for h in range(N_HEADS):
            st_ref[0, h] = stt_s[h].T


def _hgrn(proj, lb_logits, gnorm, s0, *, layer):
    b, t, _ = proj.shape
    cs = HG_CHUNK
    assert t == 1 or t % cs == 0
    t_blk, n_chunks, t_valid = (1, 1, 1) if t == 1 else (cs, t // cs, None)
    has_s0 = s0 is not None
    depth = lb_logits.shape[0]
    in_specs = [pl.BlockSpec((1, t_blk, 4 * HG_W), lambda i, c: (i, c, COL_HG // (4 * HG_W))),
                pl.BlockSpec((depth, HG_W), lambda i, c: (0, 0)),
                pl.BlockSpec((1, HEAD), lambda i, c: (0, 0))]
    args = [proj, lb_logits.astype(F32), gnorm.reshape(1, HEAD).astype(F32)]
    if has_s0:
        in_specs.append(pl.BlockSpec((1, N_HEADS, HEAD, HEAD), lambda i, c: (i, 0, 0, 0)))
        args.append(s0)
    return pl.pallas_call(
        functools.partial(_hgrn_body, layer=layer, t_valid=t_valid, has_s0=has_s0),
        grid=(b, n_chunks),
        in_specs=in_specs,
        out_specs=[pl.BlockSpec((1, t_blk, HG_W), lambda i, c: (i, c, 0)),
                   pl.BlockSpec((1, N_HEADS, HEAD, HEAD), lambda i, c: (i, 0, 0, 0))],
        out_shape=[jax.ShapeDtypeStruct((b, t, HG_W), BF16),
                   jax.ShapeDtypeStruct((b, N_HEADS, HEAD, HEAD), F32)],
        scratch_shapes=[pltpu.VMEM((cs, HG_W), F32)] * 4 + [pltpu.VMEM((N_HEADS, HEAD, HEAD), F32)],
        compiler_params=_cparams(("parallel", "arbitrary")),
        name="hgrn2",
    )(*args)


def _lane_blocks(cols, width):
    rows = cols[0].shape[0]
    lane = lax.broadcasted_iota(jnp.int32, (rows, width * len(cols)), 1)
    out = jnp.broadcast_to(cols[-1], (rows, width * len(cols)))
    for j in range(len(cols) - 2, -1, -1):
        out = jnp.where(lane < (j + 1) * width, cols[j], out)
    return out


def _ssd_body(*refs, t_valid, has_s0, n_sides):
    it = iter(refs)
    xbc_ref = next(it)
    z_ref = next(it)
    dts_ref = next(it)
    cw_ref = next(it)
    cb_ref = next(it)
    dtb_ref = next(it)
    alog_ref = next(it)
    dsk_ref = next(it)
    gn_ref = next(it)
    cs0_ref = next(it) if has_s0 else None
    s0_ref = next(it) if has_s0 else None
    side_in = [next(it) for _ in range(n_sides)]
    y_ref = next(it)
    st_ref = next(it)
    side_out = [next(it) for _ in range(n_sides)]
    prev_s = next(it)
    act_s = next(it)
    c = pl.program_id(1)
    cs = SSD_CHUNK
    t_blk = y_ref.shape[1]
    tail0 = 8 - (CONV_W - 1)

    @pl.when(c == 0)
    def _():
        prev_s[...] = jnp.zeros(prev_s.shape, F32)
        if has_s0:
            prev_s[tail0:8, :] = cs0_ref[0].astype(F32)
            st_ref[...] = s0_ref[...].astype(F32)
        else:
            st_ref[...] = jnp.zeros(st_ref.shape, F32)

    _do_side_casts(side_in, side_out)

    cur = _rows_of_chunk(xbc_ref, slice(0, CONV_DIM), cs)
    win = jnp.concatenate([prev_s[...], cur[0:8]], axis=0)
    conv = cb_ref[...] + cur * cw_ref[CONV_W - 1:CONV_W, :]
    head = cb_ref[...] + win[8:16] * cw_ref[CONV_W - 1:CONV_W, :]
    for d in range(1, CONV_W):
        w_d = cw_ref[CONV_W - 1 - d:CONV_W - d, :]
        conv = conv + pltpu.roll(cur, d, axis=0) * w_d
        head = head + pltpu.roll(win, d, axis=0)[8:16] * w_d
    act_s[...] = _silu(conv)
    act_s[0:8, :] = _silu(head)
    prev_s[...] = cur[cs - 8:cs]

    dt = _softplus(_rows_of_chunk(dts_ref, slice(0, 128), cs) + dtb_ref[...])
    if t_valid is not None:
        live = (c * cs + lax.broadcasted_iota(jnp.int32, (cs, 1), 0)) < t_valid
        dt = jnp.where(live, dt, 0.0)
    a = _cumsum_rows(dt * (-jnp.exp(alog_ref[...])))
    a_t = a.T
    dt_t = dt.T
    a_end = a[cs - 1:cs, :]
    w_upd = dt * jnp.exp(a_end - a)
    e_a = jnp.exp(a)
    e_end = jnp.exp(a_end)
    causal = _tri_incl(cs)
    lane_g = lax.broadcasted_iota(jnp.int32, (cs, SSM_GW), 1)
    row_g = lax.broadcasted_iota(jnp.int32, (SSM_GW, 1), 0)

    for g in range(SSM_GROUPS):
        xs = slice(g * SSM_GW, (g + 1) * SSM_GW)
        bsl = slice(SSM_W + g * SSM_N, SSM_W + (g + 1) * SSM_N)
        csl = slice(SSM_W + SSM_GROUPS * SSM_N + g * SSM_N, SSM_W + SSM_GROUPS * SSM_N + (g + 1) * SSM_N)
        xg = act_s[:, xs]
        bm = act_s[:, bsl].astype(BF16)
        cm = act_s[:, csl].astype(BF16)
        cbm = jnp.where(causal, _dot_nt(cm, bm), 0.0)
        s_g = st_ref[0, g]
        lanes = [DT_LANE0 + g * SSM_HPG + j for j in range(SSM_HPG)]
        yg = _dot_nt(cm, s_g.astype(BF16)) * _lane_blocks([e_a[:, ln:ln + 1] for ln in lanes], SSM_P)
        for j, ln in enumerate(lanes):
            diff = a[:, ln:ln + 1] - a_t[ln:ln + 1, :]
            mh = cbm * jnp.exp(jnp.minimum(diff, 0.0)) * dt_t[ln:ln + 1, :]
            xm = jnp.where(jnp.logical_and(lane_g >= j * SSM_P, lane_g < (j + 1) * SSM_P), xg, 0.0)
            yg = yg + _dot(mh.astype(BF16), xm.astype(BF16))
        xw = xg * _lane_blocks([w_upd[:, ln:ln + 1] for ln in lanes], SSM_P)
        decay = jnp.broadcast_to(e_end[:, lanes[-1]:lanes[-1] + 1], (SSM_GW, 1))
        for j in range(SSM_HPG - 2, -1, -1):
            decay = jnp.where(row_g < (j + 1) * SSM_P, e_end[:, lanes[j]:lanes[j] + 1], decay)
        st_ref[0, g] = s_g * decay + _dot_tn(xw.astype(BF16), bm)

        y = yg + dsk_ref[:, xs] * xg
        y = y * _silu(_rows_of_chunk(z_ref, xs, cs))
        ms = jnp.mean(y * y, axis=-1, keepdims=True)
        res = y * lax.rsqrt(ms + EPS) * gn_ref[:, xs]
        y_ref[0, :, xs] = res[0:t_blk].astype(y_ref.dtype)


def _ssd(proj, small, conv_w, conv_b, dt_bias, a_log, d_skip, gnorm, conv_s0, s0, *, side_casts=()):
    b, t, _ = proj.shape
    cs = SSD_CHUNK
    assert t == 1 or t % cs == 0
    t_blk, n_chunks, t_valid = (1, 1, 1) if t == 1 else (cs, t // cs, None)
    has_s0 = s0 is not None
    pad = jnp.zeros((DT_LANE0,), F32)
    tail = jnp.zeros((128 - DT_LANE0 - SSM_HEADS,), F32)
    on_dt_lanes = lambda v: jnp.concatenate([pad, v.astype(F32), tail]).reshape(1, 128)
    const = lambda shape: pl.BlockSpec(shape, lambda i, c: (0,) * len(shape))
    in_specs = [pl.BlockSpec((1, t_blk, CONV_DIM), lambda i, c: (i, c, COL_XBC // CONV_DIM)),
                pl.BlockSpec((1, t_blk, SSM_W), lambda i, c: (i, c, COL_Z // SSM_W)),
                pl.BlockSpec((1, t_blk, 128), lambda i, c: (i, c, 0)),
                const((CONV_W, CONV_DIM)), const((1, CONV_DIM)), const((1, 128)), const((1, 128)),
                const((1, SSM_W)), const((1, SSM_W))]
    args = [proj, proj, small, conv_w.astype(F32), conv_b.reshape(1, CONV_DIM).astype(F32),
            on_dt_lanes(dt_bias), on_dt_lanes(a_log),
            jnp.repeat(d_skip.astype(F32), SSM_P).reshape(1, SSM_W), gnorm.reshape(1, SSM_W).astype(F32)]
    if has_s0:
        in_specs += [pl.BlockSpec((1, CONV_W - 1, CONV_DIM), lambda i, c: (i, 0, 0)),
                     pl.BlockSpec((1, SSM_GROUPS, SSM_GW, SSM_N), lambda i, c: (i, 0, 0, 0))]
        args += [conv_s0, s0.reshape(b, SSM_GROUPS, SSM_GW, SSM_N)]
    grid = (b, n_chunks)
    side_in_specs, side_args, side_specs, side_shapes = _side_cast_specs(side_casts, grid)
    y, st, *copies = pl.pallas_call(
        functools.partial(_ssd_body, t_valid=t_valid, has_s0=has_s0, n_sides=len(side_casts)),
        grid=grid,
        in_specs=in_specs + side_in_specs,
        out_specs=[pl.BlockSpec((1, t_blk, SSM_W), lambda i, c: (i, c, 0)),
                   pl.BlockSpec((1, SSM_GROUPS, SSM_GW, SSM_N), lambda i, c: (i, 0, 0, 0))] + side_specs,
        out_shape=[jax.ShapeDtypeStruct((b, t, SSM_W), BF16),
                   jax.ShapeDtypeStruct((b, SSM_GROUPS, SSM_GW, SSM_N), F32)] + side_shapes,
        scratch_shapes=[pltpu.VMEM((8, CONV_DIM), F32), pltpu.VMEM((cs, CONV_DIM), F32)],
        compiler_params=_cparams(("arbitrary" if side_casts else "parallel", "arbitrary")),
        name="ssd",
    )(*(args + side_args))
    return (y, st.reshape(b, SSM_HEADS, SSM_P, SSM_N), *copies)


def _xattn_body(q_ref, k_ref, v_ref, gq_ref, o_ref):
    scale = HEAD ** -0.5
    g = gq_ref[...]
    for h in range(XA_HEADS):
        sl = slice(h * HEAD, (h + 1) * HEAD)
        q = q_ref[0, :, sl].astype(F32)
        ms = jnp.mean(q * q, axis=-1, keepdims=True)
        qn = q * lax.rsqrt(ms + EPS) * g
        s = _dot_nt(qn.astype(BF16), k_ref[0, :, sl].astype(BF16)) * scale
        p = jnp.exp(s - jnp.max(s, axis=-1, keepdims=True))
        o = _dot(p.astype(BF16), v_ref[0, :, sl].astype(BF16)) / jnp.sum(p, axis=-1, keepdims=True)
        o_ref[0, :, sl] = o.astype(o_ref.dtype)


def _xattn(q, mk, mv, gq):
    b, t, _ = q.shape
    n_mem = mk.shape[1]
    tq = t if t <= 512 else 512
    return pl.pallas_call(
        _xattn_body,
        grid=(b, t // tq),
        in_specs=[pl.BlockSpec((1, tq, XA_W), lambda i, j: (i, j, 0)),
                  pl.BlockSpec((1, n_mem, XA_W), lambda i, j: (i, 0, 0)),
                  pl.BlockSpec((1, n_mem, XA_W), lambda i, j: (i, 0, 0)),
                  pl.BlockSpec((1, HEAD), lambda i, j: (0, 0))],
        out_specs=pl.BlockSpec((1, tq, XA_W), lambda i, j: (i, j, 0)),
        out_shape=jax.ShapeDtypeStruct((b, t, XA_W), BF16),
        compiler_params=_cparams(("parallel", "parallel")),
        name="xattn",
    )(q, mk, mv, gq.reshape(1, HEAD).astype(F32))


IN_PIECES = (("fq", FOX_W), ("fk", FOX_W), ("fv", FOX_W), ("fg", N_HEADS), ("hg", 4 * HG_W), ("z", SSM_W),
             ("xbc", CONV_DIM), ("dt", SSM_HEADS))
IN_WIDTH = sum(width for _, width in IN_PIECES)
MAIN_COLS = dict(xbc=COL_XBC, hg=COL_HG, z=COL_Z, fq=COL_FQ, fk=COL_FK, fv=COL_FV)
W_TILE = 1024
SUBLANES = 8


def _in_piece_offsets():
    offs, o = {}, 0
    for name, width in IN_PIECES:
        offs[name] = (o, width)
        o += width
    return offs


def _w_main_tables():
    offs = _in_piece_offsets()
    a_blk, b_blk, shift = [0] * (MAIN_W // W_TILE), [0] * (MAIN_W // W_TILE), [0] * (MAIN_W // W_TILE)
    for name, dst in MAIN_COLS.items():
        src_off, width = offs[name]
        assert dst % W_TILE == 0 and width % W_TILE == 0
        for i in range(width // W_TILE):
            start = src_off + i * W_TILE
            t = dst // W_TILE + i
            a_blk[t], shift[t] = start // W_TILE, start % W_TILE
            assert shift[t] in (0, SUBLANES)
            b_blk[t] = (start - shift[t] + W_TILE) // SUBLANES if shift[t] else 0
    return a_blk, b_blk, shift


def _w_main_body(tbl_ref, a_ref, b_ref, o_ref):
    j = pl.program_id(0)

    @pl.when(tbl_ref[2, j] == 0)
    def _():
        o_ref[...] = a_ref[...].astype(BF16)

    @pl.when(tbl_ref[2, j] != 0)
    def _():
        o_ref[...] = jnp.concatenate([a_ref[SUBLANES:W_TILE, :], b_ref[...]], axis=0).astype(BF16)


def _w_small_body(g_ref, *rest):
    d_refs, o_ref = rest[:-1], rest[-1]
    used = SUBLANES * (1 + len(d_refs))
    pad = jnp.zeros((o_ref.shape[0] - used, o_ref.shape[1]), F32)
    o_ref[...] = jnp.concatenate([g_ref[...]] + [d_ref[...] for d_ref in d_refs] + [pad], axis=0).astype(BF16)


def _prep_w_in(w_in_t, layer):
    depth, width, d = w_in_t.shape
    assert width == IN_WIDTH and N_HEADS == SUBLANES
    offs = _in_piece_offsets()
    a_blk, b_blk, shift = _w_main_tables()
    tc = _pick_tile(d, (1024, 512, 256, 128))
    tables = jnp.asarray([a_blk, b_blk, shift], jnp.int32)
    main = pl.pallas_call(
        _w_main_body,
        grid_spec=pltpu.PrefetchScalarGridSpec(
            num_scalar_prefetch=1, grid=(MAIN_W // W_TILE, d // tc),
            in_specs=[pl.BlockSpec((None, W_TILE, tc), lambda j, c, tbl: (layer, tbl[0, j], c)),
                      pl.BlockSpec((None, SUBLANES, tc), lambda j, c, tbl: (layer, tbl[1, j], c))],
            out_specs=pl.BlockSpec((W_TILE, tc), lambda j, c, tbl: (j, c))),
        out_shape=jax.ShapeDtypeStruct((MAIN_W, d), BF16),
        compiler_params=_cparams(("parallel", "arbitrary")),
        name="w_in_main",
    )(tables, w_in_t, w_in_t)

    g_off, (d_off, d_rows) = offs["fg"][0], offs["dt"]
    assert g_off % SUBLANES == 0 and d_off % SUBLANES == 0 and d_rows % SUBLANES == 0
    rows8 = lambda off: pl.BlockSpec((None, SUBLANES, tc), lambda c: (layer, off // SUBLANES, c))
    n_dt = d_rows // SUBLANES
    small = pl.pallas_call(
        _w_small_body,
        grid=(d // tc,),
        in_specs=[rows8(g_off)] + [rows8(d_off + i * SUBLANES) for i in range(n_dt)],
        out_specs=pl.BlockSpec((128, tc), lambda c: (0, c)),
        out_shape=jax.ShapeDtypeStruct((128, d), BF16),
        compiler_params=_cparams(("parallel",)),
        name="w_in_small",
    )(*([w_in_t] * (1 + n_dt)))
    return main, small


def _mixers(x, lw, l, *, wb, fox_past, hg_s0, ssm_s0, conv_s0, page_table):
    b, t, d = x.shape
    x2 = x.reshape(b * t, d)
    proj2, small2 = _matmul(_rmsnorm_rows(x2, lw["norm_mix"]), lw["w_main"], trans_w=True, extra_w=lw["w_small"],
                            tm_max=1024, name="in_proj")
    proj = proj2.reshape(b, t, MAIN_W)
    small = small2.reshape(b, t, 128)

    gates_t = jnp.swapaxes(small[:, :, 0:N_HEADS], 1, 2)
    logf_t, cum_t = _fox_gate(gates_t, lw["fox_bf"])
    logf = jnp.swapaxes(logf_t, 1, 2)

    if fox_past is None:
        qn, kn, knb, fv, vb = _fox_prep(proj2, lw["fox_gq"], lw["fox_gk"])
        fo = _fox_flash(qn.reshape(b, t, FOX_W), knb.reshape(b, t, FOX_W), vb.reshape(b, t, FOX_W),
                        jnp.swapaxes(cum_t, 1, 2), cum_t)
    else:
        qn = _headnorm(proj2, lw["fox_gq"], col0=COL_FQ, width=FOX_W)
        kn = _headnorm(proj2, lw["fox_gk"], col0=COL_FK, width=FOX_W)
        fv = proj2[:, COL_FV:COL_FV + FOX_W]
        cache_k, cache_v, cache_lf = fox_past
        fo = _fox_decode(l, qn.reshape(b, N_HEADS, HEAD), kn.reshape(b, N_HEADS, HEAD),
                         fv.reshape(b, N_HEADS, HEAD), logf_t, cache_k, cache_v, cache_lf, page_table)
        fo = fo.reshape(b, t, FOX_W)

    ho, hg_state = _hgrn(proj, lw["hg_lb_logits"], lw["hg_gnorm"], hg_s0, layer=l)
    ssd_args = (proj, small, lw["conv_w"], lw["conv_b"], lw["dt_bias"], lw["a_log"], lw["d_skip"],
                lw["ssm_gnorm"], conv_s0, ssm_s0)
    if "w_out" in wb:
        sy, ssm_state = _ssd(*ssd_args)
    else:
        sy, ssm_state, wb["w_out"], wb["w_up"] = _ssd(*ssd_args, side_casts=((lw["w_out"], l), (lw["w_up"], l)))
    x_new = _out_proj(fo.reshape(b * t, FOX_W), ho.reshape(b * t, HG_W), sy.reshape(b * t, SSM_W),
                      wb["w_out"], x2).reshape(b, t, d)

    keep = CONV_W - 1
    if t >= keep:
        conv_state = proj[:, t - keep:, COL_XBC:COL_XBC + CONV_DIM]
    else:
        prev = jnp.zeros((b, keep, CONV_DIM), F32) if conv_s0 is None else conv_s0.astype(F32)
        conv_state = jnp.concatenate([prev[:, t:], proj[:, :, COL_XBC:COL_XBC + CONV_DIM]], axis=1)
    fk = kn.reshape(b, t, N_HEADS, HEAD)
    return x_new, fk, fv.reshape(b, t, N_HEADS, HEAD), logf, hg_state, ssm_state, conv_state


def _cross_and_mlp(x, lw, l, mk, mv, *, wb):
    b, t, d = x.shape
    x2 = x.reshape(b * t, d)
    q = _matmul(x2, lw["xa_wq"], gain=lw["norm_xa"], name="xa_q")
    o = _xattn(q.reshape(b, t, XA_W), mk, mv, lw["xa_gq"])
    x2, h = _proj_res_norm(o.reshape(b * t, XA_W), lw["xa_wo"], x2, lw["norm_mlp"])
    if "w_down" in wb:
        u = _matmul(h, wb["w_up"], act="relu2", out_dtype=BF16, tm_max=1024, name="mlp_up")
    else:
        u, wb["w_down"] = _matmul(h, wb["w_up"], act="relu2", out_dtype=BF16, tm_max=1024,
                                  side_casts=((lw["w_down"], l),), name="mlp_up")
    x2 = _matmul(u, wb["w_down"], res=x2, tm_max=1024, tk=2048, name="mlp_down")
    return x2.reshape(b, t, d)


def kernel(x_prompt, x_sample, cache_fox_k, cache_fox_v, cache_fox_logf, cache_mem_k, cache_mem_v, state_hgrn, state_ssm, state_conv, page_table, mem_prompt, norm_mix, w_in, fox_gq, fox_gk, fox_bf, hg_lb_logits, hg_gnorm, conv_w, conv_b, dt_bias, a_log, d_skip, ssm_gnorm, w_out, norm_xa, norm_mem, xa_wq, xa_wk, xa_wv, xa_gq, xa_gk, xa_wo, norm_mlp, w_up, w_down):
    depth = w_in.shape[0]
    bp = x_prompt.shape[0]
    n_mem = mem_prompt.shape[1]
    pool = cache_fox_k.shape[1]
    cache_k = cache_fox_k.reshape(depth, pool, PAGE * N_HEADS, HEAD)
    cache_v = cache_fox_v.reshape(depth, pool, PAGE * N_HEADS, HEAD)
    cache_lf = cache_fox_logf.reshape(depth, pool, 1, PAGE * N_HEADS)
    mem2 = mem_prompt.reshape(bp * n_mem, -1)
    w_in_t = jnp.swapaxes(w_in, 1, 2)

    xp, xs = x_prompt, x_sample
    outs = {k: [] for k in ("p_fk", "p_fv", "p_fl", "p_hg", "p_ss", "p_cv", "p_mk", "p_mv",
                            "s_fk", "s_fv", "s_fl", "s_hg", "s_ss", "s_cv")}
    for l in range(depth):
        w_main, w_small = _prep_w_in(w_in_t, l)
        lw = dict(w_main=w_main, w_small=w_small, norm_mix=norm_mix[l],
                  fox_gq=fox_gq[l], fox_gk=fox_gk[l], fox_bf=fox_bf[l], hg_lb_logits=hg_lb_logits,
                  hg_gnorm=hg_gnorm[l], conv_w=conv_w[l], conv_b=conv_b[l], dt_bias=dt_bias[l], a_log=a_log[l],
                  d_skip=d_skip[l], ssm_gnorm=ssm_gnorm[l], w_out=w_out, norm_xa=norm_xa[l],
                  xa_wq=xa_wq[l].astype(BF16), xa_gq=xa_gq[l], xa_wo=xa_wo[l].astype(BF16),
                  norm_mlp=norm_mlp[l], w_up=w_up, w_down=w_down)

        wb = {}
        xp, fk, fv, fl, hg, ss, cv = _mixers(xp, lw, l, wb=wb, fox_past=None, hg_s0=None, ssm_s0=None,
                                             conv_s0=None, page_table=None)
        w_kv = jnp.concatenate([xa_wk[l], xa_wv[l]], axis=1).astype(BF16)
        kv = _matmul(mem2, w_kv, gain=norm_mem[l], name="mem_kv")
        mk = _headnorm(kv, xa_gk[l], col0=0, width=XA_W).reshape(bp, n_mem, XA_W)
        mv = kv[:, XA_W:].reshape(bp, n_mem, XA_W)
        xp = _cross_and_mlp(xp, lw, l, mk, mv, wb=wb)
        for key, val in zip(("p_fk", "p_fv", "p_fl", "p_hg", "p_ss", "p_cv"), (fk, fv, fl, hg, ss, cv)):
            outs[key].append(val)
        outs["p_mk"].append(mk.reshape(bp, n_mem, XA_HEADS, HEAD))
        outs["p_mv"].append(mv.reshape(bp, n_mem, XA_HEADS, HEAD))

        bs = xs.shape[0]
        xs, fk, fv, fl, hg, ss, cv = _mixers(xs, lw, l, wb=wb, fox_past=(cache_k, cache_v, cache_lf),
                                             hg_s0=state_hgrn[l], ssm_s0=state_ssm[l],
                                             conv_s0=state_conv[l], page_table=page_table)
        xs = _cross_and_mlp(xs, lw, l, cache_mem_k[l].reshape(bs, n_mem, XA_W),
                            cache_mem_v[l].reshape(bs, n_mem, XA_W), wb=wb)
        for key, val in zip(("s_fk", "s_fv", "s_fl", "s_hg", "s_ss", "s_cv"), (fk, fv, fl, hg, ss, cv)):
            outs[key].append(val)

    st = {k: jnp.stack(v) for k, v in outs.items()}
    return (xp, xs, st["p_fk"], st["p_fv"], st["p_fl"], st["p_hg"], st["p_ss"], st["p_cv"], st["p_mk"],
            st["p_mv"], st["s_fk"], st["s_fv"], st["s_fl"], st["s_hg"], st["s_ss"], st["s_cv"])
```
